```python
import math
import jax, jax.numpy as jnp
from jax import lax
import numpy as np

D_MODEL = 1024
BATCH = 2
SEQ = 8192
DEPTH = 1

RET_HEADS = 8
RET_HEAD_DIM = 128
RET_WIDTH = RET_HEADS * RET_HEAD_DIM
RET_CHUNK = 128
ROPE_BASE = 10000.0
GN_EPS = 1e-5
POOL_WINDOWS = (2, 4, 8, 16)
POOL_GROUPS = len(POOL_WINDOWS)
POOL_WIDTH = D_MODEL
POOL_GROUP = POOL_WIDTH // POOL_GROUPS
N_BRANCHES = 2
IN_WIDTH = 4 * RET_WIDTH + POOL_WIDTH + N_BRANCHES * D_MODEL
N_EXPERTS = 32
TOP_K = 4
D_FF = D_MODEL
SWIGLU_ALPHA = 1.702
SWIGLU_LIMIT = 7.0
EXPERT_BLOCK = 128
PLE_DIM = 256
RMS_EPS = 1e-6

kernel_name = "hybrid_retention_pool_moe_block"


def rms_norm(x, g):
    xf = x.astype(jnp.float32)
    y = xf * lax.rsqrt(jnp.mean(xf * xf, axis=-1, keepdims=True) + RMS_EPS)
    return (y * g.astype(jnp.float32)).astype(x.dtype)


def rotary(x, positions):
    half = x.shape[-1] // 2
    inv = ROPE_BASE ** (-jnp.arange(half, dtype=jnp.float32) / half)
    ang = positions.astype(jnp.float32)[..., None] * inv
    cos = jnp.cos(ang)[:, :, None, :]
    sin = jnp.sin(ang)[:, :, None, :]
    xf = x.astype(jnp.float32)
    x1, x2 = xf[..., :half], xf[..., half:]
    return jnp.concatenate([x1 * cos - x2 * sin, x2 * cos + x1 * sin], axis=-1)


def retention_chunkwise(q, k, v):
    B, S, H, Dh = q.shape
    N = S // RET_CHUNK
    log_gamma = jnp.log1p(-jnp.exp2(-5.0 - jnp.arange(H, dtype=jnp.float32)))
    qc = q.reshape(B, N, RET_CHUNK, H, Dh)
    kc = k.reshape(B, N, RET_CHUNK, H, Dh)
    vc = v.reshape(B, N, RET_CHUNK, H, Dh)
    idx = jnp.arange(RET_CHUNK, dtype=jnp.float32)
    diff = idx[:, None] - idx[None, :]
    causal = diff >= 0
    dmat = jnp.where(causal[None], jnp.exp(log_gamma[:, None, None] * jnp.where(causal, diff, 0.0)[None]), 0.0)
    scores = jnp.einsum('bnihd,bnjhd->bnhij', qc, kc) * dmat
    inner = jnp.einsum('bnhij,bnjhe->bnihe', scores, vc)
    zeta = jnp.exp(log_gamma[:, None] * (RET_CHUNK - 1.0 - idx)[None])
    kv = jnp.einsum('bnjhd,hj,bnjhe->nbhde', kc, zeta, vc)
    chunk_decay = jnp.exp(log_gamma * RET_CHUNK)[None, :, None, None]

    def step(state, kv_n):
        return state * chunk_decay + kv_n, state

    _, prev = lax.scan(step, jnp.zeros((B, H, Dh, Dh), q.dtype), kv)
    xi = jnp.exp(log_gamma[:, None] * (idx + 1.0)[None])
    cross = jnp.einsum('bnihd,nbhde,hi->bnihe', qc, prev, xi)
    return (inner + cross).reshape(B, S, H, Dh)


def pool_mixer(u, w_pool, pool_scale):
    B, S, C = u.shape
    uf = u.astype(jnp.float32)
    csum = jnp.cumsum(uf, axis=1)
    pos = jnp.arange(S)
    outs = []
    for gi, w in enumerate(POOL_WINDOWS):
        sl = slice(gi * POOL_GROUP, (gi + 1) * POOL_GROUP)
        c = csum[..., sl]
        lagged = jnp.pad(c, ((0, 0), (w, 0), (0, 0)))[:, :S]
        count = jnp.minimum(pos + 1, w).astype(jnp.float32)[None, :, None]
        mixed = ((c - lagged) / count - uf[..., sl]).astype(u.dtype)
        outs.append(jnp.einsum('bsc,cd->bsd', mixed, w_pool[gi]))
    return jnp.concatenate(outs, axis=-1) * pool_scale


def swiglu_clamped(gu):
    gate, up = gu[..., :D_FF], gu[..., D_FF:]
    gate = jnp.minimum(gate, SWIGLU_LIMIT)
    up = jnp.clip(up, -SWIGLU_LIMIT, SWIGLU_LIMIT)
    return (up + 1.0) * (gate * jax.nn.sigmoid(gate * SWIGLU_ALPHA))


def moe(h, w_router, b_router, w_gate_up, b_gate_up, w_down, b_down):
    B, S, D = h.shape
    T = B * S
    A = T * TOP_K
    xt = h.reshape(T, D)
    logits = (xt @ w_router + b_router).astype(jnp.float32)
    top_val, top_idx = lax.top_k(logits, TOP_K)
    top_w = jax.nn.softmax(top_val, axis=-1)
    flat_e = top_idx.reshape(A)
    flat_tok = (jnp.arange(A, dtype=jnp.int32) // TOP_K)
    flat_w = top_w.reshape(A)
    order = jnp.argsort(flat_e)
    sorted_e = flat_e[order]
    counts = jnp.bincount(flat_e, length=N_EXPERTS)
    padded = (counts + EXPERT_BLOCK - 1) // EXPERT_BLOCK * EXPERT_BLOCK
    starts = jnp.cumsum(counts) - counts
    pends = jnp.cumsum(padded)
    pstarts = pends - padded
    rank = jnp.arange(A, dtype=jnp.int32) - starts[sorted_e]
    dest = pstarts[sorted_e] + rank
    n_blocks = (A + EXPERT_BLOCK - 1) // EXPERT_BLOCK + N_EXPERTS
    P = n_blocks * EXPERT_BLOCK
    slot_tok = jnp.full((P,), T, jnp.int32).at[dest].set(flat_tok[order])
    slot_w = jnp.zeros((P,), jnp.float32).at[dest].set(flat_w[order])
    block_e = jnp.clip(jnp.searchsorted(pends, jnp.arange(n_blocks) * EXPERT_BLOCK, side='right'), 0, N_EXPERTS - 1)
    x_pad = jnp.concatenate([xt, jnp.zeros((1, D), xt.dtype)], axis=0)
    xb = x_pad[slot_tok].reshape(n_blocks, EXPERT_BLOCK, D)

    def expert_block(args):
        xblk, e = args
        act = swiglu_clamped(xblk @ w_gate_up[e] + b_gate_up[e])
        return act @ w_down[e] + b_down[e]

    yb = lax.map(expert_block, (xb, block_e)).reshape(P, D)
    yb = yb * slot_w[:, None].astype(yb.dtype)
    y = jnp.zeros((T + 1, D), yb.dtype).at[slot_tok].add(yb)[:T]
    return y.reshape(B, S, D)


def setup_inputs(seed: int = 0) -> dict:
    key = jax.random.key(seed)
    ks = jax.random.split(key, 24)
    f32 = jnp.float32
    nrm = lambda k, shape, scale: jax.random.normal(k, shape, f32) * scale
    L = DEPTH
    return {
        "x": nrm(ks[0], (BATCH, SEQ, D_MODEL), 1.0),
        "p": nrm(ks[1], (DEPTH, BATCH, SEQ, PLE_DIM), 1.0),
        "positions": jnp.broadcast_to(jnp.arange(SEQ, dtype=jnp.int32)[None], (BATCH, SEQ)),
        "g_mix_norm": 1.0 + nrm(ks[2], (L, D_MODEL), 0.05),
        "w_in": nrm(ks[3], (L, D_MODEL, IN_WIDTH), D_MODEL ** -0.5),
        "g_ret_norm": 1.0 + nrm(ks[4], (L, RET_WIDTH), 0.05),
        "w_pool": nrm(ks[5], (L, POOL_GROUPS, POOL_GROUP, POOL_GROUP), POOL_GROUP ** -0.5),
        "pool_scale": 1.0 + nrm(ks[6], (L, POOL_WIDTH), 0.05),
        "w_branch": nrm(ks[7], (L, N_BRANCHES, RET_WIDTH, D_MODEL), RET_WIDTH ** -0.5),
        "w_out": nrm(ks[8], (L, D_MODEL, D_MODEL), D_MODEL ** -0.5),
        "g_ffn_norm": 1.0 + nrm(ks[9], (L, D_MODEL), 0.05),
        "w_router": nrm(ks[10], (L, D_MODEL, N_EXPERTS), D_MODEL ** -0.5),
        "b_router": nrm(ks[11], (L, N_EXPERTS), 0.01),
        "w_gate_up": nrm(ks[12], (L, N_EXPERTS, D_MODEL, 2 * D_FF), D_MODEL ** -0.5),
        "b_gate_up": nrm(ks[13], (L, N_EXPERTS, 2 * D_FF), 0.01),
        "w_down": nrm(ks[14], (L, N_EXPERTS, D_FF, D_MODEL), D_FF ** -0.5),
        "b_down": nrm(ks[15], (L, N_EXPERTS, D_MODEL), 0.01),
        "g_ple_norm": 1.0 + nrm(ks[16], (L, D_MODEL), 0.05),
        "w_ple_gate": nrm(ks[17], (L, D_MODEL, D_MODEL), D_MODEL ** -0.5),
        "w_ple_proj": nrm(ks[18], (L, PLE_DIM, D_MODEL), PLE_DIM ** -0.5),
        "g_final": 1.0 + nrm(ks[19], (D_MODEL,), 0.05),
    }


def reference(x, p, positions, g_mix_norm, w_in, g_ret_norm, w_pool, pool_scale, w_branch, w_out,
              g_ffn_norm, w_router, b_router, w_gate_up, b_gate_up, w_down, b_down,
              g_ple_norm, w_ple_gate, w_ple_proj, g_final):
    B, S, D = x.shape
    for i in range(DEPTH):
        h = rms_norm(x, g_mix_norm[i])
        proj = h @ w_in[i]
        q, k, v, g_ret, u_pool, gate_a, gate_b = jnp.split(
            proj, np.cumsum([RET_WIDTH] * 4 + [POOL_WIDTH, D_MODEL]).tolist(), axis=-1)
        qh = rotary(q.reshape(B, S, RET_HEADS, RET_HEAD_DIM), positions)
        kh = rotary(k.reshape(B, S, RET_HEADS, RET_HEAD_DIM), positions) * (RET_HEAD_DIM ** -0.5)
        vh = v.reshape(B, S, RET_HEADS, RET_HEAD_DIM).astype(jnp.float32)
        ret = retention_chunkwise(qh, kh, vh)
        mu = jnp.mean(ret, axis=-1, keepdims=True)
        var = jnp.mean(jnp.square(ret - mu), axis=-1, keepdims=True)
        ret = ((ret - mu) * lax.rsqrt(var + GN_EPS)).reshape(B, S, RET_WIDTH)
        ret = (ret * g_ret_norm[i].astype(jnp.float32)).astype(x.dtype)
        y_ret = jax.nn.silu(g_ret) * ret
        y_pool = pool_mixer(u_pool, w_pool[i], pool_scale[i])
        branch_a = y_ret @ w_branch[i, 0]
        branch_b = y_pool @ w_branch[i, 1]
        merged = jax.nn.sigmoid(gate_a) * branch_a + jax.nn.sigmoid(gate_b) * branch_b
        x = x + merged @ w_out[i]
        h2 = rms_norm(x, g_ffn_norm[i])
        x = x + moe(h2, w_router[i], b_router[i], w_gate_up[i], b_gate_up[i], w_down[i], b_down[i])
        h3 = rms_norm(x, g_ple_norm[i])
        ple_gate = jax.nn.sigmoid(h3 @ w_ple_gate[i])
        x = x + ple_gate * (p[i].astype(x.dtype) @ w_ple_proj[i])
    return rms_norm(x, g_final)
```

```python
import functools

import numpy as np
import jax
import jax.numpy as jnp
from jax import lax
from jax.experimental import pallas as pl
from jax.experimental.pallas import tpu as pltpu

F32 = jnp.float32
BF16 = jnp.bfloat16
I32 = jnp.int32

RET_HEADS = 8
RET_HEAD_DIM = 128
ROPE_BASE = 10000.0
GN_EPS = 1e-5
RMS_EPS = 1e-6
POOL_WINDOWS = (2, 4, 8, 16)
N_EXPERTS = 32
TOP_K = 4
SWIGLU_ALPHA = 1.702
SWIGLU_LIMIT = 7.0

LANES = 128
VMEM_LIMIT_BYTES = 56 * 1024 * 1024

MIX_TILE = 256
ROPE_TILE = 1024
DISP_TILE = 512
EXPERT_TILE = 256
COMB_TILE = 256
POOL_HALO = 16


def _const_spec(shape):
    nd = len(shape)
    return pl.BlockSpec(shape, lambda *_: (0,) * nd, pipeline_mode=pl.Buffered(1))


def _rms(x, g):
    return x * lax.rsqrt(jnp.mean(x * x, axis=-1, keepdims=True) + RMS_EPS) * g


def _dot(a, b):
    return jnp.dot(a, b, preferred_element_type=F32)


def _dot_nt(a, b, precision=None):
    return lax.dot_general(a, b, (((1,), (1,)), ((), ())),
                           preferred_element_type=F32, precision=precision)


def _dot_tn(a, b):
    return lax.dot_general(a, b, (((0,), (0,)), ((), ())), preferred_element_type=F32)


def _rope_body(pos_ref, inv_ref, sign_ref, cos_ref, sin_ref):
    ang = pos_ref[...].astype(F32) * inv_ref[...]
    cos_ref[...] = jnp.cos(ang)
    sin_ref[...] = jnp.sin(ang) * sign_ref[...]


def _rope_tables(pos_col):
    T = pos_col.shape[0]
    half = RET_HEAD_DIM // 2
    inv = ROPE_BASE ** (-jnp.arange(half, dtype=F32) / half)
    inv_full = jnp.concatenate([inv, inv]).reshape(1, RET_HEAD_DIM)
    sign = jnp.concatenate([-jnp.ones((half,), F32), jnp.ones((half,), F32)]).reshape(1, RET_HEAD_DIM)
    tile = min(ROPE_TILE, T)
    return pl.pallas_call(
        _rope_body,
        grid=(T // tile,),
        in_specs=[pl.BlockSpec((tile, 1), lambda i: (i, 0)),
                  pl.BlockSpec((1, RET_HEAD_DIM), lambda i: (0, 0)),
                  pl.BlockSpec((1, RET_HEAD_DIM), lambda i: (0, 0))],
        out_specs=[pl.BlockSpec((tile, RET_HEAD_DIM), lambda i: (i, 0)),
                   pl.BlockSpec((tile, RET_HEAD_DIM), lambda i: (i, 0))],
        out_shape=[jax.ShapeDtypeStruct((T, RET_HEAD_DIM), F32),
                   jax.ShapeDtypeStruct((T, RET_HEAD_DIM), F32)],
        name="rope_tables",
    )(pos_col, inv_full, sign)


def _retention_constants(tile):
    h = np.arange(RET_HEADS, dtype=np.float64)
    log_gamma = np.log1p(-np.exp2(-5.0 - h))
    idx = np.arange(tile, dtype=np.float64)
    diff = idx[:, None] - idx[None, :]
    dmat = np.where(diff >= 0, np.exp(log_gamma[:, None, None] * np.maximum(diff, 0.0)[None]), 0.0)
    xi = np.exp(log_gamma[:, None] * (idx + 1.0)[None])
    zeta = np.exp(log_gamma[:, None] * (tile - 1.0 - idx)[None])
    chunk_decay = np.exp(log_gamma * tile)
    xi_b = np.broadcast_to(xi[:, :, None], (RET_HEADS, tile, RET_HEAD_DIM))
    zeta_b = np.broadcast_to(zeta[:, :, None], (RET_HEADS, tile, RET_HEAD_DIM))
    return (jnp.asarray(dmat, F32), jnp.asarray(xi_b, F32), jnp.asarray(zeta_b, F32),
            tuple(float(c) for c in chunk_decay))


def _mix_body(x_ref, cos_ref, sin_ref, gmix_ref, win_ref, dmat_ref, xi_ref, zeta_ref, gret_ref,
              wpool_ref, pscale_ref, wbr_ref, wout_ref, gffn_ref, wrt_ref, br_ref, tri_ref,
              x1_ref, h2_ref, idx_ref, w_ref, rank_ref, cnt_ref,
              state_ref, ue_ref, run_ref, *, tiles_per_seq, chunk_decay):
    tm, d_model = x_ref.shape
    ret_width = RET_HEADS * RET_HEAD_DIM
    i = pl.program_id(0)
    seq_tile = i % tiles_per_seq

    @pl.when(seq_tile == 0)
    def _():
        state_ref[...] = jnp.zeros(state_ref.shape, F32)
        ue_ref[0:POOL_HALO, :] = jnp.zeros((POOL_HALO, ue_ref.shape[1]), F32)

    @pl.when(i == 0)
    def _():
        run_ref[...] = jnp.zeros(run_ref.shape, F32)

    x = x_ref[...]
    hb = _rms(x, gmix_ref[...]).astype(BF16)

    qkvg = _dot(hb, win_ref[:, 0:4 * ret_width])
    cos = cos_ref[...]
    sin = sin_ref[...]

    def rot(a):
        return a * cos + pltpu.roll(a, RET_HEAD_DIM // 2, 1) * sin

    ys = []
    for h in range(RET_HEADS):
        lo = h * RET_HEAD_DIM
        q = rot(qkvg[:, lo:lo + RET_HEAD_DIM]).astype(BF16)
        k = (rot(qkvg[:, ret_width + lo:ret_width + lo + RET_HEAD_DIM])
             * (RET_HEAD_DIM ** -0.5)).astype(BF16)
        v = qkvg[:, 2 * ret_width + lo:2 * ret_width + lo + RET_HEAD_DIM]
        g = qkvg[:, 3 * ret_width + lo:3 * ret_width + lo + RET_HEAD_DIM]
        scores = _dot_nt(q, k) * dmat_ref[h]
        inner = _dot(scores.astype(BF16), v.astype(BF16))
        st = state_ref[h]
        cross = _dot(q, st.astype(BF16)) * xi_ref[h]
        state_ref[h] = st * chunk_decay[h] + _dot_tn(k, (v * zeta_ref[h]).astype(BF16))
        ret = inner + cross
        mu = jnp.mean(ret, axis=-1, keepdims=True)
        dev = ret - mu
        var = jnp.mean(dev * dev, axis=-1, keepdims=True)
        rn = dev * lax.rsqrt(var + GN_EPS) * gret_ref[:, lo:lo + RET_HEAD_DIM]
        ys.append(((g * jax.nn.sigmoid(g)) * rn).astype(BF16))
    y_ret = jnp.concatenate(ys, axis=1)
    branch_a = _dot(y_ret, wbr_ref[0])

    u = _dot(hb, win_ref[:, 4 * ret_width:4 * ret_width + d_model])
    ue_ref[POOL_HALO:POOL_HALO + tm, :] = u
    pos = seq_tile * tm + lax.broadcasted_iota(I32, (tm, 1), 0)
    group = d_model // len(POOL_WINDOWS)
    outs = []
    for gi, w in enumerate(POOL_WINDOWS):
        c0 = gi * group
        ug = u[:, c0:c0 + group]
        acc = ug
        for s in range(1, w):
            acc = acc + ue_ref[POOL_HALO - s:POOL_HALO - s + tm, c0:c0 + group]
        inv_count = 1.0 / jnp.minimum(pos + 1, w).astype(F32)
        mixed = (acc * inv_count - ug).astype(BF16)
        outs.append(_dot(mixed, wpool_ref[gi]))
    ue_ref[0:POOL_HALO, :] = ue_ref[tm:tm + POOL_HALO, :]
    y_pool = (jnp.concatenate(outs, axis=1) * pscale_ref[...]).astype(BF16)
    branch_b = _dot(y_pool, wbr_ref[1])

    gates = _dot(hb, win_ref[:, 4 * ret_width + d_model:4 * ret_width + 3 * d_model])
    merged = (jax.nn.sigmoid(gates[:, 0:d_model]) * branch_a
              + jax.nn.sigmoid(gates[:, d_model:2 * d_model]) * branch_b)
    x1 = x + _dot(merged.astype(BF16), wout_ref[...])
    x1_ref[...] = x1

    h2 = _rms(x1, gffn_ref[...])
    h2_ref[...] = h2
    logits = _dot_nt(wrt_ref[...], h2, precision=lax.Precision.HIGHEST) + br_ref[...]
    n_exp = logits.shape[0]
    eiota = lax.broadcasted_iota(I32, (n_exp, tm), 0)
    vals, idxs = [], []
    l = logits
    for _ in range(TOP_K):
        m = jnp.max(l, axis=0, keepdims=True)
        sel = jnp.min(jnp.where(l == m, eiota, n_exp), axis=0, keepdims=True)
        vals.append(m)
        idxs.append(sel)
        l = jnp.where(eiota == sel, -jnp.inf, l)
    exps = [jnp.exp(v - vals[0]) for v in vals]
    denom = exps[0] + exps[1] + exps[2] + exps[3]
    inv_denom = 1.0 / denom
    onehot = jnp.zeros((n_exp, tm), F32)
    for kk in range(TOP_K):
        onehot = onehot + (eiota == idxs[kk]).astype(F32)
    base = _dot(onehot.astype(BF16), tri_ref[...]) + run_ref[:, 0:1]
    for kk in range(TOP_K):
        idx_ref[kk:kk + 1, :] = idxs[kk]
        w_ref[kk:kk + 1, :] = exps[kk] * inv_denom
        rank = jnp.sum(jnp.where(eiota == idxs[kk], base, 0.0), axis=0, keepdims=True)
        rank_ref[kk:kk + 1, :] = rank.astype(I32)
    run = run_ref[...] + jnp.sum(onehot, axis=1, keepdims=True)
    run_ref[...] = run
    cnt_ref[...] = run.astype(I32)


def _token_mix(x2d, cos, sin, g_mix, w_in, g_ret, w_pool, pool_scale, w_branch, w_out,
               g_ffn, w_router, b_router, seq_len):
    T, D = x2d.shape
    tm = min(MIX_TILE, seq_len)
    in_width = w_in.shape[1]
    ret_width = RET_HEADS * RET_HEAD_DIM
    dmat, xi_b, zeta_b, chunk_decay = _retention_constants(tm)
    tri = jnp.asarray(np.triu(np.ones((tm, tm), np.float32), 1), BF16)
    row = lambda a: a.reshape(1, -1)
    tile_spec = lambda w: pl.BlockSpec((tm, w), lambda i: (i, 0))
    top_spec = pl.BlockSpec((TOP_K, tm), lambda i: (0, i))
    body = functools.partial(_mix_body, tiles_per_seq=seq_len // tm, chunk_decay=chunk_decay)
    return pl.pallas_call(
        body,
        grid=(T // tm,),
        in_specs=[tile_spec(D), tile_spec(RET_HEAD_DIM), tile_spec(RET_HEAD_DIM),
                  _const_spec((1, D)), _const_spec((D, in_width)),
                  _const_spec(dmat.shape), _const_spec(xi_b.shape), _const_spec(zeta_b.shape),
                  _const_spec((1, ret_width)), _const_spec(w_pool.shape), _const_spec((1, D)),
                  _const_spec(w_branch.shape), _const_spec((D, D)), _const_spec((1, D)),
                  _const_spec((N_EXPERTS, D)), _const_spec((N_EXPERTS, 1)), _const_spec((tm, tm))],
        out_specs=[tile_spec(D), tile_spec(D), top_spec, top_spec, top_spec,
                   pl.BlockSpec((N_EXPERTS, LANES), lambda i: (0, 0))],
        out_shape=[jax.ShapeDtypeStruct((T, D), F32), jax.ShapeDtypeStruct((T, D), F32),
                   jax.ShapeDtypeStruct((TOP_K, T), I32), jax.ShapeDtypeStruct((TOP_K, T), F32),
                   jax.ShapeDtypeStruct((TOP_K, T), I32),
                   jax.ShapeDtypeStruct((N_EXPERTS, LANES), I32)],
        scratch_shapes=[pltpu.VMEM((RET_HEADS, RET_HEAD_DIM, RET_HEAD_DIM), F32),
                        pltpu.VMEM((tm + POOL_HALO, D), F32),
                        pltpu.VMEM((N_EXPERTS, LANES), F32)],
        compiler_params=pltpu.CompilerParams(dimension_semantics=("arbitrary",),
                                             vmem_limit_bytes=VMEM_LIMIT_BYTES),
        name="token_mix",
    )(x2d, cos, sin, row(g_mix), w_in.astype(BF16), dmat, xi_b, zeta_b, row(g_ret),
      w_pool.astype(BF16), row(pool_scale), w_branch.astype(BF16), w_out.astype(BF16),
      row(g_ffn), w_router.T, b_router.reshape(-1, 1), tri)


def _disp_body(pstart_ref, zstart_ref, h2_ref, idx_ref, rank_ref, xs_ref, zbuf_ref, sem, zsem):
    td = h2_ref.shape[0]
    te = zbuf_ref.shape[0]

    def zero_copy(e):
        start = pl.multiple_of(jnp.maximum(zstart_ref[e], 0), te)
        return pltpu.make_async_copy(zbuf_ref, xs_ref.at[pl.ds(start, te)], zsem)

    @pl.when(pl.program_id(0) == 0)
    def _():
        zbuf_ref[...] = jnp.zeros(zbuf_ref.shape, F32)

        def start(e, c):
            @pl.when(zstart_ref[e] >= 0)
            def _():
                zero_copy(e).start()
            return c

        def wait(e, c):
            @pl.when(zstart_ref[e] >= 0)
            def _():
                zero_copy(e).wait()
            return c

        lax.fori_loop(0, zstart_ref.shape[0], start, 0)
        lax.fori_loop(0, zstart_ref.shape[0], wait, 0)

    def row_copy(t, dest):
        return pltpu.make_async_copy(h2_ref.at[pl.ds(t, 1)], xs_ref.at[pl.ds(dest, 1)], sem)

    def start(t, c):
        for kk in range(TOP_K):
            row_copy(t, pstart_ref[idx_ref[kk, t]] + rank_ref[kk, t]).start()
        return c

    def wait(t, c):
        for kk in range(TOP_K):
            row_copy(t, 0).wait()
        return c

    lax.fori_loop(0, td, start, 0)
    lax.fori_loop(0, td, wait, 0)


def _moe_dispatch(h2, idx_t, rank_t, pstart, zstart, n_rows):
    T, D = h2.shape
    td = min(DISP_TILE, T)
    smem_spec = pl.BlockSpec((TOP_K, td), lambda i, *_: (0, i), memory_space=pltpu.SMEM)
    return pl.pallas_call(
        _disp_body,
        grid_spec=pltpu.PrefetchScalarGridSpec(
            num_scalar_prefetch=2,
            grid=(T // td,),
            in_specs=[pl.BlockSpec((td, D), lambda i, *_: (i, 0)), smem_spec, smem_spec],
            out_specs=pl.BlockSpec(memory_space=pl.ANY),
            scratch_shapes=[pltpu.VMEM((EXPERT_TILE, D), F32),
                            pltpu.SemaphoreType.DMA, pltpu.SemaphoreType.DMA]),
        out_shape=jax.ShapeDtypeStruct((n_rows, D), F32),
        compiler_params=pltpu.CompilerParams(dimension_semantics=("arbitrary",),
                                             vmem_limit_bytes=VMEM_LIMIT_BYTES),
        name="moe_dispatch",
    )(pstart, zstart, h2, idx_t, rank_t)


def _expert_body(blk_e_ref, blk_src_ref, nused_ref, xs_ref, wgu_ref, bgu_ref, wd_ref, bd_ref,
                 yb_ref, wgu_bf_ref, wd_bf_ref):
    i = pl.program_id(0)
    d_ff = wd_ref.shape[0]
    prev_e = blk_e_ref[jnp.maximum(i - 1, 0)]

    @pl.when((i == 0) | (blk_e_ref[i] != prev_e))
    def _():
        wgu_bf_ref[...] = wgu_ref[...].astype(BF16)
        wd_bf_ref[...] = wd_ref[...].astype(BF16)

    @pl.when(i < nused_ref[0])
    def _():
        gu = _dot(xs_ref[...].astype(BF16), wgu_bf_ref[...]) + bgu_ref[...]
        gate = jnp.minimum(gu[:, 0:d_ff], SWIGLU_LIMIT)
        up = jnp.clip(gu[:, d_ff:2 * d_ff], -SWIGLU_LIMIT, SWIGLU_LIMIT)
        act = (up + 1.0) * (gate * jax.nn.sigmoid(gate * SWIGLU_ALPHA))
        yb_ref[...] = _dot(act.astype(BF16), wd_bf_ref[...]) + bd_ref[...]

    @pl.when(i >= nused_ref[0])
    def _():
        yb_ref[...] = jnp.zeros(yb_ref.shape, F32)


def _moe_experts(xs, blk_e, blk_src, nused, w_gate_up, b_gate_up, w_down, b_down):
    n_rows, D = xs.shape
    E, _, two_ff = w_gate_up.shape
    d_ff = two_ff // 2
    te = EXPERT_TILE
    return pl.pallas_call(
        _expert_body,
        grid_spec=pltpu.PrefetchScalarGridSpec(
            num_scalar_prefetch=3,
            grid=(n_rows // te,),
            in_specs=[pl.BlockSpec((te, D), lambda i, be, bs, nu: (bs[i], 0)),
                      pl.BlockSpec((None, D, two_ff), lambda i, be, bs, nu: (be[i], 0, 0)),
                      pl.BlockSpec((None, 1, two_ff), lambda i, be, bs, nu: (be[i], 0, 0)),
                      pl.BlockSpec((None, d_ff, D), lambda i, be, bs, nu: (be[i], 0, 0)),
                      pl.BlockSpec((None, 1, D), lambda i, be, bs, nu: (be[i], 0, 0))],
            out_specs=pl.BlockSpec((te, D), lambda i, be, bs, nu: (i, 0)),
            scratch_shapes=[pltpu.VMEM((D, two_ff), BF16), pltpu.VMEM((d_ff, D), BF16)]),
        out_shape=jax.ShapeDtypeStruct((n_rows, D), F32),
        compiler_params=pltpu.CompilerParams(dimension_semantics=("arbitrary",),
                                             vmem_limit_bytes=VMEM_LIMIT_BYTES),
        name="moe_experts",
    )(blk_e, blk_src, nused, xs, w_gate_up, b_gate_up.reshape(E, 1, two_ff),
      w_down, b_down.reshape(E, 1, D))


def _comb_body(pstart_ref, idx_ref, rank_ref, yb_ref, w_ref, x1_ref, p_ref, gple_ref, wpg_ref,
               wpp_ref, gfin_ref, out_ref, gbuf_ref, sem):
    tc = x1_ref.shape[0]

    def row_copy(kk, t, src):
        return pltpu.make_async_copy(yb_ref.at[pl.ds(src, 1)], gbuf_ref.at[kk, pl.ds(t, 1)], sem)

    def start(t, c):
        for kk in range(TOP_K):
            row_copy(kk, t, pstart_ref[idx_ref[kk, t]] + rank_ref[kk, t]).start()
        return c

    def wait(t, c):
        for kk in range(TOP_K):
            row_copy(kk, t, 0).wait()
        return c

    lax.fori_loop(0, tc, start, 0)

    w_pad = jnp.concatenate([w_ref[...], jnp.zeros((LANES - TOP_K, tc), F32)], axis=0)
    w_col = w_pad.T
    pp = _dot(p_ref[...].astype(BF16), wpp_ref[...])

    lax.fori_loop(0, tc, wait, 0)

    moe = w_col[:, 0:1] * gbuf_ref[0]
    for kk in range(1, TOP_K):
        moe = moe + w_col[:, kk:kk + 1] * gbuf_ref[kk]
    x2 = x1_ref[...] + moe
    h3 = _rms(x2, gple_ref[...]).astype(BF16)
    gate = jax.nn.sigmoid(_dot(h3, wpg_ref[...]))
    x3 = x2 + gate * pp
    out_ref[...] = _rms(x3, gfin_ref[...])


def _moe_combine(pstart, idx_t, rank_t, yb, w_t, x1, p2d, g_ple, w_ple_gate, w_ple_proj, g_final):
    T, D = x1.shape
    tc = min(COMB_TILE, T)
    ple = p2d.shape[1]
    smem_spec = pl.BlockSpec((TOP_K, tc), lambda i, *_: (0, i), memory_space=pltpu.SMEM)
    const = lambda shape: pl.BlockSpec(shape, lambda i, *_: (0,) * len(shape),
                                       pipeline_mode=pl.Buffered(1))
    row = lambda a: a.reshape(1, -1)
    return pl.pallas_call(
        _comb_body,
        grid_spec=pltpu.PrefetchScalarGridSpec(
            num_scalar_prefetch=1,
            grid=(T // tc,),
            in_specs=[smem_spec, smem_spec,
                      pl.BlockSpec(memory_space=pl.ANY),
                      pl.BlockSpec((TOP_K, tc), lambda i, *_: (0, i)),
                      pl.BlockSpec((tc, D), lambda i, *_: (i, 0)),
                      pl.BlockSpec((tc, ple), lambda i, *_: (i, 0)),
                      const((1, D)), const((D, D)), const((ple, D)), const((1, D))],
            out_specs=pl.BlockSpec((tc, D), lambda i, *_: (i, 0)),
            scratch_shapes=[pltpu.VMEM((TOP_K, tc, D), F32), pltpu.SemaphoreType.DMA]),
        out_shape=jax.ShapeDtypeStruct((T, D), F32),
        compiler_params=pltpu.CompilerParams(dimension_semantics=("arbitrary",),
                                             vmem_limit_bytes=VMEM_LIMIT_BYTES),
        name="moe_combine",
    )(pstart, idx_t, rank_t, yb, w_t, x1, p2d, row(g_ple), w_ple_gate.astype(BF16),
      w_ple_proj.astype(BF16), row(g_final))


def _group_layout(counts, n_blocks):
    te = EXPERT_TILE
    padded = (counts + te - 1) // te * te
    pends = jnp.cumsum(padded)
    pstart = (pends - padded).astype(I32)
    nused = (pends[-1] // te).astype(I32)
    tail = nused + jnp.arange(N_EXPERTS, dtype=I32)
    zstart = jnp.concatenate([jnp.where(padded > 0, pends - te, -1),
                              jnp.where(tail < n_blocks, tail * te, -1)]).astype(I32)
    blk_src = jnp.minimum(jnp.arange(n_blocks, dtype=I32), jnp.maximum(nused - 1, 0))
    blk_e = jnp.clip(jnp.searchsorted(pends, blk_src * te, side="right"), 0, N_EXPERTS - 1).astype(I32)
    return pstart, zstart, blk_e, blk_src, nused.reshape(1)


def kernel(x, p, positions, g_mix_norm, w_in, g_ret_norm, w_pool, pool_scale, w_branch, w_out,
           g_ffn_norm, w_router, b_router, w_gate_up, b_gate_up, w_down, b_down,
           g_ple_norm, w_ple_gate, w_ple_proj, g_final):
    B, S, D = x.shape
    depth = w_in.shape[0]
    T = B * S
    xt = x.reshape(T, D)
    cos, sin = _rope_tables(positions.reshape(T, 1))
    n_blocks = (T * TOP_K) // EXPERT_TILE + N_EXPERTS
    for i in range(depth):
        x1, h2, idx_t, w_t, rank_t, cnt = _token_mix(
            xt, cos, sin, g_mix_norm[i], w_in[i], g_ret_norm[i], w_pool[i], pool_scale[i],
            w_branch[i], w_out[i], g_ffn_norm[i], w_router[i], b_router[i], S)
        pstart, zstart, blk_e, blk_src, nused = _group_layout(cnt[:, 0], n_blocks)
        xs = _moe_dispatch(h2, idx_t, rank_t, pstart, zstart, n_blocks * EXPERT_TILE)
        yb = _moe_experts(xs, blk_e, blk_src, nused, w_gate_up[i], b_gate_up[i], w_down[i], b_down[i])
        assert depth == 1
        xt = _moe_combine(pstart, idx_t, rank_t, yb, w_t, x1, p[i].reshape(T, -1), g_ple_norm[i],
                          w_ple_gate[i], w_ple_proj[i], g_final)
    return xt.reshape(B, S, D)
```

```python
import functools

import numpy as np
import jax
import jax.numpy as jnp
from jax import lax
from jax.experimental import pallas as pl
from jax.experimental.pallas import tpu as pltpu

F32 = jnp.float32
BF16 = jnp.bfloat16
I32 = jnp.int32

RET_HEADS = 8
RET_HEAD_DIM = 128
ROPE_BASE = 10000.0
GN_EPS = 1e-5
RMS_EPS = 1e-6
POOL_WINDOWS = (2, 4, 8, 16)
N_EXPERTS = 32
TOP_K = 4
SWIGLU_ALPHA = 1.702
SWIGLU_LIMIT = 7.0

LANES = 128
VMEM_LIMIT_BYTES = 56 * 1024 * 1024

MIX_TILE = 256
ROPE_TILE = 1024
DISP_TILE = 512
EXPERT_TILE = 256
COMB_TILE = 256
POOL_HALO = 16
ROW_UNROLL = 4


def _const_spec(shape):
    nd = len(shape)
    return pl.BlockSpec(shape, lambda *_: (0,) * nd, pipeline_mode=pl.Buffered(1))


def _rms(x, g):
    return x * lax.rsqrt(jnp.mean(x * x, axis=-1, keepdims=True) + RMS_EPS) * g


def _dot(a, b):
    return jnp.dot(a, b, preferred_element_type=F32)


def _dot_nt(a, b, precision=None):
    return lax.dot_general(a, b, (((1,), (1,)), ((), ())),
                           preferred_element_type=F32, precision=precision)


def _dot_tn(a, b):
    return lax.dot_general(a, b, (((0,), (0,)), ((), ())), preferred_element_type=F32)


def _rope_body(pos_ref, inv_ref, sign_ref, cos_ref, sin_ref):
    ang = pos_ref[...].astype(F32) * inv_ref[...]
    cos_ref[...] = jnp.cos(ang)
    sin_ref[...] = jnp.sin(ang) * sign_ref[...]


def _rope_tables(pos_col):
    T = pos_col.shape[0]
    half = RET_HEAD_DIM // 2
    inv = ROPE_BASE ** (-jnp.arange(half, dtype=F32) / half)
    inv_full = jnp.concatenate([inv, inv]).reshape(1, RET_HEAD_DIM)
    sign = jnp.concatenate([-jnp.ones((half,), F32), jnp.ones((half,), F32)]).reshape(1, RET_HEAD_DIM)
    tile = min(ROPE_TILE, T)
    return pl.pallas_call(
        _rope_body,
        grid=(T // tile,),
        in_specs=[pl.BlockSpec((tile, 1), lambda i: (i, 0)),
                  pl.BlockSpec((1, RET_HEAD_DIM), lambda i: (0, 0)),
                  pl.BlockSpec((1, RET_HEAD_DIM), lambda i: (0, 0))],
        out_specs=[pl.BlockSpec((tile, RET_HEAD_DIM), lambda i: (i, 0)),
                   pl.BlockSpec((tile, RET_HEAD_DIM), lambda i: (i, 0))],
        out_shape=[jax.ShapeDtypeStruct((T, RET_HEAD_DIM), F32),
                   jax.ShapeDtypeStruct((T, RET_HEAD_DIM), F32)],
        name="rope_tables",
    )(pos_col, inv_full, sign)


def _retention_constants(tile):
    h = np.arange(RET_HEADS, dtype=np.float64)
    log_gamma = np.log1p(-np.exp2(-5.0 - h))
    idx = np.arange(tile, dtype=np.float64)
    diff = idx[:, None] - idx[None, :]
    dmat = np.where(diff >= 0, np.exp(log_gamma[:, None, None] * np.maximum(diff, 0.0)[None]), 0.0)
    xi = np.exp(log_gamma[:, None] * (idx + 1.0)[None])
    zeta = np.exp(log_gamma[:, None] * (tile - 1.0 - idx)[None])
    chunk_decay = np.exp(log_gamma * tile)
    xi_b = np.broadcast_to(xi[:, :, None], (RET_HEADS, tile, RET_HEAD_DIM))
    zeta_b = np.broadcast_to(zeta[:, :, None], (RET_HEADS, tile, RET_HEAD_DIM))
    return (jnp.asarray(dmat, F32), jnp.asarray(xi_b, F32), jnp.asarray(zeta_b, F32),
            tuple(float(c) for c in chunk_decay))


def _mix_body(x_ref, cos_ref, sin_ref, gmix_ref, win_ref, dmat_ref, xi_ref, zeta_ref, gret_ref,
              wpool_ref, pscale_ref, wbr_ref, wout_ref, gffn_ref, wrt_ref, br_ref, tri_ref,
              x1_ref, h2_ref, idx_ref, w_ref, rank_ref, cnt_ref,
              state_ref, ue_ref, run_ref, *, tiles_per_seq, chunk_decay):
    tm, d_model = x_ref.shape
    ret_width = RET_HEADS * RET_HEAD_DIM
    i = pl.program_id(0)
    seq_tile = i % tiles_per_seq

    @pl.when(seq_tile == 0)
    def _():
        state_ref[...] = jnp.zeros(state_ref.shape, F32)
        ue_ref[0:POOL_HALO, :] = jnp.zeros((POOL_HALO, ue_ref.shape[1]), F32)

    @pl.when(i == 0)
    def _():
        run_ref[...] = jnp.zeros(run_ref.shape, F32)

    x = x_ref[...]
    hb = _rms(x, gmix_ref[...]).astype(BF16)

    qkvg = _dot(hb, win_ref[:, 0:4 * ret_width])
    cos = cos_ref[...]
    sin = sin_ref[...]

    def rot(a):
        return a * cos + pltpu.roll(a, RET_HEAD_DIM // 2, 1) * sin

    ys = []
    for h in range(RET_HEADS):
        lo = h * RET_HEAD_DIM
        q = rot(qkvg[:, lo:lo + RET_HEAD_DIM]).astype(BF16)
        k = (rot(qkvg[:, ret_width + lo:ret_width + lo + RET_HEAD_DIM])
             * (RET_HEAD_DIM ** -0.5)).astype(BF16)
        v = qkvg[:, 2 * ret_width + lo:2 * ret_width + lo + RET_HEAD_DIM]
        g = qkvg[:, 3 * ret_width + lo:3 * ret_width + lo + RET_HEAD_DIM]
        scores = _dot_nt(q, k) * dmat_ref[h]
        inner = _dot(scores.astype(BF16), v.astype(BF16))
        st = state_ref[h]
        cross = _dot(q, st.astype(BF16)) * xi_ref[h]
        state_ref[h] = st * chunk_decay[h] + _dot_tn(k, (v * zeta_ref[h]).astype(BF16))
        ret = inner + cross
        mu = jnp.mean(ret, axis=-1, keepdims=True)
        dev = ret - mu
        var = jnp.mean(dev * dev, axis=-1, keepdims=True)
        rn = dev * lax.rsqrt(var + GN_EPS) * gret_ref[:, lo:lo + RET_HEAD_DIM]
        ys.append(((g * jax.nn.sigmoid(g)) * rn).astype(BF16))
    y_ret = jnp.concatenate(ys, axis=1)
    branch_a = _dot(y_ret, wbr_ref[0])

    u = _dot(hb, win_ref[:, 4 * ret_width:4 * ret_width + d_model])
    ue_ref[POOL_HALO:POOL_HALO + tm, :] = u
    pos = seq_tile * tm + lax.broadcasted_iota(I32, (tm, 1), 0)
    group = d_model // len(POOL_WINDOWS)
    outs = []
    for gi, w in enumerate(POOL_WINDOWS):
        c0 = gi * group
        ug = u[:, c0:c0 + group]
        acc = ug
        for s in range(1, w):
            acc = acc + ue_ref[POOL_HALO - s:POOL_HALO - s + tm, c0:c0 + group]
        inv_count = 1.0 / jnp.minimum(pos + 1, w).astype(F32)
        mixed = (acc * inv_count - ug).astype(BF16)
        outs.append(_dot(mixed, wpool_ref[gi]))
    ue_ref[0:POOL_HALO, :] = ue_ref[tm:tm + POOL_HALO, :]
    y_pool = (jnp.concatenate(outs, axis=1) * pscale_ref[...]).astype(BF16)
    branch_b = _dot(y_pool, wbr_ref[1])

    gates = _dot(hb, win_ref[:, 4 * ret_width + d_model:4 * ret_width + 3 * d_model])
    merged = (jax.nn.sigmoid(gates[:, 0:d_model]) * branch_a
              + jax.nn.sigmoid(gates[:, d_model:2 * d_model]) * branch_b)
    x1 = x + _dot(merged.astype(BF16), wout_ref[...])
    x1_ref[...] = x1

    h2 = _rms(x1, gffn_ref[...])
    h2_ref[...] = h2
    logits = _dot_nt(wrt_ref[...], h2, precision=lax.Precision.HIGHEST) + br_ref[...]
    n_exp = logits.shape[0]
    eiota = lax.broadcasted_iota(I32, (n_exp, tm), 0)
    vals, idxs = [], []
    l = logits
    for _ in range(TOP_K):
        m = jnp.max(l, axis=0, keepdims=True)
        sel = jnp.min(jnp.where(l == m, eiota, n_exp), axis=0, keepdims=True)
        vals.append(m)
        idxs.append(sel)
        l = jnp.where(eiota == sel, -jnp.inf, l)
    exps = [jnp.exp(v - vals[0]) for v in vals]
    denom = exps[0] + exps[1] + exps[2] + exps[3]
    inv_denom = 1.0 / denom
    onehot = jnp.zeros((n_exp, tm), F32)
    for kk in range(TOP_K):
        onehot = onehot + (eiota == idxs[kk]).astype(F32)
    base = _dot(onehot.astype(BF16), tri_ref[...]) + run_ref[:, 0:1]
    for kk in range(TOP_K):
        idx_ref[kk:kk + 1, :] = idxs[kk]
        w_ref[kk:kk + 1, :] = exps[kk] * inv_denom
        rank = jnp.sum(jnp.where(eiota == idxs[kk], base, 0.0), axis=0, keepdims=True)
        rank_ref[kk:kk + 1, :] = rank.astype(I32)
    run = run_ref[...] + jnp.sum(onehot, axis=1, keepdims=True)
    run_ref[...] = run
    cnt_ref[...] = run.astype(I32)


def _token_mix(x2d, cos, sin, g_mix, w_in, g_ret, w_pool, pool_scale, w_branch, w_out,
               g_ffn, w_router, b_router, seq_len):
    T, D = x2d.shape
    tm = min(MIX_TILE, seq_len)
    in_width = w_in.shape[1]
    ret_width = RET_HEADS * RET_HEAD_DIM
    dmat, xi_b, zeta_b, chunk_decay = _retention_constants(tm)
    tri = jnp.asarray(np.triu(np.ones((tm, tm), np.float32), 1), BF16)
    row = lambda a: a.reshape(1, -1)
    tile_spec = lambda w: pl.BlockSpec((tm, w), lambda i: (i, 0))
    top_spec = pl.BlockSpec((TOP_K, tm), lambda i: (0, i))
    body = functools.partial(_mix_body, tiles_per_seq=seq_len // tm, chunk_decay=chunk_decay)
    return pl.pallas_call(
        body,
        grid=(T // tm,),
        in_specs=[tile_spec(D), tile_spec(RET_HEAD_DIM), tile_spec(RET_HEAD_DIM),
                  _const_spec((1, D)), _const_spec((D, in_width)),
                  _const_spec(dmat.shape), _const_spec(xi_b.shape), _const_spec(zeta_b.shape),
                  _const_spec((1, ret_width)), _const_spec(w_pool.shape), _const_spec((1, D)),
                  _const_spec(w_branch.shape), _const_spec((D, D)), _const_spec((1, D)),
                  _const_spec((N_EXPERTS, D)), _const_spec((N_EXPERTS, 1)), _const_spec((tm, tm))],
        out_specs=[tile_spec(D), tile_spec(D), top_spec, top_spec, top_spec,
                   pl.BlockSpec((N_EXPERTS, LANES), lambda i: (0, 0))],
        out_shape=[jax.ShapeDtypeStruct((T, D), F32), jax.ShapeDtypeStruct((T, D), F32),
                   jax.ShapeDtypeStruct((TOP_K, T), I32), jax.ShapeDtypeStruct((TOP_K, T), F32),
                   jax.ShapeDtypeStruct((TOP_K, T), I32),
                   jax.ShapeDtypeStruct((N_EXPERTS, LANES), I32)],
        scratch_shapes=[pltpu.VMEM((RET_HEADS, RET_HEAD_DIM, RET_HEAD_DIM), F32),
                        pltpu.VMEM((tm + POOL_HALO, D), F32),
                        pltpu.VMEM((N_EXPERTS, LANES), F32)],
        compiler_params=pltpu.CompilerParams(dimension_semantics=("arbitrary",),
                                             vmem_limit_bytes=VMEM_LIMIT_BYTES),
        name="token_mix",
    )(x2d, cos, sin, row(g_mix), w_in.astype(BF16), dmat, xi_b, zeta_b, row(g_ret),
      w_pool.astype(BF16), row(pool_scale), w_branch.astype(BF16), w_out.astype(BF16),
      row(g_ffn), w_router.T, b_router.reshape(-1, 1), tri)


def _disp_body(pstart_ref, zstart_ref, h2_ref, idx_ref, rank_ref, xs_ref, zbuf_ref, sem, zsem):
    td = h2_ref.shape[0]
    te = zbuf_ref.shape[0]

    def zero_copy(e):
        start = pl.multiple_of(jnp.maximum(zstart_ref[e], 0), te)
        return pltpu.make_async_copy(zbuf_ref, xs_ref.at[pl.ds(start, te)], zsem)

    @pl.when(pl.program_id(0) == 0)
    def _():
        zbuf_ref[...] = jnp.zeros(zbuf_ref.shape, F32)

        def start(e, c):
            @pl.when(zstart_ref[e] >= 0)
            def _():
                zero_copy(e).start()
            return c

        def wait(e, c):
            @pl.when(zstart_ref[e] >= 0)
            def _():
                zero_copy(e).wait()
            return c

        lax.fori_loop(0, zstart_ref.shape[0], start, 0)
        lax.fori_loop(0, zstart_ref.shape[0], wait, 0)

    def row_copy(t, dest):
        return pltpu.make_async_copy(h2_ref.at[pl.ds(t, 1)], xs_ref.at[pl.ds(dest, 1)], sem)

    def start(j, c):
        ts = [j * ROW_UNROLL + u for u in range(ROW_UNROLL)]
        dests = [[pstart_ref[idx_ref[kk, t]] + rank_ref[kk, t] for kk in range(TOP_K)] for t in ts]
        for t, dest in zip(ts, dests):
            for kk in range(TOP_K):
                row_copy(t, dest[kk]).start(priority=kk % 2)
        return c

    lax.fori_loop(0, td // ROW_UNROLL, start, 0)
    for kk in range(TOP_K):
        pltpu.make_async_copy(h2_ref, xs_ref.at[pl.ds(0, td)], sem).wait()


def _moe_dispatch(h2, idx_t, rank_t, pstart, zstart, n_rows):
    T, D = h2.shape
    td = min(DISP_TILE, T)
    smem_spec = pl.BlockSpec((TOP_K, td), lambda i, *_: (0, i), memory_space=pltpu.SMEM)
    return pl.pallas_call(
        _disp_body,
        grid_spec=pltpu.PrefetchScalarGridSpec(
            num_scalar_prefetch=2,
            grid=(T // td,),
            in_specs=[pl.BlockSpec((td, D), lambda i, *_: (i, 0)), smem_spec, smem_spec],
            out_specs=pl.BlockSpec(memory_space=pl.ANY),
            scratch_shapes=[pltpu.VMEM((EXPERT_TILE, D), F32),
                            pltpu.SemaphoreType.DMA, pltpu.SemaphoreType.DMA]),
        out_shape=jax.ShapeDtypeStruct((n_rows, D), F32),
        compiler_params=pltpu.CompilerParams(dimension_semantics=("arbitrary",),
                                             vmem_limit_bytes=VMEM_LIMIT_BYTES),
        name="moe_dispatch",
    )(pstart, zstart, h2, idx_t, rank_t)


def _expert_body(blk_e_ref, blk_src_ref, nused_ref, xs_ref, wgu_ref, bgu_ref, wd_ref, bd_ref,
                 yb_ref, wgu_bf_ref, wd_bf_ref):
    i = pl.program_id(0)
    d_ff = wd_ref.shape[0]
    prev_e = blk_e_ref[jnp.maximum(i - 1, 0)]

    @pl.when((i == 0) | (blk_e_ref[i] != prev_e))
    def _():
        wgu_bf_ref[...] = wgu_ref[...].astype(BF16)
        wd_bf_ref[...] = wd_ref[...].astype(BF16)

    @pl.when(i < nused_ref[0])
    def _():
        gu = _dot(xs_ref[...].astype(BF16), wgu_bf_ref[...]) + bgu_ref[...]
        gate = jnp.minimum(gu[:, 0:d_ff], SWIGLU_LIMIT)
        up = jnp.clip(gu[:, d_ff:2 * d_ff], -SWIGLU_LIMIT, SWIGLU_LIMIT)
        act = (up + 1.0) * (gate * jax.nn.sigmoid(gate * SWIGLU_ALPHA))
        yb_ref[...] = _dot(act.astype(BF16), wd_bf_ref[...]) + bd_ref[...]

    @pl.when(i >= nused_ref[0])
    def _():
        yb_ref[...] = jnp.zeros(yb_ref.shape, F32)


def _moe_experts(xs, blk_e, blk_src, nused, w_gate_up, b_gate_up, w_down, b_down):
    n_rows, D = xs.shape
    E, _, two_ff = w_gate_up.shape
    d_ff = two_ff // 2
    te = EXPERT_TILE
    return pl.pallas_call(
        _expert_body,
        grid_spec=pltpu.PrefetchScalarGridSpec(
            num_scalar_prefetch=3,
            grid=(n_rows // te,),
            in_specs=[pl.BlockSpec((te, D), lambda i, be, bs, nu: (bs[i], 0)),
                      pl.BlockSpec((None, D, two_ff), lambda i, be, bs, nu: (be[i], 0, 0)),
                      pl.BlockSpec((None, 1, two_ff), lambda i, be, bs, nu: (be[i], 0, 0)),
                      pl.BlockSpec((None, d_ff, D), lambda i, be, bs, nu: (be[i], 0, 0)),
                      pl.BlockSpec((None, 1, D), lambda i, be, bs, nu: (be[i], 0, 0))],
            out_specs=pl.BlockSpec((te, D), lambda i, be, bs, nu: (i, 0)),
            scratch_shapes=[pltpu.VMEM((D, two_ff), BF16), pltpu.VMEM((d_ff, D), BF16)]),
        out_shape=jax.ShapeDtypeStruct((n_rows, D), F32),
        compiler_params=pltpu.CompilerParams(dimension_semantics=("arbitrary",),
                                             vmem_limit_bytes=VMEM_LIMIT_BYTES),
        name="moe_experts",
    )(blk_e, blk_src, nused, xs, w_gate_up, b_gate_up.reshape(E, 1, two_ff),
      w_down, b_down.reshape(E, 1, D))


def _comb_body(pstart_ref, idx_ref, rank_ref, idxn_ref, rankn_ref, yb_ref, w_ref, x1_ref, p_ref,
               gple_ref, wpg_ref, wpp_ref, gfin_ref, out_ref, gbuf_ref, sems):
    tc = x1_ref.shape[0]
    i = pl.program_id(0)
    slot = i % 2

    def issue(idx_r, rank_r, s):
        def start(j, c):
            ts = [j * ROW_UNROLL + u for u in range(ROW_UNROLL)]
            srcs = [[pstart_ref[idx_r[kk, t]] + rank_r[kk, t] for kk in range(TOP_K)] for t in ts]
            for t, src in zip(ts, srcs):
                for kk in range(TOP_K):
                    pltpu.make_async_copy(yb_ref.at[pl.ds(src[kk], 1)], gbuf_ref.at[s, kk, pl.ds(t, 1)],
                                          sems.at[s]).start(priority=kk % 2)
            return c

        lax.fori_loop(0, tc // ROW_UNROLL, start, 0)

    @pl.when(i == 0)
    def _():
        issue(idx_ref, rank_ref, 0)

    @pl.when(i + 1 < pl.num_programs(0))
    def _():
        issue(idxn_ref, rankn_ref, 1 - slot)

    w_pad = jnp.concatenate([w_ref[...], jnp.zeros((LANES - TOP_K, tc), F32)], axis=0)
    w_col = w_pad.T
    pp = _dot(p_ref[...].astype(BF16), wpp_ref[...])

    for kk in range(TOP_K):
        pltpu.make_async_copy(yb_ref.at[pl.ds(0, tc)], gbuf_ref.at[slot, kk], sems.at[slot]).wait()

    moe = w_col[:, 0:1] * gbuf_ref[slot, 0]
    for kk in range(1, TOP_K):
        moe = moe + w_col[:, kk:kk + 1] * gbuf_ref[slot, kk]
    x2 = x1_ref[...] + moe
    h3 = _rms(x2, gple_ref[...]).astype(BF16)
    gate = jax.nn.sigmoid(_dot(h3, wpg_ref[...]))
    x3 = x2 + gate * pp
    out_ref[...] = _rms(x3, gfin_ref[...])


def _moe_combine(pstart, idx_t, rank_t, yb, w_t, x1, p2d, g_ple, w_ple_gate, w_ple_proj, g_final):
    T, D = x1.shape
    tc = min(COMB_TILE, T)
    n_tiles = T // tc
    ple = p2d.shape[1]
    smem_spec = pl.BlockSpec((TOP_K, tc), lambda i, *_: (0, i), memory_space=pltpu.SMEM)
    smem_next = pl.BlockSpec((TOP_K, tc), lambda i, *_: (0, jnp.minimum(i + 1, n_tiles - 1)),
                             memory_space=pltpu.SMEM)
    const = lambda shape: pl.BlockSpec(shape, lambda i, *_: (0,) * len(shape),
                                       pipeline_mode=pl.Buffered(1))
    row = lambda a: a.reshape(1, -1)
    return pl.pallas_call(
        _comb_body,
        grid_spec=pltpu.PrefetchScalarGridSpec(
            num_scalar_prefetch=1,
            grid=(n_tiles,),
            in_specs=[smem_spec, smem_spec, smem_next, smem_next,
                      pl.BlockSpec(memory_space=pl.ANY),
                      pl.BlockSpec((TOP_K, tc), lambda i, *_: (0, i)),
                      pl.BlockSpec((tc, D), lambda i, *_: (i, 0)),
                      pl.BlockSpec((tc, ple), lambda i, *_: (i, 0)),
                      const((1, D)), const((D, D)), const((ple, D)), const((1, D))],
            out_specs=pl.BlockSpec((tc, D), lambda i, *_: (i, 0)),
            scratch_shapes=[pltpu.VMEM((2, TOP_K, tc, D), F32), pltpu.SemaphoreType.DMA((2,))]),
        out_shape=jax.ShapeDtypeStruct((T, D), F32),
        compiler_params=pltpu.CompilerParams(dimension_semantics=("arbitrary",),
                                             vmem_limit_bytes=VMEM_LIMIT_BYTES),
        name="moe_combine",
    )(pstart, idx_t, rank_t, idx_t, rank_t, yb, w_t, x1, p2d, row(g_ple), w_ple_gate.astype(BF16),
      w_ple_proj.astype(BF16), row(g_final))


def _group_layout(counts, n_blocks):
    te = EXPERT_TILE
    padded = (counts + te - 1) // te * te
    pends = jnp.cumsum(padded)
    pstart = (pends - padded).astype(I32)
    nused = (pends[-1] // te).astype(I32)
    tail = nused + jnp.arange(N_EXPERTS, dtype=I32)
    zstart = jnp.concatenate([jnp.where(padded > 0, pends - te, -1),
                              jnp.where(tail < n_blocks, tail * te, -1)]).astype(I32)
    blk_src = jnp.minimum(jnp.arange(n_blocks, dtype=I32), jnp.maximum(nused - 1, 0))
    blk_e = jnp.sum((blk_src * te)[:, None] >= pends[None, :], axis=1)
    blk_e = jnp.clip(blk_e, 0, N_EXPERTS - 1).astype(I32)
    return pstart, zstart, blk_e, blk_src, nused.reshape(1)


def kernel(x, p, positions, g_mix_norm, w_in, g_ret_norm, w_pool, pool_scale, w_branch, w_out,
           g_ffn_norm, w_router, b_router, w_gate_up, b_gate_up, w_down, b_down,
           g_ple_norm, w_ple_gate, w_ple_proj, g_final):
    B, S, D = x.shape
    depth = w_in.shape[0]
    T = B * S
    xt = x.reshape(T, D)
    cos, sin = _rope_tables(positions.reshape(T, 1))
    n_blocks = (T * TOP_K) // EXPERT_TILE + N_EXPERTS
    for i in range(depth):
        x1, h2, idx_t, w_t, rank_t, cnt = _token_mix(
            xt, cos, sin, g_mix_norm[i], w_in[i], g_ret_norm[i], w_pool[i], pool_scale[i],
            w_branch[i], w_out[i], g_ffn_norm[i], w_router[i], b_router[i], S)
        pstart, zstart, blk_e, blk_src, nused = _group_layout(cnt[:, 0], n_blocks)
        xs = _moe_dispatch(h2, idx_t, rank_t, pstart, zstart, n_blocks * EXPERT_TILE)
        yb = _moe_experts(xs, blk_e, blk_src, nused, w_gate_up[i], b_gate_up[i], w_down[i], b_down[i])
        assert depth == 1
        xt = _moe_combine(pstart, idx_t, rank_t, yb, w_t, x1, p[i].reshape(T, -1), g_ple_norm[i],
                          w_ple_gate[i], w_ple_proj[i], g_final)
    return xt.reshape(B, S, D)
```

```python
import functools

import numpy as np
import jax
import jax.numpy as jnp
from jax import lax
from jax.experimental import pallas as pl
from jax.experimental.pallas import tpu as pltpu

F32 = jnp.float32
BF16 = jnp.bfloat16
I32 = jnp.int32

RET_HEADS = 8
RET_HEAD_DIM = 128
ROPE_BASE = 10000.0
GN_EPS = 1e-5
RMS_EPS = 1e-6
POOL_WINDOWS = (2, 4, 8, 16)
N_EXPERTS = 32
TOP_K = 4
SWIGLU_ALPHA = 1.702
SWIGLU_LIMIT = 7.0

LANES = 128
SUBLANES = 8
VMEM_LIMIT_BYTES = 56 * 1024 * 1024

MIX_TILE = 256
ROPE_TILE = 1024
DISP_TILE = 512
EXPERT_TILE = 256
COMB_TILE = 256
POOL_HALO = 16
COMB_WINDOW = 64
ROW_UNROLL = 4


def _const_spec(shape):
    nd = len(shape)
    return pl.BlockSpec(shape, lambda *_: (0,) * nd, pipeline_mode=pl.Buffered(1))


def _rms(x, g):
    return x * lax.rsqrt(jnp.mean(x * x, axis=-1, keepdims=True) + RMS_EPS) * g


def _dot(a, b):
    return jnp.dot(a, b, preferred_element_type=F32)


def _dot_nt(a, b, precision=None):
    return lax.dot_general(a, b, (((1,), (1,)), ((), ())),
                           preferred_element_type=F32, precision=precision)


def _dot_tn(a, b):
    return lax.dot_general(a, b, (((0,), (0,)), ((), ())), preferred_element_type=F32)


def _rope_body(pos_ref, inv_ref, sign_ref, cos_ref, sin_ref):
    ang = pos_ref[...].astype(F32) * inv_ref[...]
    cos_ref[...] = jnp.cos(ang)
    sin_ref[...] = jnp.sin(ang) * sign_ref[...]


def _rope_tables(pos_col):
    T = pos_col.shape[0]
    half = RET_HEAD_DIM // 2
    inv = ROPE_BASE ** (-jnp.arange(half, dtype=F32) / half)
    inv_full = jnp.concatenate([inv, inv]).reshape(1, RET_HEAD_DIM)
    sign = jnp.concatenate([-jnp.ones((half,), F32), jnp.ones((half,), F32)]).reshape(1, RET_HEAD_DIM)
    tile = min(ROPE_TILE, T)
    return pl.pallas_call(
        _rope_body,
        grid=(T // tile,),
        in_specs=[pl.BlockSpec((tile, 1), lambda i: (i, 0)),
                  pl.BlockSpec((1, RET_HEAD_DIM), lambda i: (0, 0)),
                  pl.BlockSpec((1, RET_HEAD_DIM), lambda i: (0, 0))],
        out_specs=[pl.BlockSpec((tile, RET_HEAD_DIM), lambda i: (i, 0)),
                   pl.BlockSpec((tile, RET_HEAD_DIM), lambda i: (i, 0))],
        out_shape=[jax.ShapeDtypeStruct((T, RET_HEAD_DIM), F32),
                   jax.ShapeDtypeStruct((T, RET_HEAD_DIM), F32)],
        name="rope_tables",
    )(pos_col, inv_full, sign)


def _retention_constants(tile):
    h = np.arange(RET_HEADS, dtype=np.float64)
    log_gamma = np.log1p(-np.exp2(-5.0 - h))
    idx = np.arange(tile, dtype=np.float64)
    diff = idx[:, None] - idx[None, :]
    dmat = np.where(diff >= 0, np.exp(log_gamma[:, None, None] * np.maximum(diff, 0.0)[None]), 0.0)
    xi = np.exp(log_gamma[:, None] * (idx + 1.0)[None])
    zeta = np.exp(log_gamma[:, None] * (tile - 1.0 - idx)[None])
    chunk_decay = np.exp(log_gamma * tile)
    xi_b = np.broadcast_to(xi[:, :, None], (RET_HEADS, tile, RET_HEAD_DIM))
    zeta_b = np.broadcast_to(zeta[:, :, None], (RET_HEADS, tile, RET_HEAD_DIM))
    return (jnp.asarray(dmat, F32), jnp.asarray(xi_b, F32), jnp.asarray(zeta_b, F32),
            tuple(float(c) for c in chunk_decay))


def _mix_body(x_ref, cos_ref, sin_ref, gmix_ref, win_ref, dmat_ref, xi_ref, zeta_ref, gret_ref,
              wpool_ref, pscale_ref, wbr_ref, wout_ref, gffn_ref, wrt_ref, br_ref, tri_ref,
              x1_ref, h2_ref, idx_ref, w_ref, rank_ref, cnt_ref, trun_ref,
              state_ref, ue_ref, run_ref, *, tiles_per_seq, chunk_decay):
    tm, d_model = x_ref.shape
    ret_width = RET_HEADS * RET_HEAD_DIM
    i = pl.program_id(0)
    seq_tile = i % tiles_per_seq

    @pl.when(seq_tile == 0)
    def _():
        state_ref[...] = jnp.zeros(state_ref.shape, F32)
        ue_ref[0:POOL_HALO, :] = jnp.zeros((POOL_HALO, ue_ref.shape[1]), F32)

    @pl.when(i == 0)
    def _():
        run_ref[...] = jnp.zeros(run_ref.shape, F32)

    x = x_ref[...]
    hb = _rms(x, gmix_ref[...]).astype(BF16)

    qkvg = _dot(hb, win_ref[:, 0:4 * ret_width])
    cos = cos_ref[...]
    sin = sin_ref[...]

    def rot(a):
        return a * cos + pltpu.roll(a, RET_HEAD_DIM // 2, 1) * sin

    ys = []
    for h in range(RET_HEADS):
        lo = h * RET_HEAD_DIM
        q = rot(qkvg[:, lo:lo + RET_HEAD_DIM]).astype(BF16)
        k = (rot(qkvg[:, ret_width + lo:ret_width + lo + RET_HEAD_DIM])
             * (RET_HEAD_DIM ** -0.5)).astype(BF16)
        v = qkvg[:, 2 * ret_width + lo:2 * ret_width + lo + RET_HEAD_DIM]
        g = qkvg[:, 3 * ret_width + lo:3 * ret_width + lo + RET_HEAD_DIM]
        scores = _dot_nt(q, k) * dmat_ref[h]
        inner = _dot(scores.astype(BF16), v.astype(BF16))
        st = state_ref[h]
        cross = _dot(q, st.astype(BF16)) * xi_ref[h]
        state_ref[h] = st * chunk_decay[h] + _dot_tn(k, (v * zeta_ref[h]).astype(BF16))
        ret = inner + cross
        mu = jnp.mean(ret, axis=-1, keepdims=True)
        dev = ret - mu
        var = jnp.mean(dev * dev, axis=-1, keepdims=True)
        rn = dev * lax.rsqrt(var + GN_EPS) * gret_ref[:, lo:lo + RET_HEAD_DIM]
        ys.append(((g * jax.nn.sigmoid(g)) * rn).astype(BF16))
    y_ret = jnp.concatenate(ys, axis=1)
    branch_a = _dot(y_ret, wbr_ref[0])

    u = _dot(hb, win_ref[:, 4 * ret_width:4 * ret_width + d_model])
    ue_ref[POOL_HALO:POOL_HALO + tm, :] = u
    pos = seq_tile * tm + lax.broadcasted_iota(I32, (tm, 1), 0)
    group = d_model // len(POOL_WINDOWS)
    outs = []
    for gi, w in enumerate(POOL_WINDOWS):
        c0 = gi * group
        ug = u[:, c0:c0 + group]
        acc = ug
        for s in range(1, w):
            acc = acc + ue_ref[POOL_HALO - s:POOL_HALO - s + tm, c0:c0 + group]
        inv_count = 1.0 / jnp.minimum(pos + 1, w).astype(F32)
        mixed = (acc * inv_count - ug).astype(BF16)
        outs.append(_dot(mixed, wpool_ref[gi]))
    ue_ref[0:POOL_HALO, :] = ue_ref[tm:tm + POOL_HALO, :]
    y_pool = (jnp.concatenate(outs, axis=1) * pscale_ref[...]).astype(BF16)
    branch_b = _dot(y_pool, wbr_ref[1])

    gates = _dot(hb, win_ref[:, 4 * ret_width + d_model:4 * ret_width + 3 * d_model])
    merged = (jax.nn.sigmoid(gates[:, 0:d_model]) * branch_a
              + jax.nn.sigmoid(gates[:, d_model:2 * d_model]) * branch_b)
    x1 = x + _dot(merged.astype(BF16), wout_ref[...])
    x1_ref[...] = x1

    h2 = _rms(x1, gffn_ref[...])
    h2_ref[...] = h2
    logits = _dot_nt(wrt_ref[...], h2, precision=lax.Precision.HIGHEST) + br_ref[...]
    n_exp = logits.shape[0]
    eiota = lax.broadcasted_iota(I32, (n_exp, tm), 0)
    vals, idxs = [], []
    l = logits
    for _ in range(TOP_K):
        m = jnp.max(l, axis=0, keepdims=True)
        sel = jnp.min(jnp.where(l == m, eiota, n_exp), axis=0, keepdims=True)
        vals.append(m)
        idxs.append(sel)
        l = jnp.where(eiota == sel, -jnp.inf, l)
    exps = [jnp.exp(v - vals[0]) for v in vals]
    denom = exps[0] + exps[1] + exps[2] + exps[3]
    inv_denom = 1.0 / denom
    onehot = jnp.zeros((n_exp, tm), F32)
    for kk in range(TOP_K):
        onehot = onehot + (eiota == idxs[kk]).astype(F32)
    base = _dot(onehot.astype(BF16), tri_ref[...]) + run_ref[:, 0:1]
    for kk in range(TOP_K):
        idx_ref[kk:kk + 1, :] = idxs[kk]
        w_ref[kk:kk + 1, :] = exps[kk] * inv_denom
        rank = jnp.sum(jnp.where(eiota == idxs[kk], base, 0.0), axis=0, keepdims=True)
        rank_ref[kk:kk + 1, :] = rank.astype(I32)
    trun_ref[...] = run_ref[...].astype(I32)
    run = run_ref[...] + jnp.sum(onehot, axis=1, keepdims=True)
    run_ref[...] = run
    cnt_ref[...] = run.astype(I32)


def _token_mix(x2d, cos, sin, g_mix, w_in, g_ret, w_pool, pool_scale, w_branch, w_out,
               g_ffn, w_router, b_router, seq_len):
    T, D = x2d.shape
    tm = min(MIX_TILE, seq_len)
    in_width = w_in.shape[1]
    ret_width = RET_HEADS * RET_HEAD_DIM
    dmat, xi_b, zeta_b, chunk_decay = _retention_constants(tm)
    tri = jnp.asarray(np.triu(np.ones((tm, tm), np.float32), 1), BF16)
    row = lambda a: a.reshape(1, -1)
    tile_spec = lambda w: pl.BlockSpec((tm, w), lambda i: (i, 0))
    top_spec = pl.BlockSpec((TOP_K, tm), lambda i: (0, i))
    body = functools.partial(_mix_body, tiles_per_seq=seq_len // tm, chunk_decay=chunk_decay)
    return pl.pallas_call(
        body,
        grid=(T // tm,),
        in_specs=[tile_spec(D), tile_spec(RET_HEAD_DIM), tile_spec(RET_HEAD_DIM),
                  _const_spec((1, D)), _const_spec((D, in_width)),
                  _const_spec(dmat.shape), _const_spec(xi_b.shape), _const_spec(zeta_b.shape),
                  _const_spec((1, ret_width)), _const_spec(w_pool.shape), _const_spec((1, D)),
                  _const_spec(w_branch.shape), _const_spec((D, D)), _const_spec((1, D)),
                  _const_spec((N_EXPERTS, D)), _const_spec((N_EXPERTS, 1)), _const_spec((tm, tm))],
        out_specs=[tile_spec(D), tile_spec(D), top_spec, top_spec, top_spec,
                   pl.BlockSpec((N_EXPERTS, LANES), lambda i: (0, 0)),
                   pl.BlockSpec((None, N_EXPERTS, LANES), lambda i: (i, 0, 0))],
        out_shape=[jax.ShapeDtypeStruct((T, D), F32), jax.ShapeDtypeStruct((T, D), F32),
                   jax.ShapeDtypeStruct((TOP_K, T), I32), jax.ShapeDtypeStruct((TOP_K, T), F32),
                   jax.ShapeDtypeStruct((TOP_K, T), I32),
                   jax.ShapeDtypeStruct((N_EXPERTS, LANES), I32),
                   jax.ShapeDtypeStruct((T // tm, N_EXPERTS, LANES), I32)],
        scratch_shapes=[pltpu.VMEM((RET_HEADS, RET_HEAD_DIM, RET_HEAD_DIM), F32),
                        pltpu.VMEM((tm + POOL_HALO, D), F32),
                        pltpu.VMEM((N_EXPERTS, LANES), F32)],
        compiler_params=pltpu.CompilerParams(dimension_semantics=("arbitrary",),
                                             vmem_limit_bytes=VMEM_LIMIT_BYTES),
        name="token_mix",
    )(x2d, cos, sin, row(g_mix), w_in.astype(BF16), dmat, xi_b, zeta_b, row(g_ret),
      w_pool.astype(BF16), row(pool_scale), w_branch.astype(BF16), w_out.astype(BF16),
      row(g_ffn), w_router.T, b_router.reshape(-1, 1), tri)


def _disp_body(zstart_ref, h2_ref, dest_ref, xs_ref, zbuf_ref, sem, zsem):
    td = h2_ref.shape[0]
    te = zbuf_ref.shape[0]

    def zero_copy(e):
        start = pl.multiple_of(jnp.maximum(zstart_ref[e], 0), te)
        return pltpu.make_async_copy(zbuf_ref, xs_ref.at[pl.ds(start, te)], zsem)

    @pl.when(pl.program_id(0) == 0)
    def _():
        zbuf_ref[...] = jnp.zeros(zbuf_ref.shape, F32)

        def start(e, c):
            @pl.when(zstart_ref[e] >= 0)
            def _():
                zero_copy(e).start()
            return c

        def wait(e, c):
            @pl.when(zstart_ref[e] >= 0)
            def _():
                zero_copy(e).wait()
            return c

        lax.fori_loop(0, zstart_ref.shape[0], start, 0)
        lax.fori_loop(0, zstart_ref.shape[0], wait, 0)

    def row_copy(t, dest):
        return pltpu.make_async_copy(h2_ref.at[pl.ds(t, 1)], xs_ref.at[pl.ds(dest, 1)], sem)

    def start(j, c):
        ts = [j * ROW_UNROLL + u for u in range(ROW_UNROLL)]
        dests = [[dest_ref[kk, t] for kk in range(TOP_K)] for t in ts]
        for t, dest in zip(ts, dests):
            for kk in range(TOP_K):
                row_copy(t, dest[kk]).start(priority=kk % 2)
        return c

    lax.fori_loop(0, td // ROW_UNROLL, start, 0)
    for kk in range(TOP_K):
        pltpu.make_async_copy(h2_ref, xs_ref.at[pl.ds(0, td)], sem).wait()


def _moe_dispatch(h2, dest_t, zstart, n_rows):
    T, D = h2.shape
    td = min(DISP_TILE, T)
    smem_spec = pl.BlockSpec((TOP_K, td), lambda i, *_: (0, i), memory_space=pltpu.SMEM)
    return pl.pallas_call(
        _disp_body,
        grid_spec=pltpu.PrefetchScalarGridSpec(
            num_scalar_prefetch=1,
            grid=(T // td,),
            in_specs=[pl.BlockSpec((td, D), lambda i, *_: (i, 0)), smem_spec],
            out_specs=pl.BlockSpec(memory_space=pl.ANY),
            scratch_shapes=[pltpu.VMEM((EXPERT_TILE, D), F32),
                            pltpu.SemaphoreType.DMA, pltpu.SemaphoreType.DMA]),
        out_shape=jax.ShapeDtypeStruct((n_rows, D), F32),
        compiler_params=pltpu.CompilerParams(dimension_semantics=("arbitrary",),
                                             vmem_limit_bytes=VMEM_LIMIT_BYTES),
        name="moe_dispatch",
    )(zstart, h2, dest_t)


def _expert_body(blk_e_ref, blk_src_ref, nused_ref, xs_ref, wgu_ref, bgu_ref, wd_ref, bd_ref,
                 yb_ref, wgu_bf_ref, wd_bf_ref):
    i = pl.program_id(0)
    d_ff = wd_ref.shape[0]
    prev_e = blk_e_ref[jnp.maximum(i - 1, 0)]

    @pl.when((i == 0) | (blk_e_ref[i] != prev_e))
    def _():
        wgu_bf_ref[...] = wgu_ref[...].astype(BF16)
        wd_bf_ref[...] = wd_ref[...].astype(BF16)

    @pl.when(i < nused_ref[0])
    def _():
        gu = _dot(xs_ref[...].astype(BF16), wgu_bf_ref[...]) + bgu_ref[...]
        gate = jnp.minimum(gu[:, 0:d_ff], SWIGLU_LIMIT)
        up = jnp.clip(gu[:, d_ff:2 * d_ff], -SWIGLU_LIMIT, SWIGLU_LIMIT)
        act = (up + 1.0) * (gate * jax.nn.sigmoid(gate * SWIGLU_ALPHA))
        yb_ref[...] = _dot(act.astype(BF16), wd_bf_ref[...]) + bd_ref[...]

    @pl.when(i >= nused_ref[0])
    def _():
        yb_ref[...] = jnp.zeros(yb_ref.shape, F32)


def _moe_experts(xs, blk_e, blk_src, nused, w_gate_up, b_gate_up, w_down, b_down):
    n_rows, D = xs.shape
    E, _, two_ff = w_gate_up.shape
    d_ff = two_ff // 2
    te = EXPERT_TILE
    return pl.pallas_call(
        _expert_body,
        grid_spec=pltpu.PrefetchScalarGridSpec(
            num_scalar_prefetch=3,
            grid=(n_rows // te,),
            in_specs=[pl.BlockSpec((te, D), lambda i, be, bs, nu: (bs[i], 0)),
                      pl.BlockSpec((None, D, two_ff), lambda i, be, bs, nu: (be[i], 0, 0)),
                      pl.BlockSpec((None, 1, two_ff), lambda i, be, bs, nu: (be[i], 0, 0)),
                      pl.BlockSpec((None, d_ff, D), lambda i, be, bs, nu: (be[i], 0, 0)),
                      pl.BlockSpec((None, 1, D), lambda i, be, bs, nu: (be[i], 0, 0))],
            out_specs=pl.BlockSpec((te, D), lambda i, be, bs, nu: (i, 0)),
            scratch_shapes=[pltpu.VMEM((D, two_ff), BF16), pltpu.VMEM((d_ff, D), BF16)]),
        out_shape=jax.ShapeDtypeStruct((n_rows, D), F32),
        compiler_params=pltpu.CompilerParams(dimension_semantics=("arbitrary",),
                                             vmem_limit_bytes=VMEM_LIMIT_BYTES),
        name="moe_experts",
    )(blk_e, blk_src, nused, xs, w_gate_up, b_gate_up.reshape(E, 1, two_ff),
      w_down, b_down.reshape(E, 1, D))


def _comb_body(ws_ref, ok_ref, dest_ref, destn_ref, yb_ref, pos_ref, w_ref, x1_ref, p_ref,
               gple_ref, wpg_ref, wpp_ref, gfin_ref, out_ref, gbuf_ref, sems):
    tc = x1_ref.shape[0]
    n_rows = gbuf_ref.shape[1]
    i = pl.program_id(0)
    slot = i % 2

    def window_copy(tile, e, s):
        src = pl.multiple_of(ws_ref[tile * N_EXPERTS + e], SUBLANES)
        return pltpu.make_async_copy(yb_ref.at[pl.ds(src, COMB_WINDOW)],
                                     gbuf_ref.at[s, pl.ds(e * COMB_WINDOW, COMB_WINDOW)], sems.at[s])

    def issue(tile, dest_r, s):
        @pl.when(ok_ref[tile] != 0)
        def _():
            for e in range(N_EXPERTS):
                window_copy(tile, e, s).start(priority=e % 2)

        @pl.when(ok_ref[tile] == 0)
        def _():
            def start(j, c):
                ts = [j * ROW_UNROLL + u for u in range(ROW_UNROLL)]
                srcs = [[dest_r[kk, t] for kk in range(TOP_K)] for t in ts]
                for t, src in zip(ts, srcs):
                    for kk in range(TOP_K):
                        pltpu.make_async_copy(yb_ref.at[pl.ds(src[kk], 1)],
                                              gbuf_ref.at[s, pl.ds(kk * tc + t, 1)],
                                              sems.at[s]).start(priority=kk % 2)
                return c

            lax.fori_loop(0, tc // ROW_UNROLL, start, 0)

    @pl.when(i == 0)
    def _():
        issue(0, dest_ref, 0)

    @pl.when(i + 1 < pl.num_programs(0))
    def _():
        issue(i + 1, destn_ref, 1 - slot)

    def to_cols(rows):
        pad = jnp.zeros((LANES - rows.shape[0], tc), F32)
        return jnp.concatenate([rows, pad], axis=0).T

    w_col = to_cols(w_ref[...])
    fast = ok_ref[i] != 0
    slow_pos = (lax.broadcasted_iota(I32, (TOP_K, tc), 0) * tc
                + lax.broadcasted_iota(I32, (TOP_K, tc), 1))
    pos_col = to_cols(jnp.where(fast, pos_ref[...], slow_pos).astype(F32))
    pp = _dot(p_ref[...].astype(BF16), wpp_ref[...])

    @pl.when(fast)
    def _():
        pltpu.make_async_copy(yb_ref.at[pl.ds(0, n_rows)], gbuf_ref.at[slot], sems.at[slot]).wait()

    @pl.when(jnp.logical_not(fast))
    def _():
        pltpu.make_async_copy(yb_ref.at[pl.ds(0, TOP_K * tc)], gbuf_ref.at[slot, pl.ds(0, TOP_K * tc)],
                              sems.at[slot]).wait()
        gbuf_ref[slot, pl.ds(TOP_K * tc, n_rows - TOP_K * tc), :] = jnp.zeros(
            (n_rows - TOP_K * tc, gbuf_ref.shape[2]), F32)

    ciota = lax.broadcasted_iota(I32, (tc, n_rows), 1).astype(F32)
    sel = jnp.zeros((tc, n_rows), F32)
    for kk in range(TOP_K):
        sel = sel + jnp.where(ciota == pos_col[:, kk:kk + 1], w_col[:, kk:kk + 1], 0.0)
    moe = _dot(sel.astype(BF16), gbuf_ref[slot].astype(BF16))
    x2 = x1_ref[...] + moe
    h3 = _rms(x2, gple_ref[...]).astype(BF16)
    gate = jax.nn.sigmoid(_dot(h3, wpg_ref[...]))
    x3 = x2 + gate * pp
    out_ref[...] = _rms(x3, gfin_ref[...])


def _moe_combine(ws, ok, dest_t, pos_t, yb, w_t, x1, p2d, g_ple, w_ple_gate, w_ple_proj, g_final):
    T, D = x1.shape
    tc = min(COMB_TILE, T)
    n_tiles = T // tc
    ple = p2d.shape[1]
    smem_spec = pl.BlockSpec((TOP_K, tc), lambda i, *_: (0, i), memory_space=pltpu.SMEM)
    smem_next = pl.BlockSpec((TOP_K, tc), lambda i, *_: (0, jnp.minimum(i + 1, n_tiles - 1)),
                             memory_space=pltpu.SMEM)
    top_spec = pl.BlockSpec((TOP_K, tc), lambda i, *_: (0, i))
    const = lambda shape: pl.BlockSpec(shape, lambda i, *_: (0,) * len(shape),
                                       pipeline_mode=pl.Buffered(1))
    row = lambda a: a.reshape(1, -1)
    return pl.pallas_call(
        _comb_body,
        grid_spec=pltpu.PrefetchScalarGridSpec(
            num_scalar_prefetch=2,
            grid=(n_tiles,),
            in_specs=[smem_spec, smem_next,
                      pl.BlockSpec(memory_space=pl.ANY),
                      top_spec, top_spec,
                      pl.BlockSpec((tc, D), lambda i, *_: (i, 0)),
                      pl.BlockSpec((tc, ple), lambda i, *_: (i, 0)),
                      const((1, D)), const((D, D)), const((ple, D)), const((1, D))],
            out_specs=pl.BlockSpec((tc, D), lambda i, *_: (i, 0)),
            scratch_shapes=[pltpu.VMEM((2, N_EXPERTS * COMB_WINDOW, D), F32),
                            pltpu.SemaphoreType.DMA((2,))]),
        out_shape=jax.ShapeDtypeStruct((T, D), F32),
        compiler_params=pltpu.CompilerParams(dimension_semantics=("arbitrary",),
                                             vmem_limit_bytes=VMEM_LIMIT_BYTES),
        name="moe_combine",
    )(ws, ok, dest_t, dest_t, yb, pos_t, w_t, x1, p2d, row(g_ple), w_ple_gate.astype(BF16),
      w_ple_proj.astype(BF16), row(g_final))


def _combine_windows(pstart, trun, counts, idx_t, dest_t, tile, n_rows):
    n_tiles = trun.shape[0]
    tcnt = jnp.concatenate([trun[1:], counts[None, :]], axis=0) - trun
    start = pstart[None, :] + trun
    ws = jnp.minimum(start // SUBLANES * SUBLANES, n_rows - COMB_WINDOW).astype(I32)
    ok = jnp.all(start + tcnt - ws <= COMB_WINDOW, axis=1).astype(I32)
    tile_of = jnp.arange(idx_t.shape[1], dtype=I32) // tile
    ws_flat = ws.reshape(-1)
    pos_t = idx_t * COMB_WINDOW + dest_t - ws_flat[tile_of[None, :] * N_EXPERTS + idx_t]
    return ws_flat, ok, pos_t.astype(I32)


def _group_layout(counts, n_blocks):
    te = EXPERT_TILE
    padded = (counts + te - 1) // te * te
    pends = jnp.cumsum(padded)
    pstart = (pends - padded).astype(I32)
    nused = (pends[-1] // te).astype(I32)
    tail = nused + jnp.arange(N_EXPERTS, dtype=I32)
    zstart = jnp.concatenate([jnp.where(padded > 0, pends - te, -1),
                              jnp.where(tail < n_blocks, tail * te, -1)]).astype(I32)
    blk_src = jnp.minimum(jnp.arange(n_blocks, dtype=I32), jnp.maximum(nused - 1, 0))
    blk_e = jnp.sum((blk_src * te)[:, None] >= pends[None, :], axis=1)
    blk_e = jnp.clip(blk_e, 0, N_EXPERTS - 1).astype(I32)
    return pstart, zstart, blk_e, blk_src, nused.reshape(1)


def kernel(x, p, positions, g_mix_norm, w_in, g_ret_norm, w_pool, pool_scale, w_branch, w_out,
           g_ffn_norm, w_router, b_router, w_gate_up, b_gate_up, w_down, b_down,
           g_ple_norm, w_ple_gate, w_ple_proj, g_final):
    B, S, D = x.shape
    depth = w_in.shape[0]
    T = B * S
    xt = x.reshape(T, D)
    cos, sin = _rope_tables(positions.reshape(T, 1))
    n_blocks = (T * TOP_K) // EXPERT_TILE + N_EXPERTS
    for i in range(depth):
        x1, h2, idx_t, w_t, rank_t, cnt, trun = _token_mix(
            xt, cos, sin, g_mix_norm[i], w_in[i], g_ret_norm[i], w_pool[i], pool_scale[i],
            w_branch[i], w_out[i], g_ffn_norm[i], w_router[i], b_router[i], S)
        pstart, zstart, blk_e, blk_src, nused = _group_layout(cnt[:, 0], n_blocks)
        dest_t = pstart[idx_t] + rank_t
        xs = _moe_dispatch(h2, dest_t, zstart, n_blocks * EXPERT_TILE)
        yb = _moe_experts(xs, blk_e, blk_src, nused, w_gate_up[i], b_gate_up[i], w_down[i], b_down[i])
        assert depth == 1
        assert min(MIX_TILE, S) == min(COMB_TILE, T)
        ws, ok, pos_t = _combine_windows(pstart, trun[:, :, 0], cnt[:, 0], idx_t, dest_t,
                                         min(COMB_TILE, T), n_blocks * EXPERT_TILE)
        xt = _moe_combine(ws, ok, dest_t, pos_t, yb, w_t, x1, p[i].reshape(T, -1), g_ple_norm[i],
                          w_ple_gate[i], w_ple_proj[i], g_final)
    return xt.reshape(B, S, D)
```

```python
import functools

import numpy as np
import jax
import jax.numpy as jnp
from jax import lax
from jax.experimental import pallas as pl
from jax.experimental.pallas import tpu as pltpu

F32 = jnp.float32
BF16 = jnp.bfloat16
I32 = jnp.int32

RET_HEADS = 8
RET_HEAD_DIM = 128
ROPE_BASE = 10000.0
GN_EPS = 1e-5
RMS_EPS = 1e-6
POOL_WINDOWS = (2, 4, 8, 16)
N_EXPERTS = 32
TOP_K = 4
SWIGLU_ALPHA = 1.702
SWIGLU_LIMIT = 7.0

LANES = 128
SUBLANES = 8
VMEM_LIMIT_BYTES = 56 * 1024 * 1024

MIX_TILE = 256
ROPE_TILE = 1024
DISP_TILE = 512
EXPERT_TILE = 256
COMB_TILE = 256
POOL_HALO = 16
COMB_WINDOW = 64
ROW_UNROLL = 4


def _const_spec(shape):
    nd = len(shape)
    return pl.BlockSpec(shape, lambda *_: (0,) * nd, pipeline_mode=pl.Buffered(1))


def _rms(x, g):
    return x * lax.rsqrt(jnp.mean(x * x, axis=-1, keepdims=True) + RMS_EPS) * g


def _dot(a, b):
    return jnp.dot(a, b, preferred_element_type=F32)


def _dot_nt(a, b, precision=None):
    return lax.dot_general(a, b, (((1,), (1,)), ((), ())),
                           preferred_element_type=F32, precision=precision)


def _dot_tn(a, b):
    return lax.dot_general(a, b, (((0,), (0,)), ((), ())), preferred_element_type=F32)


def _rope_body(pos_ref, inv_ref, sign_ref, cos_ref, sin_ref):
    ang = pos_ref[...].astype(F32) * inv_ref[...]
    cos_ref[...] = jnp.cos(ang)
    sin_ref[...] = jnp.sin(ang) * sign_ref[...]


def _rope_tables(pos_col):
    T = pos_col.shape[0]
    half = RET_HEAD_DIM // 2
    inv = ROPE_BASE ** (-jnp.arange(half, dtype=F32) / half)
    inv_full = jnp.concatenate([inv, inv]).reshape(1, RET_HEAD_DIM)
    sign = jnp.concatenate([-jnp.ones((half,), F32), jnp.ones((half,), F32)]).reshape(1, RET_HEAD_DIM)
    tile = min(ROPE_TILE, T)
    return pl.pallas_call(
        _rope_body,
        grid=(T // tile,),
        in_specs=[pl.BlockSpec((tile, 1), lambda i: (i, 0)),
                  pl.BlockSpec((1, RET_HEAD_DIM), lambda i: (0, 0)),
                  pl.BlockSpec((1, RET_HEAD_DIM), lambda i: (0, 0))],
        out_specs=[pl.BlockSpec((tile, RET_HEAD_DIM), lambda i: (i, 0)),
                   pl.BlockSpec((tile, RET_HEAD_DIM), lambda i: (i, 0))],
        out_shape=[jax.ShapeDtypeStruct((T, RET_HEAD_DIM), F32),
                   jax.ShapeDtypeStruct((T, RET_HEAD_DIM), F32)],
        name="rope_tables",
    )(pos_col, inv_full, sign)


def _retention_constants(tile):
    h = np.arange(RET_HEADS, dtype=np.float64)
    log_gamma = np.log1p(-np.exp2(-5.0 - h))
    idx = np.arange(tile, dtype=np.float64)
    diff = idx[:, None] - idx[None, :]
    dmat = np.where(diff >= 0, np.exp(log_gamma[:, None, None] * np.maximum(diff, 0.0)[None]), 0.0)
    xi = np.exp(log_gamma[:, None] * (idx + 1.0)[None])
    zeta = np.exp(log_gamma[:, None] * (tile - 1.0 - idx)[None])
    chunk_decay = np.exp(log_gamma * tile)
    xi_b = np.broadcast_to(xi[:, :, None], (RET_HEADS, tile, RET_HEAD_DIM))
    zeta_b = np.broadcast_to(zeta[:, :, None], (RET_HEADS, tile, RET_HEAD_DIM))
    return (jnp.asarray(dmat, F32), jnp.asarray(xi_b, F32), jnp.asarray(zeta_b, F32),
            tuple(float(c) for c in chunk_decay))


def _mix_body(x_ref, cos_ref, sin_ref, gmix_ref, win_ref, dmat_ref, xi_ref, zeta_ref, gret_ref,
              wpool_ref, pscale_ref, wbr_ref, wout_ref, gffn_ref, wrt_ref, br_ref, tri_ref,
              x1_ref, h2_ref, idx_ref, w_ref, rank_ref, cnt_ref, trun_ref,
              state_ref, ue_ref, run_ref, *, tiles_per_seq, chunk_decay):
    tm, d_model = x_ref.shape
    ret_width = RET_HEADS * RET_HEAD_DIM
    i = pl.program_id(0)
    seq_tile = i % tiles_per_seq

    @pl.when(seq_tile == 0)
    def _():
        state_ref[...] = jnp.zeros(state_ref.shape, F32)
        ue_ref[0:POOL_HALO, :] = jnp.zeros((POOL_HALO, ue_ref.shape[1]), F32)

    @pl.when(i == 0)
    def _():
        run_ref[...] = jnp.zeros(run_ref.shape, F32)

    x = x_ref[...]
    hb = _rms(x, gmix_ref[...]).astype(BF16)

    qkvg = _dot(hb, win_ref[:, 0:4 * ret_width])
    cos = cos_ref[...]
    sin = sin_ref[...]

    def rot(a):
        return a * cos + pltpu.roll(a, RET_HEAD_DIM // 2, 1) * sin

    ys = []
    for h in range(RET_HEADS):
        lo = h * RET_HEAD_DIM
        q = rot(qkvg[:, lo:lo + RET_HEAD_DIM]).astype(BF16)
        k = (rot(qkvg[:, ret_width + lo:ret_width + lo + RET_HEAD_DIM])
             * (RET_HEAD_DIM ** -0.5)).astype(BF16)
        v = qkvg[:, 2 * ret_width + lo:2 * ret_width + lo + RET_HEAD_DIM]
        g = qkvg[:, 3 * ret_width + lo:3 * ret_width + lo + RET_HEAD_DIM]
        scores = _dot_nt(q, k) * dmat_ref[h]
        inner = _dot(scores.astype(BF16), v.astype(BF16))
        st = state_ref[h]
        cross = _dot(q, st.astype(BF16)) * xi_ref[h]
        state_ref[h] = st * chunk_decay[h] + _dot_tn(k, (v * zeta_ref[h]).astype(BF16))
        ret = inner + cross
        mu = jnp.mean(ret, axis=-1, keepdims=True)
        dev = ret - mu
        var = jnp.mean(dev * dev, axis=-1, keepdims=True)
        rn = dev * lax.rsqrt(var + GN_EPS) * gret_ref[:, lo:lo + RET_HEAD_DIM]
        ys.append(((g * jax.nn.sigmoid(g)) * rn).astype(BF16))
    y_ret = jnp.concatenate(ys, axis=1)
    branch_a = _dot(y_ret, wbr_ref[0])

    u = _dot(hb, win_ref[:, 4 * ret_width:4 * ret_width + d_model])
    ue_ref[POOL_HALO:POOL_HALO + tm, :] = u
    pos = seq_tile * tm + lax.broadcasted_iota(I32, (tm, 1), 0)
    group = d_model // len(POOL_WINDOWS)
    outs = []
    for gi, w in enumerate(POOL_WINDOWS):
        c0 = gi * group
        ug = u[:, c0:c0 + group]
        acc = ug
        for s in range(1, w):
            acc = acc + ue_ref[POOL_HALO - s:POOL_HALO - s + tm, c0:c0 + group]
        inv_count = 1.0 / jnp.minimum(pos + 1, w).astype(F32)
        mixed = (acc * inv_count - ug).astype(BF16)
        outs.append(_dot(mixed, wpool_ref[gi]))
    ue_ref[0:POOL_HALO, :] = ue_ref[tm:tm + POOL_HALO, :]
    y_pool = (jnp.concatenate(outs, axis=1) * pscale_ref[...]).astype(BF16)
    branch_b = _dot(y_pool, wbr_ref[1])

    gates = _dot(hb, win_ref[:, 4 * ret_width + d_model:4 * ret_width + 3 * d_model])
    merged = (jax.nn.sigmoid(gates[:, 0:d_model]) * branch_a
              + jax.nn.sigmoid(gates[:, d_model:2 * d_model]) * branch_b)
    x1 = x + _dot(merged.astype(BF16), wout_ref[...])
    x1_ref[...] = x1

    h2 = _rms(x1, gffn_ref[...])
    h2_ref[...] = h2
    logits = _dot_nt(wrt_ref[...], h2, precision=lax.Precision.HIGHEST) + br_ref[...]
    n_exp = logits.shape[0]
    eiota = lax.broadcasted_iota(I32, (n_exp, tm), 0)
    vals, idxs = [], []
    l = logits
    for _ in range(TOP_K):
        m = jnp.max(l, axis=0, keepdims=True)
        sel = jnp.min(jnp.where(l == m, eiota, n_exp), axis=0, keepdims=True)
        vals.append(m)
        idxs.append(sel)
        l = jnp.where(eiota == sel, -jnp.inf, l)
    exps = [jnp.exp(v - vals[0]) for v in vals]
    denom = exps[0] + exps[1] + exps[2] + exps[3]
    inv_denom = 1.0 / denom
    onehot = jnp.zeros((n_exp, tm), F32)
    for kk in range(TOP_K):
        onehot = onehot + (eiota == idxs[kk]).astype(F32)
    base = _dot(onehot.astype(BF16), tri_ref[...]) + run_ref[:, 0:1]
    for kk in range(TOP_K):
        idx_ref[kk:kk + 1, :] = idxs[kk]
        w_ref[kk:kk + 1, :] = exps[kk] * inv_denom
        rank = jnp.sum(jnp.where(eiota == idxs[kk], base, 0.0), axis=0, keepdims=True)
        rank_ref[kk:kk + 1, :] = rank.astype(I32)
    trun_ref[...] = run_ref[...].astype(I32)
    run = run_ref[...] + jnp.sum(onehot, axis=1, keepdims=True)
    run_ref[...] = run
    cnt_ref[...] = run.astype(I32)


def _token_mix(x2d, cos, sin, g_mix, w_in, g_ret, w_pool, pool_scale, w_branch, w_out,
               g_ffn, w_router, b_router, seq_len):
    T, D = x2d.shape
    tm = min(MIX_TILE, seq_len)
    in_width = w_in.shape[1]
    ret_width = RET_HEADS * RET_HEAD_DIM
    dmat, xi_b, zeta_b, chunk_decay = _retention_constants(tm)
    tri = jnp.asarray(np.triu(np.ones((tm, tm), np.float32), 1), BF16)
    row = lambda a: a.reshape(1, -1)
    tile_spec = lambda w: pl.BlockSpec((tm, w), lambda i: (i, 0))
    top_spec = pl.BlockSpec((TOP_K, tm), lambda i: (0, i))
    body = functools.partial(_mix_body, tiles_per_seq=seq_len // tm, chunk_decay=chunk_decay)
    return pl.pallas_call(
        body,
        grid=(T // tm,),
        in_specs=[tile_spec(D), tile_spec(RET_HEAD_DIM), tile_spec(RET_HEAD_DIM),
                  _const_spec((1, D)), _const_spec((D, in_width)),
                  _const_spec(dmat.shape), _const_spec(xi_b.shape), _const_spec(zeta_b.shape),
                  _const_spec((1, ret_width)), _const_spec(w_pool.shape), _const_spec((1, D)),
                  _const_spec(w_branch.shape), _const_spec((D, D)), _const_spec((1, D)),
                  _const_spec((N_EXPERTS, D)), _const_spec((N_EXPERTS, 1)), _const_spec((tm, tm))],
        out_specs=[tile_spec(D), tile_spec(D), top_spec, top_spec, top_spec,
                   pl.BlockSpec((N_EXPERTS, LANES), lambda i: (0, 0)),
                   pl.BlockSpec((None, N_EXPERTS, LANES), lambda i: (i, 0, 0))],
        out_shape=[jax.ShapeDtypeStruct((T, D), F32), jax.ShapeDtypeStruct((T, D), F32),
                   jax.ShapeDtypeStruct((TOP_K, T), I32), jax.ShapeDtypeStruct((TOP_K, T), F32),
                   jax.ShapeDtypeStruct((TOP_K, T), I32),
                   jax.ShapeDtypeStruct((N_EXPERTS, LANES), I32),
                   jax.ShapeDtypeStruct((T // tm, N_EXPERTS, LANES), I32)],
        scratch_shapes=[pltpu.VMEM((RET_HEADS, RET_HEAD_DIM, RET_HEAD_DIM), F32),
                        pltpu.VMEM((tm + POOL_HALO, D), F32),
                        pltpu.VMEM((N_EXPERTS, LANES), F32)],
        compiler_params=pltpu.CompilerParams(dimension_semantics=("arbitrary",),
                                             vmem_limit_bytes=VMEM_LIMIT_BYTES),
        name="token_mix",
    )(x2d, cos, sin, row(g_mix), w_in.astype(BF16), dmat, xi_b, zeta_b, row(g_ret),
      w_pool.astype(BF16), row(pool_scale), w_branch.astype(BF16), w_out.astype(BF16),
      row(g_ffn), w_router.T, b_router.reshape(-1, 1), tri)


def _disp_body(zstart_ref, h2_ref, dest_ref, xs_ref, zbuf_ref, sem, zsem):
    td = h2_ref.shape[0]
    te = zbuf_ref.shape[0]

    def zero_copy(e):
        start = pl.multiple_of(jnp.maximum(zstart_ref[e], 0), te)
        return pltpu.make_async_copy(zbuf_ref, xs_ref.at[pl.ds(start, te)], zsem)

    @pl.when(pl.program_id(0) == 0)
    def _():
        zbuf_ref[...] = jnp.zeros(zbuf_ref.shape, F32)

        def start(e, c):
            @pl.when(zstart_ref[e] >= 0)
            def _():
                zero_copy(e).start()
            return c

        def wait(e, c):
            @pl.when(zstart_ref[e] >= 0)
            def _():
                zero_copy(e).wait()
            return c

        lax.fori_loop(0, zstart_ref.shape[0], start, 0)
        lax.fori_loop(0, zstart_ref.shape[0], wait, 0)

    def row_copy(t, dest):
        return pltpu.make_async_copy(h2_ref.at[pl.ds(t, 1)], xs_ref.at[pl.ds(dest, 1)], sem)

    def start(j, c):
        ts = [j * ROW_UNROLL + u for u in range(ROW_UNROLL)]
        dests = [[dest_ref[kk, t] for kk in range(TOP_K)] for t in ts]
        for t, dest in zip(ts, dests):
            for kk in range(TOP_K):
                row_copy(t, dest[kk]).start(priority=kk % 2)
        return c

    lax.fori_loop(0, td // ROW_UNROLL, start, 0)
    for kk in range(TOP_K):
        pltpu.make_async_copy(h2_ref, xs_ref.at[pl.ds(0, td)], sem).wait()


def _moe_dispatch(h2, dest_t, zstart, n_rows):
    T, D = h2.shape
    td = min(DISP_TILE, T)
    smem_spec = pl.BlockSpec((TOP_K, td), lambda i, *_: (0, i), memory_space=pltpu.SMEM)
    return pl.pallas_call(
        _disp_body,
        grid_spec=pltpu.PrefetchScalarGridSpec(
            num_scalar_prefetch=1,
            grid=(T // td,),
            in_specs=[pl.BlockSpec((td, D), lambda i, *_: (i, 0)), smem_spec],
            out_specs=pl.BlockSpec(memory_space=pl.ANY),
            scratch_shapes=[pltpu.VMEM((EXPERT_TILE, D), F32),
                            pltpu.SemaphoreType.DMA, pltpu.SemaphoreType.DMA]),
        out_shape=jax.ShapeDtypeStruct((n_rows, D), F32),
        compiler_params=pltpu.CompilerParams(dimension_semantics=("arbitrary",),
                                             vmem_limit_bytes=VMEM_LIMIT_BYTES),
        name="moe_dispatch",
    )(zstart, h2, dest_t)


def _expert_body(blk_e_ref, blk_src_ref, nused_ref, xs_ref, wgu_ref, bgu_ref, wd_ref, bd_ref,
                 yb_ref, wgu_bf_ref, wd_bf_ref):
    i = pl.program_id(0)
    d_ff = wd_ref.shape[0]
    prev_e = blk_e_ref[jnp.maximum(i - 1, 0)]

    @pl.when((i == 0) | (blk_e_ref[i] != prev_e))
    def _():
        wgu_bf_ref[...] = wgu_ref[...].astype(BF16)
        wd_bf_ref[...] = wd_ref[...].astype(BF16)

    @pl.when(i < nused_ref[0])
    def _():
        gu = _dot(xs_ref[...].astype(BF16), wgu_bf_ref[...]) + bgu_ref[...]
        gate = jnp.minimum(gu[:, 0:d_ff], SWIGLU_LIMIT)
        up = jnp.clip(gu[:, d_ff:2 * d_ff], -SWIGLU_LIMIT, SWIGLU_LIMIT)
        act = (up + 1.0) * (gate * jax.nn.sigmoid(gate * SWIGLU_ALPHA))
        yb_ref[...] = _dot(act.astype(BF16), wd_bf_ref[...]) + bd_ref[...]

    @pl.when(i >= nused_ref[0])
    def _():
        yb_ref[...] = jnp.zeros(yb_ref.shape, F32)


def _moe_experts(xs, blk_e, blk_src, nused, w_gate_up, b_gate_up, w_down, b_down):
    n_rows, D = xs.shape
    E, _, two_ff = w_gate_up.shape
    d_ff = two_ff // 2
    te = EXPERT_TILE
    return pl.pallas_call(
        _expert_body,
        grid_spec=pltpu.PrefetchScalarGridSpec(
            num_scalar_prefetch=3,
            grid=(n_rows // te,),
            in_specs=[pl.BlockSpec((te, D), lambda i, be, bs, nu: (bs[i], 0)),
                      pl.BlockSpec((None, D, two_ff), lambda i, be, bs, nu: (be[i], 0, 0)),
                      pl.BlockSpec((None, 1, two_ff), lambda i, be, bs, nu: (be[i], 0, 0)),
                      pl.BlockSpec((None, d_ff, D), lambda i, be, bs, nu: (be[i], 0, 0)),
                      pl.BlockSpec((None, 1, D), lambda i, be, bs, nu: (be[i], 0, 0))],
            out_specs=pl.BlockSpec((te, D), lambda i, be, bs, nu: (i, 0)),
            scratch_shapes=[pltpu.VMEM((D, two_ff), BF16), pltpu.VMEM((d_ff, D), BF16)]),
        out_shape=jax.ShapeDtypeStruct((n_rows, D), F32),
        compiler_params=pltpu.CompilerParams(dimension_semantics=("arbitrary",),
                                             vmem_limit_bytes=VMEM_LIMIT_BYTES),
        name="moe_experts",
    )(blk_e, blk_src, nused, xs, w_gate_up, b_gate_up.reshape(E, 1, two_ff),
      w_down, b_down.reshape(E, 1, D))


def _comb_body(ws_ref, ok_ref, dest_ref, destn_ref, yb_ref, pos_ref, w_ref, x1_ref, p_ref,
               gple_ref, wpg_ref, wpp_ref, gfin_ref, out_ref, gbuf_ref, sems):
    tc = x1_ref.shape[0]
    n_rows = gbuf_ref.shape[1]
    i = pl.program_id(0)
    slot = i % 2

    def window_copy(tile, e, s):
        src = pl.multiple_of(ws_ref[tile * N_EXPERTS + e], SUBLANES)
        return pltpu.make_async_copy(yb_ref.at[pl.ds(src, COMB_WINDOW)],
                                     gbuf_ref.at[s, pl.ds(e * COMB_WINDOW, COMB_WINDOW)], sems.at[s])

    def issue(tile, dest_r, s):
        @pl.when(ok_ref[tile] != 0)
        def _():
            for e in range(N_EXPERTS):
                window_copy(tile, e, s).start(priority=e % 2)

        @pl.when(ok_ref[tile] == 0)
        def _():
            def start(j, c):
                ts = [j * ROW_UNROLL + u for u in range(ROW_UNROLL)]
                srcs = [[dest_r[kk, t] for kk in range(TOP_K)] for t in ts]
                for t, src in zip(ts, srcs):
                    for kk in range(TOP_K):
                        pltpu.make_async_copy(yb_ref.at[pl.ds(src[kk], 1)],
                                              gbuf_ref.at[s, pl.ds(kk * tc + t, 1)],
                                              sems.at[s]).start(priority=kk % 2)
                return c

            lax.fori_loop(0, tc // ROW_UNROLL, start, 0)

    @pl.when(i == 0)
    def _():
        issue(0, dest_ref, 0)

    @pl.when(i + 1 < pl.num_programs(0))
    def _():
        issue(i + 1, destn_ref, 1 - slot)

    def to_cols(rows):
        pad = jnp.zeros((LANES - rows.shape[0], tc), F32)
        return jnp.concatenate([rows, pad], axis=0).T

    w_col = to_cols(w_ref[...])
    fast = ok_ref[i] != 0
    slow_pos = (lax.broadcasted_iota(I32, (TOP_K, tc), 0) * tc
                + lax.broadcasted_iota(I32, (TOP_K, tc), 1))
    pos_col = to_cols(jnp.where(fast, pos_ref[...], slow_pos).astype(F32))
    pp = _dot(p_ref[...].astype(BF16), wpp_ref[...])

    @pl.when(fast)
    def _():
        pltpu.make_async_copy(yb_ref.at[pl.ds(0, n_rows)], gbuf_ref.at[slot], sems.at[slot]).wait()

    @pl.when(jnp.logical_not(fast))
    def _():
        pltpu.make_async_copy(yb_ref.at[pl.ds(0, TOP_K * tc)], gbuf_ref.at[slot, pl.ds(0, TOP_K * tc)],
                              sems.at[slot]).wait()
        gbuf_ref[slot, pl.ds(TOP_K * tc, n_rows - TOP_K * tc), :] = jnp.zeros(
            (n_rows - TOP_K * tc, gbuf_ref.shape[2]), F32)

    ciota = lax.broadcasted_iota(I32, (tc, n_rows), 1).astype(F32)
    sel = jnp.zeros((tc, n_rows), F32)
    for kk in range(TOP_K):
        sel = sel + jnp.where(ciota == pos_col[:, kk:kk + 1], w_col[:, kk:kk + 1], 0.0)
    moe = _dot(sel.astype(BF16), gbuf_ref[slot].astype(BF16))
    x2 = x1_ref[...] + moe
    h3 = _rms(x2, gple_ref[...]).astype(BF16)
    gate = jax.nn.sigmoid(_dot(h3, wpg_ref[...]))
    x3 = x2 + gate * pp
    out_ref[...] = _rms(x3, gfin_ref[...])


def _moe_combine(ws, ok, dest_t, pos_t, yb, w_t, x1, p2d, g_ple, w_ple_gate, w_ple_proj, g_final):
    T, D = x1.shape
    tc = min(COMB_TILE, T)
    n_tiles = T // tc
    ple = p2d.shape[1]
    smem_spec = pl.BlockSpec((TOP_K, tc), lambda i, *_: (0, i), memory_space=pltpu.SMEM)
    smem_next = pl.BlockSpec((TOP_K, tc), lambda i, *_: (0, jnp.minimum(i + 1, n_tiles - 1)),
                             memory_space=pltpu.SMEM)
    top_spec = pl.BlockSpec((TOP_K, tc), lambda i, *_: (0, i))
    const = lambda shape: pl.BlockSpec(shape, lambda i, *_: (0,) * len(shape),
                                       pipeline_mode=pl.Buffered(1))
    row = lambda a: a.reshape(1, -1)
    return pl.pallas_call(
        _comb_body,
        grid_spec=pltpu.PrefetchScalarGridSpec(
            num_scalar_prefetch=2,
            grid=(n_tiles,),
            in_specs=[smem_spec, smem_next,
                      pl.BlockSpec(memory_space=pl.ANY),
                      top_spec, top_spec,
                      pl.BlockSpec((tc, D), lambda i, *_: (i, 0)),
                      pl.BlockSpec((tc, ple), lambda i, *_: (i, 0)),
                      const((1, D)), const((D, D)), const((ple, D)), const((1, D))],
            out_specs=pl.BlockSpec((tc, D), lambda i, *_: (i, 0)),
            scratch_shapes=[pltpu.VMEM((2, N_EXPERTS * COMB_WINDOW, D), F32),
                            pltpu.SemaphoreType.DMA((2,))]),
        out_shape=jax.ShapeDtypeStruct((T, D), F32),
        compiler_params=pltpu.CompilerParams(dimension_semantics=("arbitrary",),
                                             vmem_limit_bytes=VMEM_LIMIT_BYTES),
        name="moe_combine",
    )(ws, ok, dest_t, dest_t, yb, pos_t, w_t, x1, p2d, row(g_ple), w_ple_gate.astype(BF16),
      w_ple_proj.astype(BF16), row(g_final))


def _combine_windows(pstart, trun, counts, idx_t, dest_t, tile, n_rows):
    n_tiles = trun.shape[0]
    tcnt = jnp.concatenate([trun[1:], counts[None, :]], axis=0) - trun
    start = pstart[None, :] + trun
    ws = jnp.minimum(start // SUBLANES * SUBLANES, n_rows - COMB_WINDOW).astype(I32)
    ok = jnp.all(start + tcnt - ws <= COMB_WINDOW, axis=1).astype(I32)
    ws_tok = _select_expert(jnp.repeat(ws, tile, axis=0)[None], idx_t)
    pos_t = idx_t * COMB_WINDOW + dest_t - ws_tok
    return ws.reshape(-1), ok, pos_t.astype(I32)


def _select_expert(table, idx_t):
    onehot = idx_t[:, :, None] == jnp.arange(N_EXPERTS, dtype=I32)
    return jnp.sum(jnp.where(onehot, table, 0), axis=-1)


def _group_layout(counts, n_blocks):
    te = EXPERT_TILE
    padded = (counts + te - 1) // te * te
    pends = jnp.cumsum(padded)
    pstart = (pends - padded).astype(I32)
    nused = (pends[-1] // te).astype(I32)
    tail = nused + jnp.arange(N_EXPERTS, dtype=I32)
    zstart = jnp.concatenate([jnp.where(padded > 0, pends - te, -1),
                              jnp.where(tail < n_blocks, tail * te, -1)]).astype(I32)
    blk_src = jnp.minimum(jnp.arange(n_blocks, dtype=I32), jnp.maximum(nused - 1, 0))
    blk_e = jnp.sum((blk_src * te)[:, None] >= pends[None, :], axis=1)
    blk_e = jnp.clip(blk_e, 0, N_EXPERTS - 1).astype(I32)
    return pstart, zstart, blk_e, blk_src, nused.reshape(1)


def kernel(x, p, positions, g_mix_norm, w_in, g_ret_norm, w_pool, pool_scale, w_branch, w_out,
           g_ffn_norm, w_router, b_router, w_gate_up, b_gate_up, w_down, b_down,
           g_ple_norm, w_ple_gate, w_ple_proj, g_final):
    B, S, D = x.shape
    depth = w_in.shape[0]
    T = B * S
    xt = x.reshape(T, D)
    cos, sin = _rope_tables(positions.reshape(T, 1))
    n_blocks = (T * TOP_K) // EXPERT_TILE + N_EXPERTS
    for i in range(depth):
        x1, h2, idx_t, w_t, rank_t, cnt, trun = _token_mix(
            xt, cos, sin, g_mix_norm[i], w_in[i], g_ret_norm[i], w_pool[i], pool_scale[i],
            w_branch[i], w_out[i], g_ffn_norm[i], w_router[i], b_router[i], S)
        pstart, zstart, blk_e, blk_src, nused = _group_layout(cnt[:, 0], n_blocks)
        dest_t = _select_expert(pstart[None, None, :], idx_t) + rank_t
        xs = _moe_dispatch(h2, dest_t, zstart, n_blocks * EXPERT_TILE)
        yb = _moe_experts(xs, blk_e, blk_src, nused, w_gate_up[i], b_gate_up[i], w_down[i], b_down[i])
        assert depth == 1
        assert min(MIX_TILE, S) == min(COMB_TILE, T)
        ws, ok, pos_t = _combine_windows(pstart, trun[:, :, 0], cnt[:, 0], idx_t, dest_t,
                                         min(COMB_TILE, T), n_blocks * EXPERT_TILE)
        xt = _moe_combine(ws, ok, dest_t, pos_t, yb, w_t, x1, p[i].reshape(T, -1), g_ple_norm[i],
                          w_ple_gate[i], w_ple_proj[i], g_final)
    return xt.reshape(B, S, D)
```

```python
import functools

import numpy as np
import jax
import jax.numpy as jnp
from jax import lax
from jax.experimental import pallas as pl
from jax.experimental.pallas import tpu as pltpu

F32 = jnp.float32
BF16 = jnp.bfloat16
I32 = jnp.int32

RET_HEADS = 8
RET_HEAD_DIM = 128
ROPE_BASE = 10000.0
GN_EPS = 1e-5
RMS_EPS = 1e-6
POOL_WINDOWS = (2, 4, 8, 16)
N_EXPERTS = 32
TOP_K = 4
SWIGLU_ALPHA = 1.702
SWIGLU_LIMIT = 7.0

LANES = 128
SUBLANES = 8
VMEM_LIMIT_BYTES = 56 * 1024 * 1024

MIX_TILE = 256
ROPE_TILE = 1024
DISP_TILE = 512
EXPERT_TILE = 256
COMB_TILE = 256
POOL_HALO = 16
COMB_WINDOW = 64
ROW_UNROLL = 4


def _const_spec(shape):
    nd = len(shape)
    return pl.BlockSpec(shape, lambda *_: (0,) * nd, pipeline_mode=pl.Buffered(1))


def _rms(x, g):
    return x * lax.rsqrt(jnp.mean(x * x, axis=-1, keepdims=True) + RMS_EPS) * g


def _dot(a, b):
    return jnp.dot(a, b, preferred_element_type=F32)


def _dot_nt(a, b, precision=None):
    return lax.dot_general(a, b, (((1,), (1,)), ((), ())),
                           preferred_element_type=F32, precision=precision)


def _dot_tn(a, b):
    return lax.dot_general(a, b, (((0,), (0,)), ((), ())), preferred_element_type=F32)


def _rope_body(pos_ref, inv_ref, sign_ref, cos_ref, sin_ref):
    ang = pos_ref[...].astype(F32) * inv_ref[...]
    cos_ref[...] = jnp.cos(ang)
    sin_ref[...] = jnp.sin(ang) * sign_ref[...]


def _rope_tables(pos_col):
    T = pos_col.shape[0]
    half = RET_HEAD_DIM // 2
    inv = ROPE_BASE ** (-jnp.arange(half, dtype=F32) / half)
    inv_full = jnp.concatenate([inv, inv]).reshape(1, RET_HEAD_DIM)
    sign = jnp.concatenate([-jnp.ones((half,), F32), jnp.ones((half,), F32)]).reshape(1, RET_HEAD_DIM)
    tile = min(ROPE_TILE, T)
    return pl.pallas_call(
        _rope_body,
        grid=(T // tile,),
        in_specs=[pl.BlockSpec((tile, 1), lambda i: (i, 0)),
                  pl.BlockSpec((1, RET_HEAD_DIM), lambda i: (0, 0)),
                  pl.BlockSpec((1, RET_HEAD_DIM), lambda i: (0, 0))],
        out_specs=[pl.BlockSpec((tile, RET_HEAD_DIM), lambda i: (i, 0)),
                   pl.BlockSpec((tile, RET_HEAD_DIM), lambda i: (i, 0))],
        out_shape=[jax.ShapeDtypeStruct((T, RET_HEAD_DIM), F32),
                   jax.ShapeDtypeStruct((T, RET_HEAD_DIM), F32)],
        name="rope_tables",
    )(pos_col, inv_full, sign)


def _retention_constants(tile):
    h = np.arange(RET_HEADS, dtype=np.float64)
    log_gamma = np.log1p(-np.exp2(-5.0 - h))
    idx = np.arange(tile, dtype=np.float64)
    diff = idx[:, None] - idx[None, :]
    dmat = np.where(diff >= 0, np.exp(log_gamma[:, None, None] * np.maximum(diff, 0.0)[None]), 0.0)
    xi = np.exp(log_gamma[:, None] * (idx + 1.0)[None])
    zeta = np.exp(log_gamma[:, None] * (tile - 1.0 - idx)[None])
    chunk_decay = np.exp(log_gamma * tile)
    xi_b = np.broadcast_to(xi[:, :, None], (RET_HEADS, tile, RET_HEAD_DIM))
    zeta_b = np.broadcast_to(zeta[:, :, None], (RET_HEADS, tile, RET_HEAD_DIM))
    return (jnp.asarray(dmat, F32), jnp.asarray(xi_b, F32), jnp.asarray(zeta_b, F32),
            tuple(float(c) for c in chunk_decay))


def _mix_body(x_ref, cos_ref, sin_ref, gmix_ref, win_ref, dmat_ref, xi_ref, zeta_ref, gret_ref,
              wpool_ref, pscale_ref, wbr_ref, wout_ref, gffn_ref, wrt_ref, br_ref, tri_ref,
              x1_ref, h2_ref, idx_ref, w_ref, rank_ref, cnt_ref, trun_ref,
              state_ref, ue_ref, run_ref, *, tiles_per_seq, chunk_decay):
    tm, d_model = x_ref.shape
    ret_width = RET_HEADS * RET_HEAD_DIM
    i = pl.program_id(0)
    seq_tile = i % tiles_per_seq

    @pl.when(seq_tile == 0)
    def _():
        state_ref[...] = jnp.zeros(state_ref.shape, F32)
        ue_ref[0:POOL_HALO, :] = jnp.zeros((POOL_HALO, ue_ref.shape[1]), F32)

    @pl.when(i == 0)
    def _():
        run_ref[...] = jnp.zeros(run_ref.shape, F32)

    x = x_ref[...]
    hb = _rms(x, gmix_ref[...]).astype(BF16)

    qkvg = _dot(hb, win_ref[:, 0:4 * ret_width])
    cos = cos_ref[...]
    sin = sin_ref[...]

    def rot(a):
        return a * cos + pltpu.roll(a, RET_HEAD_DIM // 2, 1) * sin

    ys = []
    for h in range(RET_HEADS):
        lo = h * RET_HEAD_DIM
        q = rot(qkvg[:, lo:lo + RET_HEAD_DIM]).astype(BF16)
        k = (rot(qkvg[:, ret_width + lo:ret_width + lo + RET_HEAD_DIM])
             * (RET_HEAD_DIM ** -0.5)).astype(BF16)
        v = qkvg[:, 2 * ret_width + lo:2 * ret_width + lo + RET_HEAD_DIM]
        g = qkvg[:, 3 * ret_width + lo:3 * ret_width + lo + RET_HEAD_DIM]
        scores = _dot_nt(q, k) * dmat_ref[h]
        inner = _dot(scores.astype(BF16), v.astype(BF16))
        st = state_ref[h]
        cross = _dot(q, st.astype(BF16)) * xi_ref[h]
        state_ref[h] = st * chunk_decay[h] + _dot_tn(k, (v * zeta_ref[h]).astype(BF16))
        ret = inner + cross
        mu = jnp.mean(ret, axis=-1, keepdims=True)
        dev = ret - mu
        var = jnp.mean(dev * dev, axis=-1, keepdims=True)
        rn = dev * lax.rsqrt(var + GN_EPS) * gret_ref[:, lo:lo + RET_HEAD_DIM]
        ys.append(((g * jax.nn.sigmoid(g)) * rn).astype(BF16))
    y_ret = jnp.concatenate(ys, axis=1)
    branch_a = _dot(y_ret, wbr_ref[0])

    u = _dot(hb, win_ref[:, 4 * ret_width:4 * ret_width + d_model])
    ue_ref[POOL_HALO:POOL_HALO + tm, :] = u
    pos = seq_tile * tm + lax.broadcasted_iota(I32, (tm, 1), 0)
    group = d_model // len(POOL_WINDOWS)
    outs = []
    for gi, w in enumerate(POOL_WINDOWS):
        c0 = gi * group
        ug = u[:, c0:c0 + group]
        acc = ug
        for s in range(1, w):
            acc = acc + ue_ref[POOL_HALO - s:POOL_HALO - s + tm, c0:c0 + group]
        inv_count = 1.0 / jnp.minimum(pos + 1, w).astype(F32)
        mixed = (acc * inv_count - ug).astype(BF16)
        outs.append(_dot(mixed, wpool_ref[gi]))
    ue_ref[0:POOL_HALO, :] = ue_ref[tm:tm + POOL_HALO, :]
    y_pool = (jnp.concatenate(outs, axis=1) * pscale_ref[...]).astype(BF16)
    branch_b = _dot(y_pool, wbr_ref[1])

    gates = _dot(hb, win_ref[:, 4 * ret_width + d_model:4 * ret_width + 3 * d_model])
    merged = (jax.nn.sigmoid(gates[:, 0:d_model]) * branch_a
              + jax.nn.sigmoid(gates[:, d_model:2 * d_model]) * branch_b)
    x1 = x + _dot(merged.astype(BF16), wout_ref[...])
    x1_ref[...] = x1

    h2 = _rms(x1, gffn_ref[...])
    h2_ref[...] = h2
    logits = _dot_nt(wrt_ref[...], h2, precision=lax.Precision.HIGHEST) + br_ref[...]
    n_exp = logits.shape[0]
    eiota = lax.broadcasted_iota(I32, (n_exp, tm), 0)
    vals, idxs = [], []
    l = logits
    for _ in range(TOP_K):
        m = jnp.max(l, axis=0, keepdims=True)
        sel = jnp.min(jnp.where(l == m, eiota, n_exp), axis=0, keepdims=True)
        vals.append(m)
        idxs.append(sel)
        l = jnp.where(eiota == sel, -jnp.inf, l)
    exps = [jnp.exp(v - vals[0]) for v in vals]
    denom = exps[0] + exps[1] + exps[2] + exps[3]
    inv_denom = 1.0 / denom
    onehot = jnp.zeros((n_exp, tm), F32)
    for kk in range(TOP_K):
        onehot = onehot + (eiota == idxs[kk]).astype(F32)
    base = _dot(onehot.astype(BF16), tri_ref[...]) + run_ref[:, 0:1]
    for kk in range(TOP_K):
        idx_ref[kk:kk + 1, :] = idxs[kk]
        w_ref[kk:kk + 1, :] = exps[kk] * inv_denom
        rank = jnp.sum(jnp.where(eiota == idxs[kk], base, 0.0), axis=0, keepdims=True)
        rank_ref[kk:kk + 1, :] = rank.astype(I32)
    trun_ref[...] = run_ref[...].astype(I32)
    run = run_ref[...] + jnp.sum(onehot, axis=1, keepdims=True)
    run_ref[...] = run
    cnt_ref[...] = run.astype(I32)


def _token_mix(x2d, cos, sin, g_mix, w_in, g_ret, w_pool, pool_scale, w_branch, w_out,
               g_ffn, w_router, b_router, seq_len):
    T, D = x2d.shape
    tm = min(MIX_TILE, seq_len)
    in_width = w_in.shape[1]
    ret_width = RET_HEADS * RET_HEAD_DIM
    dmat, xi_b, zeta_b, chunk_decay = _retention_constants(tm)
    tri = jnp.asarray(np.triu(np.ones((tm, tm), np.float32), 1), BF16)
    row = lambda a: a.reshape(1, -1)
    tile_spec = lambda w: pl.BlockSpec((tm, w), lambda i: (i, 0))
    top_spec = pl.BlockSpec((TOP_K, tm), lambda i: (0, i))
    body = functools.partial(_mix_body, tiles_per_seq=seq_len // tm, chunk_decay=chunk_decay)
    return pl.pallas_call(
        body,
        grid=(T // tm,),
        in_specs=[tile_spec(D), tile_spec(RET_HEAD_DIM), tile_spec(RET_HEAD_DIM),
                  _const_spec((1, D)), _const_spec((D, in_width)),
                  _const_spec(dmat.shape), _const_spec(xi_b.shape), _const_spec(zeta_b.shape),
                  _const_spec((1, ret_width)), _const_spec(w_pool.shape), _const_spec((1, D)),
                  _const_spec(w_branch.shape), _const_spec((D, D)), _const_spec((1, D)),
                  _const_spec((N_EXPERTS, D)), _const_spec((N_EXPERTS, 1)), _const_spec((tm, tm))],
        out_specs=[tile_spec(D), tile_spec(D), top_spec, top_spec, top_spec,
                   pl.BlockSpec((N_EXPERTS, LANES), lambda i: (0, 0)),
                   pl.BlockSpec((None, N_EXPERTS, LANES), lambda i: (i, 0, 0))],
        out_shape=[jax.ShapeDtypeStruct((T, D), F32), jax.ShapeDtypeStruct((T, D), F32),
                   jax.ShapeDtypeStruct((TOP_K, T), I32), jax.ShapeDtypeStruct((TOP_K, T), F32),
                   jax.ShapeDtypeStruct((TOP_K, T), I32),
                   jax.ShapeDtypeStruct((N_EXPERTS, LANES), I32),
                   jax.ShapeDtypeStruct((T // tm, N_EXPERTS, LANES), I32)],
        scratch_shapes=[pltpu.VMEM((RET_HEADS, RET_HEAD_DIM, RET_HEAD_DIM), F32),
                        pltpu.VMEM((tm + POOL_HALO, D), F32),
                        pltpu.VMEM((N_EXPERTS, LANES), F32)],
        compiler_params=pltpu.CompilerParams(dimension_semantics=("arbitrary",),
                                             vmem_limit_bytes=VMEM_LIMIT_BYTES),
        name="token_mix",
    )(x2d, cos, sin, row(g_mix), w_in.astype(BF16), dmat, xi_b, zeta_b, row(g_ret),
      w_pool.astype(BF16), row(pool_scale), w_branch.astype(BF16), w_out.astype(BF16),
      row(g_ffn), w_router.T, b_router.reshape(-1, 1), tri)


def _disp_body(zstart_ref, h2_ref, dest_ref, xs_ref, zbuf_ref, sem, zsem):
    td = h2_ref.shape[0]
    te = zbuf_ref.shape[0]

    def zero_copy(e):
        start = pl.multiple_of(jnp.maximum(zstart_ref[e], 0), te)
        return pltpu.make_async_copy(zbuf_ref, xs_ref.at[pl.ds(start, te)], zsem)

    @pl.when(pl.program_id(0) == 0)
    def _():
        zbuf_ref[...] = jnp.zeros(zbuf_ref.shape, F32)

        def start(e, c):
            @pl.when(zstart_ref[e] >= 0)
            def _():
                zero_copy(e).start()
            return c

        def wait(e, c):
            @pl.when(zstart_ref[e] >= 0)
            def _():
                zero_copy(e).wait()
            return c

        lax.fori_loop(0, zstart_ref.shape[0], start, 0)
        lax.fori_loop(0, zstart_ref.shape[0], wait, 0)

    def row_copy(t, dest):
        return pltpu.make_async_copy(h2_ref.at[pl.ds(t, 1)], xs_ref.at[pl.ds(dest, 1)], sem)

    def start(j, c):
        ts = [j * ROW_UNROLL + u for u in range(ROW_UNROLL)]
        dests = [[dest_ref[kk, t] for kk in range(TOP_K)] for t in ts]
        for t, dest in zip(ts, dests):
            for kk in range(TOP_K):
                row_copy(t, dest[kk]).start(priority=kk % 2)
        return c

    lax.fori_loop(0, td // ROW_UNROLL, start, 0)
    for kk in range(TOP_K):
        pltpu.make_async_copy(h2_ref, xs_ref.at[pl.ds(0, td)], sem).wait()


def _moe_dispatch(h2, dest_t, zstart, n_rows):
    T, D = h2.shape
    td = min(DISP_TILE, T)
    smem_spec = pl.BlockSpec((TOP_K, td), lambda i, *_: (0, i), memory_space=pltpu.SMEM)
    return pl.pallas_call(
        _disp_body,
        grid_spec=pltpu.PrefetchScalarGridSpec(
            num_scalar_prefetch=1,
            grid=(T // td,),
            in_specs=[pl.BlockSpec((td, D), lambda i, *_: (i, 0)), smem_spec],
            out_specs=pl.BlockSpec(memory_space=pl.ANY),
            scratch_shapes=[pltpu.VMEM((EXPERT_TILE, D), F32),
                            pltpu.SemaphoreType.DMA, pltpu.SemaphoreType.DMA]),
        out_shape=jax.ShapeDtypeStruct((n_rows, D), F32),
        compiler_params=pltpu.CompilerParams(dimension_semantics=("arbitrary",),
                                             vmem_limit_bytes=VMEM_LIMIT_BYTES),
        name="moe_dispatch",
    )(zstart, h2, dest_t)


def _expert_body(blk_e_ref, blk_src_ref, nused_ref, first_ref, next_e_ref, xs_ref, wgu_hbm, bgu_ref,
                 wd_hbm, bd_ref, yb_ref, wgu_f32_ref, wd_f32_ref, wgu_bf_ref, wd_bf_ref, wsem):
    i = pl.program_id(0)
    d_ff = wd_bf_ref.shape[0]

    def weight_copies(e):
        return (pltpu.make_async_copy(wgu_hbm.at[pl.ds(e, 1)], wgu_f32_ref, wsem.at[0]),
                pltpu.make_async_copy(wd_hbm.at[pl.ds(e, 1)], wd_f32_ref, wsem.at[1]))

    @pl.when(i == 0)
    def _():
        for cp in weight_copies(blk_e_ref[0]):
            cp.start()

    @pl.when(first_ref[i] != 0)
    def _():
        for cp in weight_copies(blk_e_ref[i]):
            cp.wait()
        wgu_bf_ref[...] = wgu_f32_ref[0].astype(BF16)
        wd_bf_ref[...] = wd_f32_ref[0].astype(BF16)

    @pl.when((first_ref[i] != 0) & (next_e_ref[i] >= 0))
    def _():
        for cp in weight_copies(next_e_ref[i]):
            cp.start()

    @pl.when(i < nused_ref[0])
    def _():
        gu = _dot(xs_ref[...].astype(BF16), wgu_bf_ref[...]) + bgu_ref[...]
        gate = jnp.minimum(gu[:, 0:d_ff], SWIGLU_LIMIT)
        up = jnp.clip(gu[:, d_ff:2 * d_ff], -SWIGLU_LIMIT, SWIGLU_LIMIT)
        act = (up + 1.0) * (gate * jax.nn.sigmoid(gate * SWIGLU_ALPHA))
        yb_ref[...] = _dot(act.astype(BF16), wd_bf_ref[...]) + bd_ref[...]

    @pl.when(i >= nused_ref[0])
    def _():
        yb_ref[...] = jnp.zeros(yb_ref.shape, F32)


def _moe_experts(xs, blk_e, blk_src, nused, first, next_e, w_gate_up, b_gate_up, w_down, b_down):
    n_rows, D = xs.shape
    E, _, two_ff = w_gate_up.shape
    d_ff = two_ff // 2
    te = EXPERT_TILE
    return pl.pallas_call(
        _expert_body,
        grid_spec=pltpu.PrefetchScalarGridSpec(
            num_scalar_prefetch=5,
            grid=(n_rows // te,),
            in_specs=[pl.BlockSpec((te, D), lambda i, be, bs, *_: (bs[i], 0)),
                      pl.BlockSpec(memory_space=pl.ANY),
                      pl.BlockSpec((None, 1, two_ff), lambda i, be, *_: (be[i], 0, 0)),
                      pl.BlockSpec(memory_space=pl.ANY),
                      pl.BlockSpec((None, 1, D), lambda i, be, *_: (be[i], 0, 0))],
            out_specs=pl.BlockSpec((te, D), lambda i, *_: (i, 0)),
            scratch_shapes=[pltpu.VMEM((1, D, two_ff), F32), pltpu.VMEM((1, d_ff, D), F32),
                            pltpu.VMEM((D, two_ff), BF16), pltpu.VMEM((d_ff, D), BF16),
                            pltpu.SemaphoreType.DMA((2,))]),
        out_shape=jax.ShapeDtypeStruct((n_rows, D), F32),
        compiler_params=pltpu.CompilerParams(dimension_semantics=("arbitrary",),
                                             vmem_limit_bytes=VMEM_LIMIT_BYTES),
        name="moe_experts",
    )(blk_e, blk_src, nused, first, next_e, xs, w_gate_up, b_gate_up.reshape(E, 1, two_ff),
      w_down, b_down.reshape(E, 1, D))


def _comb_body(ws_ref, ok_ref, dest_ref, destn_ref, yb_ref, pos_ref, w_ref, x1_ref, p_ref,
               gple_ref, wpg_ref, wpp_ref, gfin_ref, out_ref, gbuf_ref, sems):
    tc = x1_ref.shape[0]
    n_rows = gbuf_ref.shape[1]
    i = pl.program_id(0)
    slot = i % 2

    def window_copy(tile, e, s):
        src = pl.multiple_of(ws_ref[tile * N_EXPERTS + e], SUBLANES)
        return pltpu.make_async_copy(yb_ref.at[pl.ds(src, COMB_WINDOW)],
                                     gbuf_ref.at[s, pl.ds(e * COMB_WINDOW, COMB_WINDOW)], sems.at[s])

    def issue(tile, dest_r, s):
        @pl.when(ok_ref[tile] != 0)
        def _():
            for e in range(N_EXPERTS):
                window_copy(tile, e, s).start(priority=e % 2)

        @pl.when(ok_ref[tile] == 0)
        def _():
            def start(j, c):
                ts = [j * ROW_UNROLL + u for u in range(ROW_UNROLL)]
                srcs = [[dest_r[kk, t] for kk in range(TOP_K)] for t in ts]
                for t, src in zip(ts, srcs):
                    for kk in range(TOP_K):
                        pltpu.make_async_copy(yb_ref.at[pl.ds(src[kk], 1)],
                                              gbuf_ref.at[s, pl.ds(kk * tc + t, 1)],
                                              sems.at[s]).start(priority=kk % 2)
                return c

            lax.fori_loop(0, tc // ROW_UNROLL, start, 0)

    @pl.when(i == 0)
    def _():
        issue(0, dest_ref, 0)

    @pl.when(i + 1 < pl.num_programs(0))
    def _():
        issue(i + 1, destn_ref, 1 - slot)

    def to_cols(rows):
        pad = jnp.zeros((LANES - rows.shape[0], tc), F32)
        return jnp.concatenate([rows, pad], axis=0).T

    w_col = to_cols(w_ref[...])
    fast = ok_ref[i] != 0
    slow_pos = (lax.broadcasted_iota(I32, (TOP_K, tc), 0) * tc
                + lax.broadcasted_iota(I32, (TOP_K, tc), 1))
    pos_col = to_cols(jnp.where(fast, pos_ref[...], slow_pos).astype(F32))
    pp = _dot(p_ref[...].astype(BF16), wpp_ref[...])

    @pl.when(fast)
    def _():
        pltpu.make_async_copy(yb_ref.at[pl.ds(0, n_rows)], gbuf_ref.at[slot], sems.at[slot]).wait()

    @pl.when(jnp.logical_not(fast))
    def _():
        pltpu.make_async_copy(yb_ref.at[pl.ds(0, TOP_K * tc)], gbuf_ref.at[slot, pl.ds(0, TOP_K * tc)],
                              sems.at[slot]).wait()
        gbuf_ref[slot, pl.ds(TOP_K * tc, n_rows - TOP_K * tc), :] = jnp.zeros(
            (n_rows - TOP_K * tc, gbuf_ref.shape[2]), F32)

    ciota = lax.broadcasted_iota(I32, (tc, n_rows), 1).astype(F32)
    sel = jnp.zeros((tc, n_rows), F32)
    for kk in range(TOP_K):
        sel = sel + jnp.where(ciota == pos_col[:, kk:kk + 1], w_col[:, kk:kk + 1], 0.0)
    moe = _dot(sel.astype(BF16), gbuf_ref[slot].astype(BF16))
    x2 = x1_ref[...] + moe
    h3 = _rms(x2, gple_ref[...]).astype(BF16)
    gate = jax.nn.sigmoid(_dot(h3, wpg_ref[...]))
    x3 = x2 + gate * pp
    out_ref[...] = _rms(x3, gfin_ref[...])


def _moe_combine(ws, ok, dest_t, pos_t, yb, w_t, x1, p2d, g_ple, w_ple_gate, w_ple_proj, g_final):
    T, D = x1.shape
    tc = min(COMB_TILE, T)
    n_tiles = T // tc
    ple = p2d.shape[1]
    smem_spec = pl.BlockSpec((TOP_K, tc), lambda i, *_: (0, i), memory_space=pltpu.SMEM)
    smem_next = pl.BlockSpec((TOP_K, tc), lambda i, *_: (0, jnp.minimum(i + 1, n_tiles - 1)),
                             memory_space=pltpu.SMEM)
    top_spec = pl.BlockSpec((TOP_K, tc), lambda i, *_: (0, i))
    const = lambda shape: pl.BlockSpec(shape, lambda i, *_: (0,) * len(shape),
                                       pipeline_mode=pl.Buffered(1))
    row = lambda a: a.reshape(1, -1)
    return pl.pallas_call(
        _comb_body,
        grid_spec=pltpu.PrefetchScalarGridSpec(
            num_scalar_prefetch=2,
            grid=(n_tiles,),
            in_specs=[smem_spec, smem_next,
                      pl.BlockSpec(memory_space=pl.ANY),
                      top_spec, top_spec,
                      pl.BlockSpec((tc, D), lambda i, *_: (i, 0)),
                      pl.BlockSpec((tc, ple), lambda i, *_: (i, 0)),
                      const((1, D)), const((D, D)), const((ple, D)), const((1, D))],
            out_specs=pl.BlockSpec((tc, D), lambda i, *_: (i, 0)),
            scratch_shapes=[pltpu.VMEM((2, N_EXPERTS * COMB_WINDOW, D), F32),
                            pltpu.SemaphoreType.DMA((2,))]),
        out_shape=jax.ShapeDtypeStruct((T, D), F32),
        compiler_params=pltpu.CompilerParams(dimension_semantics=("arbitrary",),
                                             vmem_limit_bytes=VMEM_LIMIT_BYTES),
        name="moe_combine",
    )(ws, ok, dest_t, dest_t, yb, pos_t, w_t, x1, p2d, row(g_ple), w_ple_gate.astype(BF16),
      w_ple_proj.astype(BF16), row(g_final))


def _combine_windows(pstart, trun, counts, idx_t, dest_t, tile, n_rows):
    n_tiles = trun.shape[0]
    tcnt = jnp.concatenate([trun[1:], counts[None, :]], axis=0) - trun
    start = pstart[None, :] + trun
    ws = jnp.minimum(start // SUBLANES * SUBLANES, n_rows - COMB_WINDOW).astype(I32)
    ok = jnp.all(start + tcnt - ws <= COMB_WINDOW, axis=1).astype(I32)
    ws_tok = _select_expert(jnp.repeat(ws, tile, axis=0)[None], idx_t)
    pos_t = idx_t * COMB_WINDOW + dest_t - ws_tok
    return ws.reshape(-1), ok, pos_t.astype(I32)


def _select_expert(table, idx_t):
    onehot = idx_t[:, :, None] == jnp.arange(N_EXPERTS, dtype=I32)
    return jnp.sum(jnp.where(onehot, table, 0), axis=-1)


def _group_layout(counts, n_blocks):
    te = EXPERT_TILE
    padded = (counts + te - 1) // te * te
    pends = jnp.cumsum(padded)
    pstart = (pends - padded).astype(I32)
    nused = (pends[-1] // te).astype(I32)
    tail = nused + jnp.arange(N_EXPERTS, dtype=I32)
    zstart = jnp.concatenate([jnp.where(padded > 0, pends - te, -1),
                              jnp.where(tail < n_blocks, tail * te, -1)]).astype(I32)
    blk_src = jnp.minimum(jnp.arange(n_blocks, dtype=I32), jnp.maximum(nused - 1, 0))
    blk_e = jnp.sum((blk_src * te)[:, None] >= pends[None, :], axis=1)
    blk_e = jnp.clip(blk_e, 0, N_EXPERTS - 1).astype(I32)
    blk = jnp.arange(n_blocks, dtype=I32)
    prev_e = jnp.concatenate([blk_e[:1], blk_e[:-1]])
    first = ((blk < nused) & ((blk == 0) | (blk_e != prev_e))).astype(I32)
    experts = jnp.arange(N_EXPERTS, dtype=I32)[None, :]
    later = (experts > blk_e[:, None]) & (counts[None, :] > 0)
    next_e = jnp.min(jnp.where(later, experts, N_EXPERTS), axis=1)
    next_e = jnp.where(next_e < N_EXPERTS, next_e, -1).astype(I32)
    return pstart, zstart, blk_e, blk_src, nused.reshape(1), first, next_e


def kernel(x, p, positions, g_mix_norm, w_in, g_ret_norm, w_pool, pool_scale, w_branch, w_out,
           g_ffn_norm, w_router, b_router, w_gate_up, b_gate_up, w_down, b_down,
           g_ple_norm, w_ple_gate, w_ple_proj, g_final):
    B, S, D = x.shape
    depth = w_in.shape[0]
    T = B * S
    xt = x.reshape(T, D)
    cos, sin = _rope_tables(positions.reshape(T, 1))
    n_blocks = (T * TOP_K) // EXPERT_TILE + N_EXPERTS
    for i in range(depth):
        x1, h2, idx_t, w_t, rank_t, cnt, trun = _token_mix(
            xt, cos, sin, g_mix_norm[i], w_in[i], g_ret_norm[i], w_pool[i], pool_scale[i],
            w_branch[i], w_out[i], g_ffn_norm[i], w_router[i], b_router[i], S)
        pstart, zstart, blk_e, blk_src, nused, first, next_e = _group_layout(cnt[:, 0], n_blocks)
        dest_t = _select_expert(pstart[None, None, :], idx_t) + rank_t
        xs = _moe_dispatch(h2, dest_t, zstart, n_blocks * EXPERT_TILE)
        yb = _moe_experts(xs, blk_e, blk_src, nused, first, next_e,
                          w_gate_up[i], b_gate_up[i], w_down[i], b_down[i])
        assert depth == 1
        assert min(MIX_TILE, S) == min(COMB_TILE, T)
        ws, ok, pos_t = _combine_windows(pstart, trun[:, :, 0], cnt[:, 0], idx_t, dest_t,
                                         min(COMB_TILE, T), n_blocks * EXPERT_TILE)
        xt = _moe_combine(ws, ok, dest_t, pos_t, yb, w_t, x1, p[i].reshape(T, -1), g_ple_norm[i],
                          w_ple_gate[i], w_ple_proj[i], g_final)
    return xt.reshape(B, S, D)
```

```python
import functools

import numpy as np
import jax
import jax.numpy as jnp
from jax import lax
from jax.experimental import pallas as pl
from jax.experimental.pallas import tpu as pltpu

F32 = jnp.float32
BF16 = jnp.bfloat16
I32 = jnp.int32

RET_HEADS = 8
RET_HEAD_DIM = 128
ROPE_BASE = 10000.0
GN_EPS = 1e-5
RMS_EPS = 1e-6
POOL_WINDOWS = (2, 4, 8, 16)
N_EXPERTS = 32
TOP_K = 4
SWIGLU_ALPHA = 1.702
SWIGLU_LIMIT = 7.0

LANES = 128
SUBLANES = 8
VMEM_LIMIT_BYTES = 56 * 1024 * 1024

MIX_TILE = 256
ROPE_TILE = 1024
DISP_TILE = 512
EXPERT_TILE = 256
COMB_TILE = 256
POOL_HALO = 16
COMB_WINDOW = 64
ROW_UNROLL = 4


def _const_spec(shape):
    nd = len(shape)
    return pl.BlockSpec(shape, lambda *_: (0,) * nd, pipeline_mode=pl.Buffered(1))


def _rms(x, g):
    return x * lax.rsqrt(jnp.mean(x * x, axis=-1, keepdims=True) + RMS_EPS) * g


def _dot(a, b):
    return jnp.dot(a, b, preferred_element_type=F32)


def _dot_nt(a, b, precision=None):
    return lax.dot_general(a, b, (((1,), (1,)), ((), ())),
                           preferred_element_type=F32, precision=precision)


def _dot_tn(a, b):
    return lax.dot_general(a, b, (((0,), (0,)), ((), ())), preferred_element_type=F32)


def _rope_body(pos_ref, inv_ref, sign_ref, cos_ref, sin_ref):
    ang = pos_ref[...].astype(F32) * inv_ref[...]
    cos_ref[...] = jnp.cos(ang)
    sin_ref[...] = jnp.sin(ang) * sign_ref[...]


def _rope_tables(pos_col):
    T = pos_col.shape[0]
    half = RET_HEAD_DIM // 2
    inv = ROPE_BASE ** (-jnp.arange(half, dtype=F32) / half)
    inv_full = jnp.concatenate([inv, inv]).reshape(1, RET_HEAD_DIM)
    sign = jnp.concatenate([-jnp.ones((half,), F32), jnp.ones((half,), F32)]).reshape(1, RET_HEAD_DIM)
    tile = min(ROPE_TILE, T)
    return pl.pallas_call(
        _rope_body,
        grid=(T // tile,),
        in_specs=[pl.BlockSpec((tile, 1), lambda i: (i, 0)),
                  pl.BlockSpec((1, RET_HEAD_DIM), lambda i: (0, 0)),
                  pl.BlockSpec((1, RET_HEAD_DIM), lambda i: (0, 0))],
        out_specs=[pl.BlockSpec((tile, RET_HEAD_DIM), lambda i: (i, 0)),
                   pl.BlockSpec((tile, RET_HEAD_DIM), lambda i: (i, 0))],
        out_shape=[jax.ShapeDtypeStruct((T, RET_HEAD_DIM), F32),
                   jax.ShapeDtypeStruct((T, RET_HEAD_DIM), F32)],
        name="rope_tables",
    )(pos_col, inv_full, sign)


def _retention_constants(tile):
    h = np.arange(RET_HEADS, dtype=np.float64)
    log_gamma = np.log1p(-np.exp2(-5.0 - h))
    idx = np.arange(tile, dtype=np.float64)
    diff = idx[:, None] - idx[None, :]
    dmat = np.where(diff >= 0, np.exp(log_gamma[:, None, None] * np.maximum(diff, 0.0)[None]), 0.0)
    xi = np.exp(log_gamma[:, None] * (idx + 1.0)[None])
    zeta = np.exp(log_gamma[:, None] * (tile - 1.0 - idx)[None])
    chunk_decay = np.exp(log_gamma * tile)
    xi_b = np.broadcast_to(xi[:, :, None], (RET_HEADS, tile, RET_HEAD_DIM))
    zeta_b = np.broadcast_to(zeta[:, :, None], (RET_HEADS, tile, RET_HEAD_DIM))
    return (jnp.asarray(dmat, F32), jnp.asarray(xi_b, F32), jnp.asarray(zeta_b, F32),
            tuple(float(c) for c in chunk_decay))


def _mix_body(x_ref, cos_ref, sin_ref, gmix_ref, win_ref, dmat_ref, xi_ref, zeta_ref, gret_ref,
              wpool_ref, pscale_ref, wbr_ref, wout_ref, gffn_ref, wr_ref, br_ref, tri_ref,
              x1_ref, h2_ref, idx_ref, w_ref, rank_ref, cnt_ref, trun_ref,
              state_ref, ue_ref, run_ref, *, tiles_per_seq, chunk_decay):
    tm, d_model = x_ref.shape
    ret_width = RET_HEADS * RET_HEAD_DIM
    i = pl.program_id(0)
    seq_tile = i % tiles_per_seq

    @pl.when(seq_tile == 0)
    def _():
        state_ref[...] = jnp.zeros(state_ref.shape, F32)
        ue_ref[0:POOL_HALO, :] = jnp.zeros((POOL_HALO, ue_ref.shape[1]), F32)

    @pl.when(i == 0)
    def _():
        run_ref[...] = jnp.zeros(run_ref.shape, F32)

    x = x_ref[...]
    hb = _rms(x, gmix_ref[...]).astype(BF16)

    qkvg = _dot(hb, win_ref[:, 0:4 * ret_width])
    cos = cos_ref[...]
    sin = sin_ref[...]

    def rot(a):
        return a * cos + pltpu.roll(a, RET_HEAD_DIM // 2, 1) * sin

    ys = []
    for h in range(RET_HEADS):
        lo = h * RET_HEAD_DIM
        q = rot(qkvg[:, lo:lo + RET_HEAD_DIM]).astype(BF16)
        k = (rot(qkvg[:, ret_width + lo:ret_width + lo + RET_HEAD_DIM])
             * (RET_HEAD_DIM ** -0.5)).astype(BF16)
        v = qkvg[:, 2 * ret_width + lo:2 * ret_width + lo + RET_HEAD_DIM]
        g = qkvg[:, 3 * ret_width + lo:3 * ret_width + lo + RET_HEAD_DIM]
        scores = _dot_nt(q, k) * dmat_ref[h]
        inner = _dot(scores.astype(BF16), v.astype(BF16))
        st = state_ref[h]
        cross = _dot(q, st.astype(BF16)) * xi_ref[h]
        state_ref[h] = st * chunk_decay[h] + _dot_tn(k, (v * zeta_ref[h]).astype(BF16))
        ret = inner + cross
        mu = jnp.mean(ret, axis=-1, keepdims=True)
        dev = ret - mu
        var = jnp.mean(dev * dev, axis=-1, keepdims=True)
        rn = dev * lax.rsqrt(var + GN_EPS) * gret_ref[:, lo:lo + RET_HEAD_DIM]
        ys.append(((g * jax.nn.sigmoid(g)) * rn).astype(BF16))
    y_ret = jnp.concatenate(ys, axis=1)
    branch_a = _dot(y_ret, wbr_ref[0])

    u = _dot(hb, win_ref[:, 4 * ret_width:4 * ret_width + d_model])
    ue_ref[POOL_HALO:POOL_HALO + tm, :] = u
    pos = seq_tile * tm + lax.broadcasted_iota(I32, (tm, 1), 0)
    group = d_model // len(POOL_WINDOWS)
    outs = []
    for gi, w in enumerate(POOL_WINDOWS):
        c0 = gi * group
        ug = u[:, c0:c0 + group]
        acc = ug
        for s in range(1, w):
            acc = acc + ue_ref[POOL_HALO - s:POOL_HALO - s + tm, c0:c0 + group]
        inv_count = 1.0 / jnp.minimum(pos + 1, w).astype(F32)
        mixed = (acc * inv_count - ug).astype(BF16)
        outs.append(_dot(mixed, wpool_ref[gi]))
    ue_ref[0:POOL_HALO, :] = ue_ref[tm:tm + POOL_HALO, :]
    y_pool = (jnp.concatenate(outs, axis=1) * pscale_ref[...]).astype(BF16)
    branch_b = _dot(y_pool, wbr_ref[1])

    gates = _dot(hb, win_ref[:, 4 * ret_width + d_model:4 * ret_width + 3 * d_model])
    merged = (jax.nn.sigmoid(gates[:, 0:d_model]) * branch_a
              + jax.nn.sigmoid(gates[:, d_model:2 * d_model]) * branch_b)
    x1 = x + _dot(merged.astype(BF16), wout_ref[...])
    x1_ref[...] = x1

    h2 = _rms(x1, gffn_ref[...])
    h2_ref[...] = h2
    n_exp = br_ref.shape[0]
    h2_hi = h2.astype(BF16)
    h2_lo = (h2 - h2_hi.astype(F32)).astype(BF16)
    both = _dot(h2_hi, wr_ref[...])
    lg = both[:, 0:LANES] + both[:, LANES:2 * LANES] + _dot(h2_lo, wr_ref[:, 0:LANES])
    logits = lg.T[0:n_exp, :] + br_ref[...]
    eiota = lax.broadcasted_iota(I32, (n_exp, tm), 0)
    vals, idxs = [], []
    l = logits
    for _ in range(TOP_K):
        m = jnp.max(l, axis=0, keepdims=True)
        sel = jnp.min(jnp.where(l == m, eiota, n_exp), axis=0, keepdims=True)
        vals.append(m)
        idxs.append(sel)
        l = jnp.where(eiota == sel, -jnp.inf, l)
    exps = [jnp.exp(v - vals[0]) for v in vals]
    denom = exps[0] + exps[1] + exps[2] + exps[3]
    inv_denom = 1.0 / denom
    onehot = jnp.zeros((n_exp, tm), F32)
    for kk in range(TOP_K):
        onehot = onehot + (eiota == idxs[kk]).astype(F32)
    base = _dot(onehot.astype(BF16), tri_ref[...]) + run_ref[:, 0:1]
    for kk in range(TOP_K):
        idx_ref[kk:kk + 1, :] = idxs[kk]
        w_ref[kk:kk + 1, :] = exps[kk] * inv_denom
        rank = jnp.sum(jnp.where(eiota == idxs[kk], base, 0.0), axis=0, keepdims=True)
        rank_ref[kk:kk + 1, :] = rank.astype(I32)
    trun_ref[...] = run_ref[...].astype(I32)
    run = run_ref[...] + jnp.sum(onehot, axis=1, keepdims=True)
    run_ref[...] = run
    cnt_ref[...] = run.astype(I32)


def _split_router(w_router):
    d, e = w_router.shape
    hi = w_router.astype(BF16)
    lo = (w_router - hi.astype(F32)).astype(BF16)
    pad = jnp.zeros((d, LANES - e), BF16)
    return jnp.concatenate([hi, pad, lo, pad], axis=1)


def _token_mix(x2d, cos, sin, g_mix, w_in, g_ret, w_pool, pool_scale, w_branch, w_out,
               g_ffn, w_router, b_router, seq_len):
    T, D = x2d.shape
    tm = min(MIX_TILE, seq_len)
    in_width = w_in.shape[1]
    ret_width = RET_HEADS * RET_HEAD_DIM
    dmat, xi_b, zeta_b, chunk_decay = _retention_constants(tm)
    tri = jnp.asarray(np.triu(np.ones((tm, tm), np.float32), 1), BF16)
    row = lambda a: a.reshape(1, -1)
    tile_spec = lambda w: pl.BlockSpec((tm, w), lambda i: (i, 0))
    top_spec = pl.BlockSpec((TOP_K, tm), lambda i: (0, i))
    body = functools.partial(_mix_body, tiles_per_seq=seq_len // tm, chunk_decay=chunk_decay)
    return pl.pallas_call(
        body,
        grid=(T // tm,),
        in_specs=[tile_spec(D), tile_spec(RET_HEAD_DIM), tile_spec(RET_HEAD_DIM),
                  _const_spec((1, D)), _const_spec((D, in_width)),
                  _const_spec(dmat.shape), _const_spec(xi_b.shape), _const_spec(zeta_b.shape),
                  _const_spec((1, ret_width)), _const_spec(w_pool.shape), _const_spec((1, D)),
                  _const_spec(w_branch.shape), _const_spec((D, D)), _const_spec((1, D)),
                  _const_spec((D, 2 * LANES)), _const_spec((N_EXPERTS, 1)), _const_spec((tm, tm))],
        out_specs=[tile_spec(D), tile_spec(D), top_spec, top_spec, top_spec,
                   pl.BlockSpec((N_EXPERTS, LANES), lambda i: (0, 0)),
                   pl.BlockSpec((None, N_EXPERTS, LANES), lambda i: (i, 0, 0))],
        out_shape=[jax.ShapeDtypeStruct((T, D), F32), jax.ShapeDtypeStruct((T, D), F32),
                   jax.ShapeDtypeStruct((TOP_K, T), I32), jax.ShapeDtypeStruct((TOP_K, T), F32),
                   jax.ShapeDtypeStruct((TOP_K, T), I32),
                   jax.ShapeDtypeStruct((N_EXPERTS, LANES), I32),
                   jax.ShapeDtypeStruct((T // tm, N_EXPERTS, LANES), I32)],
        scratch_shapes=[pltpu.VMEM((RET_HEADS, RET_HEAD_DIM, RET_HEAD_DIM), F32),
                        pltpu.VMEM((tm + POOL_HALO, D), F32),
                        pltpu.VMEM((N_EXPERTS, LANES), F32)],
        compiler_params=pltpu.CompilerParams(dimension_semantics=("arbitrary",),
                                             vmem_limit_bytes=VMEM_LIMIT_BYTES),
        name="token_mix",
    )(x2d, cos, sin, row(g_mix), w_in.astype(BF16), dmat, xi_b, zeta_b, row(g_ret),
      w_pool.astype(BF16), row(pool_scale), w_branch.astype(BF16), w_out.astype(BF16),
      row(g_ffn), _split_router(w_router), b_router.reshape(-1, 1), tri)


def _disp_body(zstart_ref, h2_ref, dest_ref, xs_ref, zbuf_ref, sem, zsem):
    td = h2_ref.shape[0]
    te = zbuf_ref.shape[0]

    def zero_copy(e):
        start = pl.multiple_of(jnp.maximum(zstart_ref[e], 0), te)
        return pltpu.make_async_copy(zbuf_ref, xs_ref.at[pl.ds(start, te)], zsem)

    @pl.when(pl.program_id(0) == 0)
    def _():
        zbuf_ref[...] = jnp.zeros(zbuf_ref.shape, F32)

        def start(e, c):
            @pl.when(zstart_ref[e] >= 0)
            def _():
                zero_copy(e).start()
            return c

        def wait(e, c):
            @pl.when(zstart_ref[e] >= 0)
            def _():
                zero_copy(e).wait()
            return c

        lax.fori_loop(0, zstart_ref.shape[0], start, 0)
        lax.fori_loop(0, zstart_ref.shape[0], wait, 0)

    def row_copy(t, dest):
        return pltpu.make_async_copy(h2_ref.at[pl.ds(t, 1)], xs_ref.at[pl.ds(dest, 1)], sem)

    def start(j, c):
        ts = [j * ROW_UNROLL + u for u in range(ROW_UNROLL)]
        dests = [[dest_ref[kk, t] for kk in range(TOP_K)] for t in ts]
        for t, dest in zip(ts, dests):
            for kk in range(TOP_K):
                row_copy(t, dest[kk]).start(priority=kk % 2)
        return c

    lax.fori_loop(0, td // ROW_UNROLL, start, 0)
    for kk in range(TOP_K):
        pltpu.make_async_copy(h2_ref, xs_ref.at[pl.ds(0, td)], sem).wait()


def _moe_dispatch(h2, dest_t, zstart, n_rows):
    T, D = h2.shape
    td = min(DISP_TILE, T)
    smem_spec = pl.BlockSpec((TOP_K, td), lambda i, *_: (0, i), memory_space=pltpu.SMEM)
    return pl.pallas_call(
        _disp_body,
        grid_spec=pltpu.PrefetchScalarGridSpec(
            num_scalar_prefetch=1,
            grid=(T // td,),
            in_specs=[pl.BlockSpec((td, D), lambda i, *_: (i, 0)), smem_spec],
            out_specs=pl.BlockSpec(memory_space=pl.ANY),
            scratch_shapes=[pltpu.VMEM((EXPERT_TILE, D), F32),
                            pltpu.SemaphoreType.DMA, pltpu.SemaphoreType.DMA]),
        out_shape=jax.ShapeDtypeStruct((n_rows, D), F32),
        compiler_params=pltpu.CompilerParams(dimension_semantics=("arbitrary",),
                                             vmem_limit_bytes=VMEM_LIMIT_BYTES),
        name="moe_dispatch",
    )(zstart, h2, dest_t)


def _expert_body(blk_e_ref, blk_src_ref, nused_ref, first_ref, next_e_ref, xs_ref, wgu_hbm, bgu_ref,
                 wd_hbm, bd_ref, yb_ref, wgu_f32_ref, wd_f32_ref, wgu_bf_ref, wd_bf_ref, wsem):
    i = pl.program_id(0)
    d_ff = wd_bf_ref.shape[0]

    def weight_copies(e):
        return (pltpu.make_async_copy(wgu_hbm.at[pl.ds(e, 1)], wgu_f32_ref, wsem.at[0]),
                pltpu.make_async_copy(wd_hbm.at[pl.ds(e, 1)], wd_f32_ref, wsem.at[1]))

    @pl.when(i == 0)
    def _():
        for cp in weight_copies(blk_e_ref[0]):
            cp.start()

    @pl.when(first_ref[i] != 0)
    def _():
        for cp in weight_copies(blk_e_ref[i]):
            cp.wait()
        wgu_bf_ref[...] = wgu_f32_ref[0].astype(BF16)
        wd_bf_ref[...] = wd_f32_ref[0].astype(BF16)

    @pl.when((first_ref[i] != 0) & (next_e_ref[i] >= 0))
    def _():
        for cp in weight_copies(next_e_ref[i]):
            cp.start()

    @pl.when(i < nused_ref[0])
    def _():
        gu = _dot(xs_ref[...].astype(BF16), wgu_bf_ref[...]) + bgu_ref[...]
        gate = jnp.minimum(gu[:, 0:d_ff], SWIGLU_LIMIT)
        up = jnp.clip(gu[:, d_ff:2 * d_ff], -SWIGLU_LIMIT, SWIGLU_LIMIT)
        act = (up + 1.0) * (gate * jax.nn.sigmoid(gate * SWIGLU_ALPHA))
        yb_ref[...] = _dot(act.astype(BF16), wd_bf_ref[...]) + bd_ref[...]

    @pl.when(i >= nused_ref[0])
    def _():
        yb_ref[...] = jnp.zeros(yb_ref.shape, F32)


def _moe_experts(xs, blk_e, blk_src, nused, first, next_e, w_gate_up, b_gate_up, w_down, b_down):
    n_rows, D = xs.shape
    E, _, two_ff = w_gate_up.shape
    d_ff = two_ff // 2
    te = EXPERT_TILE
    return pl.pallas_call(
        _expert_body,
        grid_spec=pltpu.PrefetchScalarGridSpec(
            num_scalar_prefetch=5,
            grid=(n_rows // te,),
            in_specs=[pl.BlockSpec((te, D), lambda i, be, bs, *_: (bs[i], 0)),
                      pl.BlockSpec(memory_space=pl.ANY),
                      pl.BlockSpec((None, 1, two_ff), lambda i, be, *_: (be[i], 0, 0)),
                      pl.BlockSpec(memory_space=pl.ANY),
                      pl.BlockSpec((None, 1, D), lambda i, be, *_: (be[i], 0, 0))],
            out_specs=pl.BlockSpec((te, D), lambda i, *_: (i, 0)),
            scratch_shapes=[pltpu.VMEM((1, D, two_ff), F32), pltpu.VMEM((1, d_ff, D), F32),
                            pltpu.VMEM((D, two_ff), BF16), pltpu.VMEM((d_ff, D), BF16),
                            pltpu.SemaphoreType.DMA((2,))]),
        out_shape=jax.ShapeDtypeStruct((n_rows, D), F32),
        compiler_params=pltpu.CompilerParams(dimension_semantics=("arbitrary",),
                                             vmem_limit_bytes=VMEM_LIMIT_BYTES),
        name="moe_experts",
    )(blk_e, blk_src, nused, first, next_e, xs, w_gate_up, b_gate_up.reshape(E, 1, two_ff),
      w_down, b_down.reshape(E, 1, D))


def _comb_body(ws_ref, ok_ref, dest_ref, destn_ref, yb_ref, pos_ref, w_ref, x1_ref, p_ref,
               gple_ref, wpg_ref, wpp_ref, gfin_ref, out_ref, gbuf_ref, sems):
    tc = x1_ref.shape[0]
    n_rows = gbuf_ref.shape[1]
    i = pl.program_id(0)
    slot = i % 2

    def window_copy(tile, e, s):
        src = pl.multiple_of(ws_ref[tile * N_EXPERTS + e], SUBLANES)
        return pltpu.make_async_copy(yb_ref.at[pl.ds(src, COMB_WINDOW)],
                                     gbuf_ref.at[s, pl.ds(e * COMB_WINDOW, COMB_WINDOW)], sems.at[s])

    def issue(tile, dest_r, s):
        @pl.when(ok_ref[tile] != 0)
        def _():
            for e in range(N_EXPERTS):
                window_copy(tile, e, s).start(priority=e % 2)

        @pl.when(ok_ref[tile] == 0)
        def _():
            def start(j, c):
                ts = [j * ROW_UNROLL + u for u in range(ROW_UNROLL)]
                srcs = [[dest_r[kk, t] for kk in range(TOP_K)] for t in ts]
                for t, src in zip(ts, srcs):
                    for kk in range(TOP_K):
                        pltpu.make_async_copy(yb_ref.at[pl.ds(src[kk], 1)],
                                              gbuf_ref.at[s, pl.ds(kk * tc + t, 1)],
                                              sems.at[s]).start(priority=kk % 2)
                return c

            lax.fori_loop(0, tc // ROW_UNROLL, start, 0)

    @pl.when(i == 0)
    def _():
        issue(0, dest_ref, 0)

    @pl.when(i + 1 < pl.num_programs(0))
    def _():
        issue(i + 1, destn_ref, 1 - slot)

    def to_cols(rows):
        pad = jnp.zeros((LANES - rows.shape[0], tc), F32)
        return jnp.concatenate([rows, pad], axis=0).T

    w_col = to_cols(w_ref[...])
    fast = ok_ref[i] != 0
    slow_pos = (lax.broadcasted_iota(I32, (TOP_K, tc), 0) * tc
                + lax.broadcasted_iota(I32, (TOP_K, tc), 1))
    pos_col = to_cols(jnp.where(fast, pos_ref[...], slow_pos).astype(F32))
    pp = _dot(p_ref[...].astype(BF16), wpp_ref[...])

    @pl.when(fast)
    def _():
        pltpu.make_async_copy(yb_ref.at[pl.ds(0, n_rows)], gbuf_ref.at[slot], sems.at[slot]).wait()

    @pl.when(jnp.logical_not(fast))
    def _():
        pltpu.make_async_copy(yb_ref.at[pl.ds(0, TOP_K * tc)], gbuf_ref.at[slot, pl.ds(0, TOP_K * tc)],
                              sems.at[slot]).wait()
        gbuf_ref[slot, pl.ds(TOP_K * tc, n_rows - TOP_K * tc), :] = jnp.zeros(
            (n_rows - TOP_K * tc, gbuf_ref.shape[2]), F32)

    ciota = lax.broadcasted_iota(I32, (tc, n_rows), 1).astype(F32)
    sel = jnp.zeros((tc, n_rows), F32)
    for kk in range(TOP_K):
        sel = sel + jnp.where(ciota == pos_col[:, kk:kk + 1], w_col[:, kk:kk + 1], 0.0)
    moe = _dot(sel.astype(BF16), gbuf_ref[slot].astype(BF16))
    x2 = x1_ref[...] + moe
    h3 = _rms(x2, gple_ref[...]).astype(BF16)
    gate = jax.nn.sigmoid(_dot(h3, wpg_ref[...]))
    x3 = x2 + gate * pp
    out_ref[...] = _rms(x3, gfin_ref[...])


def _moe_combine(ws, ok, dest_t, pos_t, yb, w_t, x1, p2d, g_ple, w_ple_gate, w_ple_proj, g_final):
    T, D = x1.shape
    tc = min(COMB_TILE, T)
    n_tiles = T // tc
    ple = p2d.shape[1]
    smem_spec = pl.BlockSpec((TOP_K, tc), lambda i, *_: (0, i), memory_space=pltpu.SMEM)
    smem_next = pl.BlockSpec((TOP_K, tc), lambda i, *_: (0, jnp.minimum(i + 1, n_tiles - 1)),
                             memory_space=pltpu.SMEM)
    top_spec = pl.BlockSpec((TOP_K, tc), lambda i, *_: (0, i))
    const = lambda shape: pl.BlockSpec(shape, lambda i, *_: (0,) * len(shape),
                                       pipeline_mode=pl.Buffered(1))
    row = lambda a: a.reshape(1, -1)
    return pl.pallas_call(
        _comb_body,
        grid_spec=pltpu.PrefetchScalarGridSpec(
            num_scalar_prefetch=2,
            grid=(n_tiles,),
            in_specs=[smem_spec, smem_next,
                      pl.BlockSpec(memory_space=pl.ANY),
                      top_spec, top_spec,
                      pl.BlockSpec((tc, D), lambda i, *_: (i, 0)),
                      pl.BlockSpec((tc, ple), lambda i, *_: (i, 0)),
                      const((1, D)), const((D, D)), const((ple, D)), const((1, D))],
            out_specs=pl.BlockSpec((tc, D), lambda i, *_: (i, 0)),
            scratch_shapes=[pltpu.VMEM((2, N_EXPERTS * COMB_WINDOW, D), F32),
                            pltpu.SemaphoreType.DMA((2,))]),
        out_shape=jax.ShapeDtypeStruct((T, D), F32),
        compiler_params=pltpu.CompilerParams(dimension_semantics=("arbitrary",),
                                             vmem_limit_bytes=VMEM_LIMIT_BYTES),
        name="moe_combine",
    )(ws, ok, dest_t, dest_t, yb, pos_t, w_t, x1, p2d, row(g_ple), w_ple_gate.astype(BF16),
      w_ple_proj.astype(BF16), row(g_final))


def _combine_windows(pstart, trun, counts, idx_t, dest_t, tile, n_rows):
    n_tiles = trun.shape[0]
    tcnt = jnp.concatenate([trun[1:], counts[None, :]], axis=0) - trun
    start = pstart[None, :] + trun
    ws = jnp.minimum(start // SUBLANES * SUBLANES, n_rows - COMB_WINDOW).astype(I32)
    ok = jnp.all(start + tcnt - ws <= COMB_WINDOW, axis=1).astype(I32)
    ws_tok = _select_expert(jnp.repeat(ws, tile, axis=0)[None], idx_t)
    pos_t = idx_t * COMB_WINDOW + dest_t - ws_tok
    return ws.reshape(-1), ok, pos_t.astype(I32)


def _select_expert(table, idx_t):
    onehot = idx_t[:, :, None] == jnp.arange(N_EXPERTS, dtype=I32)
    return jnp.sum(jnp.where(onehot, table, 0), axis=-1)


def _group_layout(counts, n_blocks):
    te = EXPERT_TILE
    padded = (counts + te - 1) // te * te
    pends = jnp.cumsum(padded)
    pstart = (pends - padded).astype(I32)
    nused = (pends[-1] // te).astype(I32)
    tail = nused + jnp.arange(N_EXPERTS, dtype=I32)
    zstart = jnp.concatenate([jnp.where(padded > 0, pends - te, -1),
                              jnp.where(tail < n_blocks, tail * te, -1)]).astype(I32)
    blk_src = jnp.minimum(jnp.arange(n_blocks, dtype=I32), jnp.maximum(nused - 1, 0))
    blk_e = jnp.sum((blk_src * te)[:, None] >= pends[None, :], axis=1)
    blk_e = jnp.clip(blk_e, 0, N_EXPERTS - 1).astype(I32)
    blk = jnp.arange(n_blocks, dtype=I32)
    prev_e = jnp.concatenate([blk_e[:1], blk_e[:-1]])
    first = ((blk < nused) & ((blk == 0) | (blk_e != prev_e))).astype(I32)
    experts = jnp.arange(N_EXPERTS, dtype=I32)[None, :]
    later = (experts > blk_e[:, None]) & (counts[None, :] > 0)
    next_e = jnp.min(jnp.where(later, experts, N_EXPERTS), axis=1)
    next_e = jnp.where(next_e < N_EXPERTS, next_e, -1).astype(I32)
    return pstart, zstart, blk_e, blk_src, nused.reshape(1), first, next_e


def kernel(x, p, positions, g_mix_norm, w_in, g_ret_norm, w_pool, pool_scale, w_branch, w_out,
           g_ffn_norm, w_router, b_router, w_gate_up, b_gate_up, w_down, b_down,
           g_ple_norm, w_ple_gate, w_ple_proj, g_final):
    B, S, D = x.shape
    depth = w_in.shape[0]
    T = B * S
    xt = x.reshape(T, D)
    cos, sin = _rope_tables(positions.reshape(T, 1))
    n_blocks = (T * TOP_K) // EXPERT_TILE + N_EXPERTS
    for i in range(depth):
        x1, h2, idx_t, w_t, rank_t, cnt, trun = _token_mix(
            xt, cos, sin, g_mix_norm[i], w_in[i], g_ret_norm[i], w_pool[i], pool_scale[i],
            w_branch[i], w_out[i], g_ffn_norm[i], w_router[i], b_router[i], S)
        pstart, zstart, blk_e, blk_src, nused, first, next_e = _group_layout(cnt[:, 0], n_blocks)
        dest_t = _select_expert(pstart[None, None, :], idx_t) + rank_t
        xs = _moe_dispatch(h2, dest_t, zstart, n_blocks * EXPERT_TILE)
        yb = _moe_experts(xs, blk_e, blk_src, nused, first, next_e,
                          w_gate_up[i], b_gate_up[i], w_down[i], b_down[i])
        assert depth == 1
        assert min(MIX_TILE, S) == min(COMB_TILE, T)
        ws, ok, pos_t = _combine_windows(pstart, trun[:, :, 0], cnt[:, 0], idx_t, dest_t,
                                         min(COMB_TILE, T), n_blocks * EXPERT_TILE)
        xt = _moe_combine(ws, ok, dest_t, pos_t, yb, w_t, x1, p[i].reshape(T, -1), g_ple_norm[i],
                          w_ple_gate[i], w_ple_proj[i], g_final)
    return xt.reshape(B, S, D)
```

```python
import functools

import numpy as np
import jax
import jax.numpy as jnp
from jax import lax
from jax.experimental import pallas as pl
from jax.experimental.pallas import tpu as pltpu

F32 = jnp.float32
BF16 = jnp.bfloat16
I32 = jnp.int32

RET_HEADS = 8
RET_HEAD_DIM = 128
ROPE_BASE = 10000.0
GN_EPS = 1e-5
RMS_EPS = 1e-6
POOL_WINDOWS = (2, 4, 8, 16)
N_EXPERTS = 32
TOP_K = 4
SWIGLU_ALPHA = 1.702
SWIGLU_LIMIT = 7.0

LANES = 128
SUBLANES = 8
VMEM_LIMIT_BYTES = 56 * 1024 * 1024

MIX_TILE = 256
ROPE_TILE = 1024
DISP_TILE = 512
EXPERT_TILE = 256
COMB_TILE = 256
POOL_HALO = 16
POOL_PAD = 8
COMB_WINDOW = 64
ROW_UNROLL = 8


def _const_spec(shape):
    nd = len(shape)
    return pl.BlockSpec(shape, lambda *_: (0,) * nd, pipeline_mode=pl.Buffered(1))


def _rms(x, g):
    return x * lax.rsqrt(jnp.mean(x * x, axis=-1, keepdims=True) + RMS_EPS) * g


def _dot(a, b):
    return jnp.dot(a, b, preferred_element_type=F32)


def _dot_nt(a, b, precision=None):
    return lax.dot_general(a, b, (((1,), (1,)), ((), ())),
                           preferred_element_type=F32, precision=precision)


def _dot_tn(a, b):
    return lax.dot_general(a, b, (((0,), (0,)), ((), ())), preferred_element_type=F32)


def _rope_body(pos_ref, inv_ref, sign_ref, cos_ref, sin_ref):
    ang = pos_ref[...].astype(F32) * inv_ref[...]
    cos_ref[...] = jnp.cos(ang)
    sin_ref[...] = jnp.sin(ang) * sign_ref[...]


def _rope_tables(pos_col):
    T = pos_col.shape[0]
    half = RET_HEAD_DIM // 2
    inv = ROPE_BASE ** (-jnp.arange(half, dtype=F32) / half)
    inv_full = jnp.concatenate([inv, inv]).reshape(1, RET_HEAD_DIM)
    sign = jnp.concatenate([-jnp.ones((half,), F32), jnp.ones((half,), F32)]).reshape(1, RET_HEAD_DIM)
    tile = min(ROPE_TILE, T)
    return pl.pallas_call(
        _rope_body,
        grid=(T // tile,),
        in_specs=[pl.BlockSpec((tile, 1), lambda i: (i, 0)),
                  pl.BlockSpec((1, RET_HEAD_DIM), lambda i: (0, 0)),
                  pl.BlockSpec((1, RET_HEAD_DIM), lambda i: (0, 0))],
        out_specs=[pl.BlockSpec((tile, RET_HEAD_DIM), lambda i: (i, 0)),
                   pl.BlockSpec((tile, RET_HEAD_DIM), lambda i: (i, 0))],
        out_shape=[jax.ShapeDtypeStruct((T, RET_HEAD_DIM), F32),
                   jax.ShapeDtypeStruct((T, RET_HEAD_DIM), F32)],
        name="rope_tables",
    )(pos_col, inv_full, sign)


def _retention_constants(tile):
    h = np.arange(RET_HEADS, dtype=np.float64)
    log_gamma = np.log1p(-np.exp2(-5.0 - h))
    idx = np.arange(tile, dtype=np.float64)
    diff = idx[:, None] - idx[None, :]
    dmat = np.where(diff >= 0, np.exp(log_gamma[:, None, None] * np.maximum(diff, 0.0)[None]), 0.0)
    xi = np.exp(log_gamma[:, None] * (idx + 1.0)[None])
    zeta = np.exp(log_gamma[:, None] * (tile - 1.0 - idx)[None])
    chunk_decay = np.exp(log_gamma * tile)
    xi_b = np.broadcast_to(xi[:, :, None], (RET_HEADS, tile, RET_HEAD_DIM))
    zeta_b = np.broadcast_to(zeta[:, :, None], (RET_HEADS, tile, RET_HEAD_DIM))
    return (jnp.asarray(dmat, F32), jnp.asarray(xi_b, F32), jnp.asarray(zeta_b, F32),
            tuple(float(c) for c in chunk_decay))


def _mix_body(x_ref, cos_ref, sin_ref, gmix_ref, win_ref, dmat_ref, xi_ref, zeta_ref, gret_ref,
              wpool_ref, pscale_ref, wbr_ref, wout_ref, gffn_ref, wr_ref, br_ref, tri_ref,
              x1_ref, h2_ref, idx_ref, w_ref, rank_ref, cnt_ref, trun_ref,
              state_ref, ue_ref, lv_ref, run_ref, *, tiles_per_seq, chunk_decay):
    tm, d_model = x_ref.shape
    ret_width = RET_HEADS * RET_HEAD_DIM
    i = pl.program_id(0)
    seq_tile = i % tiles_per_seq

    @pl.when(seq_tile == 0)
    def _():
        state_ref[...] = jnp.zeros(state_ref.shape, F32)
        ue_ref[0:POOL_PAD + POOL_HALO, :] = jnp.zeros((POOL_PAD + POOL_HALO, ue_ref.shape[1]), F32)

    @pl.when(i == 0)
    def _():
        run_ref[...] = jnp.zeros(run_ref.shape, F32)
        lv_ref[:, 0:POOL_PAD, :] = jnp.zeros((lv_ref.shape[0], POOL_PAD, lv_ref.shape[2]), F32)

    x = x_ref[...]
    hb = _rms(x, gmix_ref[...]).astype(BF16)

    qkvg = _dot(hb, win_ref[:, 0:4 * ret_width])
    cos = cos_ref[...]
    sin = sin_ref[...]

    def rot(a):
        return a * cos + pltpu.roll(a, RET_HEAD_DIM // 2, 1) * sin

    ys = []
    for h in range(RET_HEADS):
        lo = h * RET_HEAD_DIM
        q = rot(qkvg[:, lo:lo + RET_HEAD_DIM]).astype(BF16)
        k = (rot(qkvg[:, ret_width + lo:ret_width + lo + RET_HEAD_DIM])
             * (RET_HEAD_DIM ** -0.5)).astype(BF16)
        v = qkvg[:, 2 * ret_width + lo:2 * ret_width + lo + RET_HEAD_DIM]
        g = qkvg[:, 3 * ret_width + lo:3 * ret_width + lo + RET_HEAD_DIM]
        scores = _dot_nt(q, k) * dmat_ref[h]
        inner = _dot(scores.astype(BF16), v.astype(BF16))
        st = state_ref[h]
        cross = _dot(q, st.astype(BF16)) * xi_ref[h]
        state_ref[h] = st * chunk_decay[h] + _dot_tn(k, (v * zeta_ref[h]).astype(BF16))
        ret = inner + cross
        mu = jnp.mean(ret, axis=-1, keepdims=True)
        dev = ret - mu
        var = jnp.mean(dev * dev, axis=-1, keepdims=True)
        rn = dev * lax.rsqrt(var + GN_EPS) * gret_ref[:, lo:lo + RET_HEAD_DIM]
        ys.append(((g * jax.nn.sigmoid(g)) * rn).astype(BF16))
    y_ret = jnp.concatenate(ys, axis=1)
    branch_a = _dot(y_ret, wbr_ref[0])

    u = _dot(hb, win_ref[:, 4 * ret_width:4 * ret_width + d_model])
    top = POOL_PAD + POOL_HALO
    ext = tm + POOL_HALO
    group = d_model // len(POOL_WINDOWS)
    ue_ref[top:top + tm, :] = u
    s2 = ue_ref[POOL_PAD:POOL_PAD + ext, :] + ue_ref[POOL_PAD - 1:POOL_PAD - 1 + ext, :]
    lv_ref[0, POOL_PAD:POOL_PAD + ext, :] = s2
    s4 = s2[:, group:] + lv_ref[0, POOL_PAD - 2:POOL_PAD - 2 + ext, group:]
    lv_ref[1, POOL_PAD:POOL_PAD + ext, group:] = s4
    s8 = s4[:, group:] + lv_ref[1, POOL_PAD - 4:POOL_PAD - 4 + ext, 2 * group:]
    lv_ref[2, POOL_PAD:POOL_PAD + ext, 2 * group:] = s8
    sums = [lv_ref[0, top:top + tm, 0:group],
            lv_ref[1, top:top + tm, group:2 * group],
            lv_ref[2, top:top + tm, 2 * group:3 * group],
            lv_ref[2, top:top + tm, 3 * group:] + lv_ref[2, top - 8:top - 8 + tm, 3 * group:]]
    ue_ref[POOL_PAD:top, :] = ue_ref[tm + POOL_PAD:tm + top, :]
    pos = seq_tile * tm + lax.broadcasted_iota(I32, (tm, 1), 0)
    outs = []
    for gi, w in enumerate(POOL_WINDOWS):
        ug = u[:, gi * group:(gi + 1) * group]
        inv_count = 1.0 / jnp.minimum(pos + 1, w).astype(F32)
        mixed = (sums[gi] * inv_count - ug).astype(BF16)
        outs.append(_dot(mixed, wpool_ref[gi]))
    y_pool = (jnp.concatenate(outs, axis=1) * pscale_ref[...]).astype(BF16)
    branch_b = _dot(y_pool, wbr_ref[1])

    gates = _dot(hb, win_ref[:, 4 * ret_width + d_model:4 * ret_width + 3 * d_model])
    merged = (jax.nn.sigmoid(gates[:, 0:d_model]) * branch_a
              + jax.nn.sigmoid(gates[:, d_model:2 * d_model]) * branch_b)
    x1 = x + _dot(merged.astype(BF16), wout_ref[...])
    x1_ref[...] = x1

    h2 = _rms(x1, gffn_ref[...])
    h2_ref[...] = h2
    n_exp = br_ref.shape[0]
    h2_hi = h2.astype(BF16)
    h2_lo = (h2 - h2_hi.astype(F32)).astype(BF16)
    both = _dot(h2_hi, wr_ref[...])
    lg = both[:, 0:LANES] + both[:, LANES:2 * LANES] + _dot(h2_lo, wr_ref[:, 0:LANES])
    logits = lg.T[0:n_exp, :] + br_ref[...]
    eiota = lax.broadcasted_iota(I32, (n_exp, tm), 0)
    vals, idxs = [], []
    l = logits
    for _ in range(TOP_K):
        m = jnp.max(l, axis=0, keepdims=True)
        sel = jnp.min(jnp.where(l == m, eiota, n_exp), axis=0, keepdims=True)
        vals.append(m)
        idxs.append(sel)
        l = jnp.where(eiota == sel, -jnp.inf, l)
    exps = [jnp.exp(v - vals[0]) for v in vals]
    denom = exps[0] + exps[1] + exps[2] + exps[3]
    inv_denom = 1.0 / denom
    onehot = jnp.zeros((n_exp, tm), F32)
    for kk in range(TOP_K):
        onehot = onehot + (eiota == idxs[kk]).astype(F32)
    base = _dot(onehot.astype(BF16), tri_ref[...]) + run_ref[:, 0:1]
    for kk in range(TOP_K):
        idx_ref[kk:kk + 1, :] = idxs[kk]
        w_ref[kk:kk + 1, :] = exps[kk] * inv_denom
        rank = jnp.sum(jnp.where(eiota == idxs[kk], base, 0.0), axis=0, keepdims=True)
        rank_ref[kk:kk + 1, :] = rank.astype(I32)
    trun_ref[...] = run_ref[...].astype(I32)
    run = run_ref[...] + jnp.sum(onehot, axis=1, keepdims=True)
    run_ref[...] = run
    cnt_ref[...] = run.astype(I32)


def _split_router(w_router):
    d, e = w_router.shape
    hi = w_router.astype(BF16)
    lo = (w_router - hi.astype(F32)).astype(BF16)
    pad = jnp.zeros((d, LANES - e), BF16)
    return jnp.concatenate([hi, pad, lo, pad], axis=1)


def _token_mix(x2d, cos, sin, g_mix, w_in, g_ret, w_pool, pool_scale, w_branch, w_out,
               g_ffn, w_router, b_router, seq_len):
    T, D = x2d.shape
    tm = min(MIX_TILE, seq_len)
    in_width = w_in.shape[1]
    ret_width = RET_HEADS * RET_HEAD_DIM
    assert POOL_WINDOWS == (2, 4, 8, 16)
    dmat, xi_b, zeta_b, chunk_decay = _retention_constants(tm)
    tri = jnp.asarray(np.triu(np.ones((tm, tm), np.float32), 1), BF16)
    row = lambda a: a.reshape(1, -1)
    tile_spec = lambda w: pl.BlockSpec((tm, w), lambda i: (i, 0))
    top_spec = pl.BlockSpec((TOP_K, tm), lambda i: (0, i))
    body = functools.partial(_mix_body, tiles_per_seq=seq_len // tm, chunk_decay=chunk_decay)
    return pl.pallas_call(
        body,
        grid=(T // tm,),
        in_specs=[tile_spec(D), tile_spec(RET_HEAD_DIM), tile_spec(RET_HEAD_DIM),
                  _const_spec((1, D)), _const_spec((D, in_width)),
                  _const_spec(dmat.shape), _const_spec(xi_b.shape), _const_spec(zeta_b.shape),
                  _const_spec((1, ret_width)), _const_spec(w_pool.shape), _const_spec((1, D)),
                  _const_spec(w_branch.shape), _const_spec((D, D)), _const_spec((1, D)),
                  _const_spec((D, 2 * LANES)), _const_spec((N_EXPERTS, 1)), _const_spec((tm, tm))],
        out_specs=[tile_spec(D), tile_spec(D), top_spec, top_spec, top_spec,
                   pl.BlockSpec((N_EXPERTS, LANES), lambda i: (0, 0)),
                   pl.BlockSpec((None, N_EXPERTS, LANES), lambda i: (i, 0, 0))],
        out_shape=[jax.ShapeDtypeStruct((T, D), F32), jax.ShapeDtypeStruct((T, D), F32),
                   jax.ShapeDtypeStruct((TOP_K, T), I32), jax.ShapeDtypeStruct((TOP_K, T), F32),
                   jax.ShapeDtypeStruct((TOP_K, T), I32),
                   jax.ShapeDtypeStruct((N_EXPERTS, LANES), I32),
                   jax.ShapeDtypeStruct((T // tm, N_EXPERTS, LANES), I32)],
        scratch_shapes=[pltpu.VMEM((RET_HEADS, RET_HEAD_DIM, RET_HEAD_DIM), F32),
                        pltpu.VMEM((tm + POOL_PAD + POOL_HALO, D), F32),
                        pltpu.VMEM((3, tm + POOL_PAD + POOL_HALO, D), F32),
                        pltpu.VMEM((N_EXPERTS, LANES), F32)],
        compiler_params=pltpu.CompilerParams(dimension_semantics=("arbitrary",),
                                             vmem_limit_bytes=VMEM_LIMIT_BYTES),
        name="token_mix",
    )(x2d, cos, sin, row(g_mix), w_in.astype(BF16), dmat, xi_b, zeta_b, row(g_ret),
      w_pool.astype(BF16), row(pool_scale), w_branch.astype(BF16), w_out.astype(BF16),
      row(g_ffn), _split_router(w_router), b_router.reshape(-1, 1), tri)


def _disp_body(zstart_ref, h2_ref, dest_ref, xs_ref, zbuf_ref, sem, zsem):
    td = h2_ref.shape[0]
    te = zbuf_ref.shape[0]

    def zero_copy(e):
        start = pl.multiple_of(jnp.maximum(zstart_ref[e], 0), te)
        return pltpu.make_async_copy(zbuf_ref, xs_ref.at[pl.ds(start, te)], zsem)

    @pl.when(pl.program_id(0) == 0)
    def _():
        zbuf_ref[...] = jnp.zeros(zbuf_ref.shape, F32)

        def start(e, c):
            @pl.when(zstart_ref[e] >= 0)
            def _():
                zero_copy(e).start()
            return c

        def wait(e, c):
            @pl.when(zstart_ref[e] >= 0)
            def _():
                zero_copy(e).wait()
            return c

        lax.fori_loop(0, zstart_ref.shape[0], start, 0)
        lax.fori_loop(0, zstart_ref.shape[0], wait, 0)

    def row_copy(t, dest):
        return pltpu.make_async_copy(h2_ref.at[pl.ds(t, 1)], xs_ref.at[pl.ds(dest, 1)], sem)

    def start(j, c):
        ts = [j * ROW_UNROLL + u for u in range(ROW_UNROLL)]
        dests = [[dest_ref[t * TOP_K + kk] for kk in range(TOP_K)] for t in ts]
        for t, dest in zip(ts, dests):
            for kk in range(TOP_K):
                row_copy(t, dest[kk]).start(priority=kk % 2)
        return c

    lax.fori_loop(0, td // ROW_UNROLL, start, 0)
    for kk in range(TOP_K):
        pltpu.make_async_copy(h2_ref, xs_ref.at[pl.ds(0, td)], sem).wait()


def _moe_dispatch(h2, dest_t, zstart, n_rows):
    T, D = h2.shape
    td = min(DISP_TILE, T)
    smem_spec = pl.BlockSpec((TOP_K * td,), lambda i, *_: (i,), memory_space=pltpu.SMEM)
    return pl.pallas_call(
        _disp_body,
        grid_spec=pltpu.PrefetchScalarGridSpec(
            num_scalar_prefetch=1,
            grid=(T // td,),
            in_specs=[pl.BlockSpec((td, D), lambda i, *_: (i, 0)), smem_spec],
            out_specs=pl.BlockSpec(memory_space=pl.ANY),
            scratch_shapes=[pltpu.VMEM((EXPERT_TILE, D), F32),
                            pltpu.SemaphoreType.DMA, pltpu.SemaphoreType.DMA]),
        out_shape=jax.ShapeDtypeStruct((n_rows, D), F32),
        compiler_params=pltpu.CompilerParams(dimension_semantics=("arbitrary",),
                                             vmem_limit_bytes=VMEM_LIMIT_BYTES),
        name="moe_dispatch",
    )(zstart, h2, dest_t.T.reshape(-1))


def _expert_body(blk_e_ref, blk_src_ref, nused_ref, first_ref, next_e_ref, xs_ref, wgu_hbm, bgu_ref,
                 wd_hbm, bd_ref, yb_ref, wgu_f32_ref, wd_f32_ref, wgu_bf_ref, wd_bf_ref, wsem):
    i = pl.program_id(0)
    d_ff = wd_bf_ref.shape[0]

    def weight_copies(e):
        return (pltpu.make_async_copy(wgu_hbm.at[pl.ds(e, 1)], wgu_f32_ref, wsem.at[0]),
                pltpu.make_async_copy(wd_hbm.at[pl.ds(e, 1)], wd_f32_ref, wsem.at[1]))

    @pl.when(i == 0)
    def _():
        for cp in weight_copies(blk_e_ref[0]):
            cp.start()

    @pl.when(first_ref[i] != 0)
    def _():
        for cp in weight_copies(blk_e_ref[i]):
            cp.wait()
        wgu_bf_ref[...] = wgu_f32_ref[0].astype(BF16)
        wd_bf_ref[...] = wd_f32_ref[0].astype(BF16)

    @pl.when((first_ref[i] != 0) & (next_e_ref[i] >= 0))
    def _():
        for cp in weight_copies(next_e_ref[i]):
            cp.start()

    @pl.when(i < nused_ref[0])
    def _():
        gu = _dot(xs_ref[...].astype(BF16), wgu_bf_ref[...]) + bgu_ref[...]
        gate = jnp.minimum(gu[:, 0:d_ff], SWIGLU_LIMIT)
        up = jnp.clip(gu[:, d_ff:2 * d_ff], -SWIGLU_LIMIT, SWIGLU_LIMIT)
        act = (up + 1.0) * (gate * jax.nn.sigmoid(gate * SWIGLU_ALPHA))
        yb_ref[...] = _dot(act.astype(BF16), wd_bf_ref[...]) + bd_ref[...]

    @pl.when(i >= nused_ref[0])
    def _():
        yb_ref[...] = jnp.zeros(yb_ref.shape, F32)


def _moe_experts(xs, blk_e, blk_src, nused, first, next_e, w_gate_up, b_gate_up, w_down, b_down):
    n_rows, D = xs.shape
    E, _, two_ff = w_gate_up.shape
    d_ff = two_ff // 2
    te = EXPERT_TILE
    return pl.pallas_call(
        _expert_body,
        grid_spec=pltpu.PrefetchScalarGridSpec(
            num_scalar_prefetch=5,
            grid=(n_rows // te,),
            in_specs=[pl.BlockSpec((te, D), lambda i, be, bs, *_: (bs[i], 0)),
                      pl.BlockSpec(memory_space=pl.ANY),
                      pl.BlockSpec((None, 1, two_ff), lambda i, be, *_: (be[i], 0, 0)),
                      pl.BlockSpec(memory_space=pl.ANY),
                      pl.BlockSpec((None, 1, D), lambda i, be, *_: (be[i], 0, 0))],
            out_specs=pl.BlockSpec((te, D), lambda i, *_: (i, 0)),
            scratch_shapes=[pltpu.VMEM((1, D, two_ff), F32), pltpu.VMEM((1, d_ff, D), F32),
                            pltpu.VMEM((D, two_ff), BF16), pltpu.VMEM((d_ff, D), BF16),
                            pltpu.SemaphoreType.DMA((2,))]),
        out_shape=jax.ShapeDtypeStruct((n_rows, D), F32),
        compiler_params=pltpu.CompilerParams(dimension_semantics=("arbitrary",),
                                             vmem_limit_bytes=VMEM_LIMIT_BYTES),
        name="moe_experts",
    )(blk_e, blk_src, nused, first, next_e, xs, w_gate_up, b_gate_up.reshape(E, 1, two_ff),
      w_down, b_down.reshape(E, 1, D))


def _comb_body(ws_ref, ok_ref, dest_ref, destn_ref, yb_ref, pos_ref, w_ref, x1_ref, p_ref,
               gple_ref, wpg_ref, wpp_ref, gfin_ref, out_ref, gbuf_ref, sems):
    tc = x1_ref.shape[0]
    n_rows = gbuf_ref.shape[1]
    i = pl.program_id(0)
    slot = i % 2

    def window_copy(tile, e, s):
        src = pl.multiple_of(ws_ref[tile * N_EXPERTS + e], SUBLANES)
        return pltpu.make_async_copy(yb_ref.at[pl.ds(src, COMB_WINDOW)],
                                     gbuf_ref.at[s, pl.ds(e * COMB_WINDOW, COMB_WINDOW)], sems.at[s])

    def issue(tile, dest_r, s):
        @pl.when(ok_ref[tile] != 0)
        def _():
            for e in range(N_EXPERTS):
                window_copy(tile, e, s).start(priority=e % 2)

        @pl.when(ok_ref[tile] == 0)
        def _():
            def start(j, c):
                ts = [j * ROW_UNROLL + u for u in range(ROW_UNROLL)]
                srcs = [[dest_r[kk, t] for kk in range(TOP_K)] for t in ts]
                for t, src in zip(ts, srcs):
                    for kk in range(TOP_K):
                        pltpu.make_async_copy(yb_ref.at[pl.ds(src[kk], 1)],
                                              gbuf_ref.at[s, pl.ds(kk * tc + t, 1)],
                                              sems.at[s]).start(priority=kk % 2)
                return c

            lax.fori_loop(0, tc // ROW_UNROLL, start, 0)

    @pl.when(i == 0)
    def _():
        issue(0, dest_ref, 0)

    @pl.when(i + 1 < pl.num_programs(0))
    def _():
        issue(i + 1, destn_ref, 1 - slot)

    def to_cols(rows):
        pad = jnp.zeros((LANES - rows.shape[0], tc), F32)
        return jnp.concatenate([rows, pad], axis=0).T

    w_col = to_cols(w_ref[...])
    fast = ok_ref[i] != 0
    slow_pos = (lax.broadcasted_iota(I32, (TOP_K, tc), 0) * tc
                + lax.broadcasted_iota(I32, (TOP_K, tc), 1))
    pos_col = to_cols(jnp.where(fast, pos_ref[...], slow_pos).astype(F32))
    pp = _dot(p_ref[...].astype(BF16), wpp_ref[...])

    @pl.when(fast)
    def _():
        pltpu.make_async_copy(yb_ref.at[pl.ds(0, n_rows)], gbuf_ref.at[slot], sems.at[slot]).wait()

    @pl.when(jnp.logical_not(fast))
    def _():
        pltpu.make_async_copy(yb_ref.at[pl.ds(0, TOP_K * tc)], gbuf_ref.at[slot, pl.ds(0, TOP_K * tc)],
                              sems.at[slot]).wait()
        gbuf_ref[slot, pl.ds(TOP_K * tc, n_rows - TOP_K * tc), :] = jnp.zeros(
            (n_rows - TOP_K * tc, gbuf_ref.shape[2]), F32)

    ciota = lax.broadcasted_iota(I32, (tc, n_rows), 1).astype(F32)
    sel = jnp.zeros((tc, n_rows), F32)
    for kk in range(TOP_K):
        sel = jnp.where(ciota == pos_col[:, kk:kk + 1], w_col[:, kk:kk + 1], sel)
    moe = _dot(sel.astype(BF16), gbuf_ref[slot].astype(BF16))
    x2 = x1_ref[...] + moe
    h3 = _rms(x2, gple_ref[...]).astype(BF16)
    gate = jax.nn.sigmoid(_dot(h3, wpg_ref[...]))
    x3 = x2 + gate * pp
    out_ref[...] = _rms(x3, gfin_ref[...])


def _moe_combine(ws, ok, dest_t, pos_t, yb, w_t, x1, p2d, g_ple, w_ple_gate, w_ple_proj, g_final):
    T, D = x1.shape
    tc = min(COMB_TILE, T)
    n_tiles = T // tc
    ple = p2d.shape[1]
    smem_spec = pl.BlockSpec((TOP_K, tc), lambda i, *_: (0, i), memory_space=pltpu.SMEM)
    smem_next = pl.BlockSpec((TOP_K, tc), lambda i, *_: (0, jnp.minimum(i + 1, n_tiles - 1)),
                             memory_space=pltpu.SMEM)
    top_spec = pl.BlockSpec((TOP_K, tc), lambda i, *_: (0, i))
    const = lambda shape: pl.BlockSpec(shape, lambda i, *_: (0,) * len(shape),
                                       pipeline_mode=pl.Buffered(1))
    row = lambda a: a.reshape(1, -1)
    return pl.pallas_call(
        _comb_body,
        grid_spec=pltpu.PrefetchScalarGridSpec(
            num_scalar_prefetch=2,
            grid=(n_tiles,),
            in_specs=[smem_spec, smem_next,
                      pl.BlockSpec(memory_space=pl.ANY),
                      top_spec, top_spec,
                      pl.BlockSpec((tc, D), lambda i, *_: (i, 0)),
                      pl.BlockSpec((tc, ple), lambda i, *_: (i, 0)),
                      const((1, D)), const((D, D)), const((ple, D)), const((1, D))],
            out_specs=pl.BlockSpec((tc, D), lambda i, *_: (i, 0)),
            scratch_shapes=[pltpu.VMEM((2, N_EXPERTS * COMB_WINDOW, D), F32),
                            pltpu.SemaphoreType.DMA((2,))]),
        out_shape=jax.ShapeDtypeStruct((T, D), F32),
        compiler_params=pltpu.CompilerParams(dimension_semantics=("arbitrary",),
                                             vmem_limit_bytes=VMEM_LIMIT_BYTES),
        name="moe_combine",
    )(ws, ok, dest_t, dest_t, yb, pos_t, w_t, x1, p2d, row(g_ple), w_ple_gate.astype(BF16),
      w_ple_proj.astype(BF16), row(g_final))


def _combine_windows(pstart, trun, counts, idx_t, dest_t, tile, n_rows):
    n_tiles = trun.shape[0]
    tcnt = jnp.concatenate([trun[1:], counts[None, :]], axis=0) - trun
    start = pstart[None, :] + trun
    ws = jnp.minimum(start // SUBLANES * SUBLANES, n_rows - COMB_WINDOW).astype(I32)
    ok = jnp.all(start + tcnt - ws <= COMB_WINDOW, axis=1).astype(I32)
    ws_tok = _select_expert(jnp.repeat(ws, tile, axis=0)[None], idx_t)
    pos_t = idx_t * COMB_WINDOW + dest_t - ws_tok
    return ws.reshape(-1), ok, pos_t.astype(I32)


def _select_expert(table, idx_t):
    onehot = idx_t[:, :, None] == jnp.arange(N_EXPERTS, dtype=I32)
    return jnp.sum(jnp.where(onehot, table, 0), axis=-1)


def _group_layout(counts, n_blocks):
    te = EXPERT_TILE
    padded = (counts + te - 1) // te * te
    pends = jnp.cumsum(padded)
    pstart = (pends - padded).astype(I32)
    nused = (pends[-1] // te).astype(I32)
    tail = nused + jnp.arange(N_EXPERTS, dtype=I32)
    zstart = jnp.concatenate([jnp.where(padded > 0, pends - te, -1),
                              jnp.where(tail < n_blocks, tail * te, -1)]).astype(I32)
    blk_src = jnp.minimum(jnp.arange(n_blocks, dtype=I32), jnp.maximum(nused - 1, 0))
    blk_e = jnp.sum((blk_src * te)[:, None] >= pends[None, :], axis=1)
    blk_e = jnp.clip(blk_e, 0, N_EXPERTS - 1).astype(I32)
    blk = jnp.arange(n_blocks, dtype=I32)
    prev_e = jnp.concatenate([blk_e[:1], blk_e[:-1]])
    first = ((blk < nused) & ((blk == 0) | (blk_e != prev_e))).astype(I32)
    experts = jnp.arange(N_EXPERTS, dtype=I32)[None, :]
    later = (experts > blk_e[:, None]) & (counts[None, :] > 0)
    next_e = jnp.min(jnp.where(later, experts, N_EXPERTS), axis=1)
    next_e = jnp.where(next_e < N_EXPERTS, next_e, -1).astype(I32)
    return pstart, zstart, blk_e, blk_src, nused.reshape(1), first, next_e


def kernel(x, p, positions, g_mix_norm, w_in, g_ret_norm, w_pool, pool_scale, w_branch, w_out,
           g_ffn_norm, w_router, b_router, w_gate_up, b_gate_up, w_down, b_down,
           g_ple_norm, w_ple_gate, w_ple_proj, g_final):
    B, S, D = x.shape
    depth = w_in.shape[0]
    T = B * S
    xt = x.reshape(T, D)
    cos, sin = _rope_tables(positions.reshape(T, 1))
    n_blocks = (T * TOP_K) // EXPERT_TILE + N_EXPERTS
    for i in range(depth):
        x1, h2, idx_t, w_t, rank_t, cnt, trun = _token_mix(
            xt, cos, sin, g_mix_norm[i], w_in[i], g_ret_norm[i], w_pool[i], pool_scale[i],
            w_branch[i], w_out[i], g_ffn_norm[i], w_router[i], b_router[i], S)
        pstart, zstart, blk_e, blk_src, nused, first, next_e = _group_layout(cnt[:, 0], n_blocks)
        dest_t = _select_expert(pstart[None, None, :], idx_t) + rank_t
        xs = _moe_dispatch(h2, dest_t, zstart, n_blocks * EXPERT_TILE)
        yb = _moe_experts(xs, blk_e, blk_src, nused, first, next_e,
                          w_gate_up[i], b_gate_up[i], w_down[i], b_down[i])
        assert depth == 1
        assert min(MIX_TILE, S) == min(COMB_TILE, T)
        ws, ok, pos_t = _combine_windows(pstart, trun[:, :, 0], cnt[:, 0], idx_t, dest_t,
                                         min(COMB_TILE, T), n_blocks * EXPERT_TILE)
        xt = _moe_combine(ws, ok, dest_t, pos_t, yb, w_t, x1, p[i].reshape(T, -1), g_ple_norm[i],
                          w_ple_gate[i], w_ple_proj[i], g_final)
    return xt.reshape(B, S, D)
```

```python
import functools

import numpy as np
import jax
import jax.numpy as jnp
from jax import lax
from jax.experimental import pallas as pl
from jax.experimental.pallas import tpu as pltpu

F32 = jnp.float32
BF16 = jnp.bfloat16
I32 = jnp.int32

RET_HEADS = 8
RET_HEAD_DIM = 128
ROPE_BASE = 10000.0
GN_EPS = 1e-5
RMS_EPS = 1e-6
POOL_WINDOWS = (2, 4, 8, 16)
N_EXPERTS = 32
TOP_K = 4
SWIGLU_ALPHA = 1.702
SWIGLU_LIMIT = 7.0

LANES = 128
SUBLANES = 8
VMEM_LIMIT_BYTES = 56 * 1024 * 1024

MIX_TILE = 256
ROPE_TILE = 1024
DISP_TILE = 2048
EXPERT_CAP = 2304
EXPERT_TILE = 256
COMB_TILE = 256
POOL_HALO = 16
POOL_PAD = 8
COMB_WINDOW = 64
ROW_UNROLL = 8


def _const_spec(shape):
    nd = len(shape)
    return pl.BlockSpec(shape, lambda *_: (0,) * nd, pipeline_mode=pl.Buffered(1))


def _rms(x, g):
    return x * lax.rsqrt(jnp.mean(x * x, axis=-1, keepdims=True) + RMS_EPS) * g


def _dot(a, b):
    return jnp.dot(a, b, preferred_element_type=F32)


def _dot_nt(a, b, precision=None):
    return lax.dot_general(a, b, (((1,), (1,)), ((), ())),
                           preferred_element_type=F32, precision=precision)


def _dot_tn(a, b):
    return lax.dot_general(a, b, (((0,), (0,)), ((), ())), preferred_element_type=F32)


def _rope_body(pos_ref, inv_ref, sign_ref, cos_ref, sin_ref):
    ang = pos_ref[...].astype(F32) * inv_ref[...]
    cos_ref[...] = jnp.cos(ang)
    sin_ref[...] = jnp.sin(ang) * sign_ref[...]


def _rope_tables(pos_col):
    T = pos_col.shape[0]
    half = RET_HEAD_DIM // 2
    inv = ROPE_BASE ** (-jnp.arange(half, dtype=F32) / half)
    inv_full = jnp.concatenate([inv, inv]).reshape(1, RET_HEAD_DIM)
    sign = jnp.concatenate([-jnp.ones((half,), F32), jnp.ones((half,), F32)]).reshape(1, RET_HEAD_DIM)
    tile = min(ROPE_TILE, T)
    return pl.pallas_call(
        _rope_body,
        grid=(T // tile,),
        in_specs=[pl.BlockSpec((tile, 1), lambda i: (i, 0)),
                  pl.BlockSpec((1, RET_HEAD_DIM), lambda i: (0, 0)),
                  pl.BlockSpec((1, RET_HEAD_DIM), lambda i: (0, 0))],
        out_specs=[pl.BlockSpec((tile, RET_HEAD_DIM), lambda i: (i, 0)),
                   pl.BlockSpec((tile, RET_HEAD_DIM), lambda i: (i, 0))],
        out_shape=[jax.ShapeDtypeStruct((T, RET_HEAD_DIM), F32),
                   jax.ShapeDtypeStruct((T, RET_HEAD_DIM), F32)],
        name="rope_tables",
    )(pos_col, inv_full, sign)


def _retention_constants(tile):
    h = np.arange(RET_HEADS, dtype=np.float64)
    log_gamma = np.log1p(-np.exp2(-5.0 - h))
    idx = np.arange(tile, dtype=np.float64)
    diff = idx[:, None] - idx[None, :]
    dmat = np.where(diff >= 0, np.exp(log_gamma[:, None, None] * np.maximum(diff, 0.0)[None]), 0.0)
    xi = np.exp(log_gamma[:, None] * (idx + 1.0)[None])
    zeta = np.exp(log_gamma[:, None] * (tile - 1.0 - idx)[None])
    chunk_decay = np.exp(log_gamma * tile)
    xi_b = np.broadcast_to(xi[:, :, None], (RET_HEADS, tile, RET_HEAD_DIM))
    zeta_b = np.broadcast_to(zeta[:, :, None], (RET_HEADS, tile, RET_HEAD_DIM))
    return (jnp.asarray(dmat, F32), jnp.asarray(xi_b, F32), jnp.asarray(zeta_b, F32),
            tuple(float(c) for c in chunk_decay))


def _mix_body(x_ref, cos_ref, sin_ref, gmix_ref, win_ref, dmat_ref, xi_ref, zeta_ref, gret_ref,
              wpool_ref, pscale_ref, wbr_ref, wout_ref, gffn_ref, wr_ref, br_ref, tri_ref,
              x1_ref, h2_ref, idx_ref, w_ref, rank_ref, cnt_ref, trun_ref, xs_hbm,
              state_ref, ue_ref, lv_ref, run_ref, hbuf_ref, dvm_ref, dsm_ref, rsem, dsem,
              *, tiles_per_seq, chunk_decay, trash):
    tm, d_model = x_ref.shape
    ret_width = RET_HEADS * RET_HEAD_DIM
    i = pl.program_id(0)
    seq_tile = i % tiles_per_seq
    slot = i % 2
    prev = 1 - slot

    def row_copy(src_slot, t, dest):
        return pltpu.make_async_copy(hbuf_ref.at[src_slot, pl.ds(t, 1)], xs_hbm.at[pl.ds(dest, 1)], rsem)

    def wait_tile_rows():
        for _ in range(TOP_K):
            pltpu.make_async_copy(hbuf_ref.at[0], xs_hbm.at[pl.ds(0, tm)], rsem).wait()

    def dest_copy(s_):
        return pltpu.make_async_copy(dvm_ref, dsm_ref.at[s_], dsem)

    @pl.when(seq_tile == 0)
    def _():
        state_ref[...] = jnp.zeros(state_ref.shape, F32)
        ue_ref[0:POOL_PAD + POOL_HALO, :] = jnp.zeros((POOL_PAD + POOL_HALO, ue_ref.shape[1]), F32)

    @pl.when(i == 0)
    def _():
        run_ref[...] = jnp.zeros(run_ref.shape, F32)
        lv_ref[:, 0:POOL_PAD, :] = jnp.zeros((lv_ref.shape[0], POOL_PAD, lv_ref.shape[2]), F32)
        hbuf_ref[1] = jnp.zeros(hbuf_ref.shape[1:], F32)

        def spare(t, c):
            for kk in range(TOP_K):
                dsm_ref[1, kk, t] = trash + kk * tm + t
            return c

        lax.fori_loop(0, tm, spare, 0)

    @pl.when(i > 0)
    def _():
        dest_copy(prev).wait()

    x = x_ref[...]
    hb = _rms(x, gmix_ref[...]).astype(BF16)

    qkvg = _dot(hb, win_ref[:, 0:4 * ret_width])
    cos = cos_ref[...]
    sin = sin_ref[...]

    def rot(a):
        return a * cos + pltpu.roll(a, RET_HEAD_DIM // 2, 1) * sin

    ys = []
    for h in range(RET_HEADS):
        lo = h * RET_HEAD_DIM
        q = rot(qkvg[:, lo:lo + RET_HEAD_DIM]).astype(BF16)
        k = (rot(qkvg[:, ret_width + lo:ret_width + lo + RET_HEAD_DIM])
             * (RET_HEAD_DIM ** -0.5)).astype(BF16)
        v = qkvg[:, 2 * ret_width + lo:2 * ret_width + lo + RET_HEAD_DIM]
        g = qkvg[:, 3 * ret_width + lo:3 * ret_width + lo + RET_HEAD_DIM]
        scores = _dot_nt(q, k) * dmat_ref[h]
        inner = _dot(scores.astype(BF16), v.astype(BF16))
        st = state_ref[h]
        cross = _dot(q, st.astype(BF16)) * xi_ref[h]
        state_ref[h] = st * chunk_decay[h] + _dot_tn(k, (v * zeta_ref[h]).astype(BF16))
        ret = inner + cross
        mu = jnp.mean(ret, axis=-1, keepdims=True)
        dev = ret - mu
        var = jnp.mean(dev * dev, axis=-1, keepdims=True)
        rn = dev * lax.rsqrt(var + GN_EPS) * gret_ref[:, lo:lo + RET_HEAD_DIM]
        ys.append(((g * jax.nn.sigmoid(g)) * rn).astype(BF16))
        per_head = tm // RET_HEADS
        for t in range(h * per_head, (h + 1) * per_head):
            for kk in range(TOP_K):
                row_copy(prev, t, dsm_ref[prev, kk, t]).start(priority=kk % 2)
    y_ret = jnp.concatenate(ys, axis=1)
    branch_a = _dot(y_ret, wbr_ref[0])

    u = _dot(hb, win_ref[:, 4 * ret_width:4 * ret_width + d_model])
    top = POOL_PAD + POOL_HALO
    ext = tm + POOL_HALO
    group = d_model // len(POOL_WINDOWS)
    ue_ref[top:top + tm, :] = u
    s2 = ue_ref[POOL_PAD:POOL_PAD + ext, :] + ue_ref[POOL_PAD - 1:POOL_PAD - 1 + ext, :]
    lv_ref[0, POOL_PAD:POOL_PAD + ext, :] = s2
    s4 = s2[:, group:] + lv_ref[0, POOL_PAD - 2:POOL_PAD - 2 + ext, group:]
    lv_ref[1, POOL_PAD:POOL_PAD + ext, group:] = s4
    s8 = s4[:, group:] + lv_ref[1, POOL_PAD - 4:POOL_PAD - 4 + ext, 2 * group:]
    lv_ref[2, POOL_PAD:POOL_PAD + ext, 2 * group:] = s8
    sums = [lv_ref[0, top:top + tm, 0:group],
            lv_ref[1, top:top + tm, group:2 * group],
            lv_ref[2, top:top + tm, 2 * group:3 * group],
            lv_ref[2, top:top + tm, 3 * group:] + lv_ref[2, top - 8:top - 8 + tm, 3 * group:]]
    ue_ref[POOL_PAD:top, :] = ue_ref[tm + POOL_PAD:tm + top, :]
    pos = seq_tile * tm + lax.broadcasted_iota(I32, (tm, 1), 0)
    outs = []
    for gi, w in enumerate(POOL_WINDOWS):
        ug = u[:, gi * group:(gi + 1) * group]
        inv_count = 1.0 / jnp.minimum(pos + 1, w).astype(F32)
        mixed = (sums[gi] * inv_count - ug).astype(BF16)
        outs.append(_dot(mixed, wpool_ref[gi]))
    y_pool = (jnp.concatenate(outs, axis=1) * pscale_ref[...]).astype(BF16)
    branch_b = _dot(y_pool, wbr_ref[1])

    gates = _dot(hb, win_ref[:, 4 * ret_width + d_model:4 * ret_width + 3 * d_model])
    merged = (jax.nn.sigmoid(gates[:, 0:d_model]) * branch_a
              + jax.nn.sigmoid(gates[:, d_model:2 * d_model]) * branch_b)
    x1 = x + _dot(merged.astype(BF16), wout_ref[...])
    x1_ref[...] = x1

    h2 = _rms(x1, gffn_ref[...])
    h2_ref[...] = h2
    n_exp = br_ref.shape[0]
    h2_hi = h2.astype(BF16)
    h2_lo = (h2 - h2_hi.astype(F32)).astype(BF16)
    both = _dot(h2_hi, wr_ref[...])
    lg = both[:, 0:LANES] + both[:, LANES:2 * LANES] + _dot(h2_lo, wr_ref[:, 0:LANES])
    logits = lg.T[0:n_exp, :] + br_ref[...]
    eiota = lax.broadcasted_iota(I32, (n_exp, tm), 0)
    vals, idxs = [], []
    l = logits
    for _ in range(TOP_K):
        m = jnp.max(l, axis=0, keepdims=True)
        sel = jnp.min(jnp.where(l == m, eiota, n_exp), axis=0, keepdims=True)
        vals.append(m)
        idxs.append(sel)
        l = jnp.where(eiota == sel, -jnp.inf, l)
    exps = [jnp.exp(v - vals[0]) for v in vals]
    denom = exps[0] + exps[1] + exps[2] + exps[3]
    inv_denom = 1.0 / denom
    onehot = jnp.zeros((n_exp, tm), F32)
    for kk in range(TOP_K):
        onehot = onehot + (eiota == idxs[kk]).astype(F32)
    base = _dot(onehot.astype(BF16), tri_ref[...]) + run_ref[:, 0:1]
    tiota = lax.broadcasted_iota(I32, (1, tm), 1)
    for kk in range(TOP_K):
        idx_ref[kk:kk + 1, :] = idxs[kk]
        w_ref[kk:kk + 1, :] = exps[kk] * inv_denom
        rank = jnp.sum(jnp.where(eiota == idxs[kk], base, 0.0), axis=0, keepdims=True).astype(I32)
        rank_ref[kk:kk + 1, :] = rank
        dvm_ref[kk:kk + 1, :] = jnp.where(rank < EXPERT_CAP, idxs[kk] * EXPERT_CAP + rank,
                                          trash + kk * tm + tiota)
    trun_ref[...] = run_ref[...].astype(I32)
    run = run_ref[...] + jnp.sum(onehot, axis=1, keepdims=True)
    run_ref[...] = run
    cnt_ref[...] = run.astype(I32)

    hbuf_ref[slot] = h2
    wait_tile_rows()
    dest_copy(slot).start()

    @pl.when(i == pl.num_programs(0) - 1)
    def _():
        dest_copy(slot).wait()

        def start(j, c):
            ts = [j * ROW_UNROLL + u for u in range(ROW_UNROLL)]
            dests = [[dsm_ref[slot, kk, t] for kk in range(TOP_K)] for t in ts]
            for t, dest in zip(ts, dests):
                for kk in range(TOP_K):
                    row_copy(slot, t, dest[kk]).start(priority=kk % 2)
            return c

        lax.fori_loop(0, tm // ROW_UNROLL, start, 0)
        wait_tile_rows()


def _split_router(w_router):
    d, e = w_router.shape
    hi = w_router.astype(BF16)
    lo = (w_router - hi.astype(F32)).astype(BF16)
    pad = jnp.zeros((d, LANES - e), BF16)
    return jnp.concatenate([hi, pad, lo, pad], axis=1)


def _token_mix(x2d, cos, sin, g_mix, w_in, g_ret, w_pool, pool_scale, w_branch, w_out,
               g_ffn, w_router, b_router, seq_len):
    T, D = x2d.shape
    tm = min(MIX_TILE, seq_len)
    in_width = w_in.shape[1]
    ret_width = RET_HEADS * RET_HEAD_DIM
    assert POOL_WINDOWS == (2, 4, 8, 16)
    dmat, xi_b, zeta_b, chunk_decay = _retention_constants(tm)
    tri = jnp.asarray(np.triu(np.ones((tm, tm), np.float32), 1), BF16)
    row = lambda a: a.reshape(1, -1)
    tile_spec = lambda w: pl.BlockSpec((tm, w), lambda i: (i, 0))
    top_spec = pl.BlockSpec((TOP_K, tm), lambda i: (0, i))
    n_xs = max(N_EXPERTS * EXPERT_CAP, (T * TOP_K // EXPERT_TILE + N_EXPERTS) * EXPERT_TILE)
    body = functools.partial(_mix_body, tiles_per_seq=seq_len // tm, chunk_decay=chunk_decay, trash=n_xs)
    return pl.pallas_call(
        body,
        grid=(T // tm,),
        in_specs=[tile_spec(D), tile_spec(RET_HEAD_DIM), tile_spec(RET_HEAD_DIM),
                  _const_spec((1, D)), _const_spec((D, in_width)),
                  _const_spec(dmat.shape), _const_spec(xi_b.shape), _const_spec(zeta_b.shape),
                  _const_spec((1, ret_width)), _const_spec(w_pool.shape), _const_spec((1, D)),
                  _const_spec(w_branch.shape), _const_spec((D, D)), _const_spec((1, D)),
                  _const_spec((D, 2 * LANES)), _const_spec((N_EXPERTS, 1)), _const_spec((tm, tm))],
        out_specs=[tile_spec(D), tile_spec(D), top_spec, top_spec, top_spec,
                   pl.BlockSpec((N_EXPERTS, LANES), lambda i: (0, 0)),
                   pl.BlockSpec((None, N_EXPERTS, LANES), lambda i: (i, 0, 0)),
                   pl.BlockSpec(memory_space=pl.ANY)],
        out_shape=[jax.ShapeDtypeStruct((T, D), F32), jax.ShapeDtypeStruct((T, D), F32),
                   jax.ShapeDtypeStruct((TOP_K, T), I32), jax.ShapeDtypeStruct((TOP_K, T), F32),
                   jax.ShapeDtypeStruct((TOP_K, T), I32),
                   jax.ShapeDtypeStruct((N_EXPERTS, LANES), I32),
                   jax.ShapeDtypeStruct((T // tm, N_EXPERTS, LANES), I32),
                   jax.ShapeDtypeStruct((n_xs + TOP_K * tm, D), F32)],
        scratch_shapes=[pltpu.VMEM((RET_HEADS, RET_HEAD_DIM, RET_HEAD_DIM), F32),
                        pltpu.VMEM((tm + POOL_PAD + POOL_HALO, D), F32),
                        pltpu.VMEM((3, tm + POOL_PAD + POOL_HALO, D), F32),
                        pltpu.VMEM((N_EXPERTS, LANES), F32),
                        pltpu.VMEM((2, tm, D), F32), pltpu.VMEM((TOP_K, tm), I32),
                        pltpu.SMEM((2, TOP_K, tm), I32),
                        pltpu.SemaphoreType.DMA, pltpu.SemaphoreType.DMA],
        compiler_params=pltpu.CompilerParams(dimension_semantics=("arbitrary",),
                                             vmem_limit_bytes=VMEM_LIMIT_BYTES),
        name="token_mix",
    )(x2d, cos, sin, row(g_mix), w_in.astype(BF16), dmat, xi_b, zeta_b, row(g_ret),
      w_pool.astype(BF16), row(pool_scale), w_branch.astype(BF16), w_out.astype(BF16),
      row(g_ffn), _split_router(w_router), b_router.reshape(-1, 1), tri)


def _fixup_body(flag_ref, zstart_ref, pad_ref, dest_ref, h2_hbm, xs_in, xs_ref, hbuf_ref, zbuf_ref,
                sem, zsem, hsem):
    del xs_in
    td = hbuf_ref.shape[0]
    te = zbuf_ref.shape[0]
    i = pl.program_id(0)
    overflow = flag_ref[0] != 0

    @pl.when(i == 0)
    def _():
        zbuf_ref[...] = jnp.zeros(zbuf_ref.shape, F32)

    def pad_copies(e):
        first = pad_ref[e]
        n = pad_ref[N_EXPERTS + e]
        head = jnp.minimum(n, (-first) % SUBLANES)
        out = []
        for r in range(SUBLANES - 1):
            out.append((r < head, pltpu.make_async_copy(zbuf_ref.at[pl.ds(0, 1)],
                                                        xs_ref.at[pl.ds(first + r, 1)], zsem)))
        rest = n - head
        pos = first + head
        size = te // 2
        while size >= SUBLANES:
            at = pl.multiple_of(pos + (rest & ~(2 * size - 1)), SUBLANES)
            out.append(((rest & size) != 0,
                        pltpu.make_async_copy(zbuf_ref.at[pl.ds(0, size)], xs_ref.at[pl.ds(at, size)], zsem)))
            size //= 2
        used_end = first + n
        for b in range(EXPERT_CAP // te):
            at = pl.multiple_of(used_end + b * te, te)
            out.append((at < (e + 1) * EXPERT_CAP,
                        pltpu.make_async_copy(zbuf_ref, xs_ref.at[pl.ds(at, te)], zsem)))
        return out

    @pl.when((i == 0) & jnp.logical_not(overflow))
    def _():
        def start(e, c):
            for cond, cp in pad_copies(e):
                @pl.when(cond)
                def _():
                    cp.start()
            return c

        def wait(e, c):
            for cond, cp in pad_copies(e):
                @pl.when(cond)
                def _():
                    cp.wait()
            return c

        lax.fori_loop(0, N_EXPERTS, start, 0)
        lax.fori_loop(0, N_EXPERTS, wait, 0)

    def zero_copy(e):
        start = pl.multiple_of(jnp.maximum(zstart_ref[e], 0), te)
        return pltpu.make_async_copy(zbuf_ref, xs_ref.at[pl.ds(start, te)], zsem)

    @pl.when((i == 0) & overflow)
    def _():
        def start(e, c):
            @pl.when(zstart_ref[e] >= 0)
            def _():
                zero_copy(e).start()
            return c

        def wait(e, c):
            @pl.when(zstart_ref[e] >= 0)
            def _():
                zero_copy(e).wait()
            return c

        lax.fori_loop(0, zstart_ref.shape[0], start, 0)
        lax.fori_loop(0, zstart_ref.shape[0], wait, 0)

    @pl.when(overflow)
    def _():
        tile = pltpu.make_async_copy(h2_hbm.at[pl.ds(i * td, td)], hbuf_ref, hsem)
        tile.start()
        tile.wait()

        def start(j, c):
            ts = [j * ROW_UNROLL + u for u in range(ROW_UNROLL)]
            dests = [[dest_ref[t * TOP_K + kk] for kk in range(TOP_K)] for t in ts]
            for t, dest in zip(ts, dests):
                for kk in range(TOP_K):
                    pltpu.make_async_copy(hbuf_ref.at[pl.ds(t, 1)], xs_ref.at[pl.ds(dest[kk], 1)],
                                          sem).start(priority=kk % 2)
            return c

        lax.fori_loop(0, td // ROW_UNROLL, start, 0)
        for kk in range(TOP_K):
            pltpu.make_async_copy(hbuf_ref, xs_ref.at[pl.ds(0, td)], sem).wait()


def _moe_dispatch_fixup(overflow, zstart, pad, dest_t, h2, xs):
    T, D = h2.shape
    td = min(DISP_TILE, T)
    smem_spec = pl.BlockSpec((TOP_K * td,), lambda i, *_: (i,), memory_space=pltpu.SMEM)
    return pl.pallas_call(
        _fixup_body,
        grid_spec=pltpu.PrefetchScalarGridSpec(
            num_scalar_prefetch=3,
            grid=(T // td,),
            in_specs=[smem_spec, pl.BlockSpec(memory_space=pl.ANY), pl.BlockSpec(memory_space=pl.ANY)],
            out_specs=pl.BlockSpec(memory_space=pl.ANY),
            scratch_shapes=[pltpu.VMEM((td, D), F32), pltpu.VMEM((EXPERT_TILE, D), F32),
                            pltpu.SemaphoreType.DMA, pltpu.SemaphoreType.DMA, pltpu.SemaphoreType.DMA]),
        out_shape=jax.ShapeDtypeStruct(xs.shape, xs.dtype),
        input_output_aliases={5: 0},
        compiler_params=pltpu.CompilerParams(dimension_semantics=("arbitrary",),
                                             vmem_limit_bytes=VMEM_LIMIT_BYTES),
        name="moe_dispatch_fixup",
    )(overflow, zstart, pad, dest_t.T.reshape(-1), h2, xs)


def _expert_body(blk_e_ref, blk_src_ref, nused_ref, first_ref, next_e_ref, xs_ref, wgu_hbm, bgu_ref,
                 wd_hbm, bd_ref, yb_ref, wgu_f32_ref, wd_f32_ref, wgu_bf_ref, wd_bf_ref, wsem):
    i = pl.program_id(0)
    d_ff = wd_bf_ref.shape[0]

    def weight_copies(e):
        return (pltpu.make_async_copy(wgu_hbm.at[pl.ds(e, 1)], wgu_f32_ref, wsem.at[0]),
                pltpu.make_async_copy(wd_hbm.at[pl.ds(e, 1)], wd_f32_ref, wsem.at[1]))

    @pl.when(i == 0)
    def _():
        for cp in weight_copies(blk_e_ref[0]):
            cp.start()

    @pl.when(first_ref[i] != 0)
    def _():
        for cp in weight_copies(blk_e_ref[i]):
            cp.wait()
        wgu_bf_ref[...] = wgu_f32_ref[0].astype(BF16)
        wd_bf_ref[...] = wd_f32_ref[0].astype(BF16)

    @pl.when((first_ref[i] != 0) & (next_e_ref[i] >= 0))
    def _():
        for cp in weight_copies(next_e_ref[i]):
            cp.start()

    @pl.when(i < nused_ref[0])
    def _():
        gu = _dot(xs_ref[...].astype(BF16), wgu_bf_ref[...]) + bgu_ref[...]
        gate = jnp.minimum(gu[:, 0:d_ff], SWIGLU_LIMIT)
        up = jnp.clip(gu[:, d_ff:2 * d_ff], -SWIGLU_LIMIT, SWIGLU_LIMIT)
        act = (up + 1.0) * (gate * jax.nn.sigmoid(gate * SWIGLU_ALPHA))
        yb_ref[...] = _dot(act.astype(BF16), wd_bf_ref[...]) + bd_ref[...]

    @pl.when(i >= nused_ref[0])
    def _():
        yb_ref[...] = jnp.zeros(yb_ref.shape, F32)


def _moe_experts(xs, blk_e, blk_src, nused, first, next_e, w_gate_up, b_gate_up, w_down, b_down):
    D = xs.shape[1]
    E, _, two_ff = w_gate_up.shape
    d_ff = two_ff // 2
    te = EXPERT_TILE
    n_rows = blk_e.shape[0] * te
    return pl.pallas_call(
        _expert_body,
        grid_spec=pltpu.PrefetchScalarGridSpec(
            num_scalar_prefetch=5,
            grid=(n_rows // te,),
            in_specs=[pl.BlockSpec((te, D), lambda i, be, bs, *_: (bs[i], 0)),
                      pl.BlockSpec(memory_space=pl.ANY),
                      pl.BlockSpec((None, 1, two_ff), lambda i, be, *_: (be[i], 0, 0)),
                      pl.BlockSpec(memory_space=pl.ANY),
                      pl.BlockSpec((None, 1, D), lambda i, be, *_: (be[i], 0, 0))],
            out_specs=pl.BlockSpec((te, D), lambda i, *_: (i, 0)),
            scratch_shapes=[pltpu.VMEM((1, D, two_ff), F32), pltpu.VMEM((1, d_ff, D), F32),
                            pltpu.VMEM((D, two_ff), BF16), pltpu.VMEM((d_ff, D), BF16),
                            pltpu.SemaphoreType.DMA((2,))]),
        out_shape=jax.ShapeDtypeStruct((n_rows, D), F32),
        compiler_params=pltpu.CompilerParams(dimension_semantics=("arbitrary",),
                                             vmem_limit_bytes=VMEM_LIMIT_BYTES),
        name="moe_experts",
    )(blk_e, blk_src, nused, first, next_e, xs, w_gate_up, b_gate_up.reshape(E, 1, two_ff),
      w_down, b_down.reshape(E, 1, D))


def _comb_body(ws_ref, ok_ref, dest_ref, destn_ref, yb_ref, pos_ref, w_ref, x1_ref, p_ref,
               gple_ref, wpg_ref, wpp_ref, gfin_ref, out_ref, gbuf_ref, sems):
    tc = x1_ref.shape[0]
    n_rows = gbuf_ref.shape[1]
    i = pl.program_id(0)
    slot = i % 2

    def window_copy(tile, e, s):
        src = pl.multiple_of(ws_ref[tile * N_EXPERTS + e], SUBLANES)
        return pltpu.make_async_copy(yb_ref.at[pl.ds(src, COMB_WINDOW)],
                                     gbuf_ref.at[s, pl.ds(e * COMB_WINDOW, COMB_WINDOW)], sems.at[s])

    def issue(tile, dest_r, s):
        @pl.when(ok_ref[tile] != 0)
        def _():
            for e in range(N_EXPERTS):
                window_copy(tile, e, s).start(priority=e % 2)

        @pl.when(ok_ref[tile] == 0)
        def _():
            def start(j, c):
                ts = [j * ROW_UNROLL + u for u in range(ROW_UNROLL)]
                srcs = [[dest_r[kk, t] for kk in range(TOP_K)] for t in ts]
                for t, src in zip(ts, srcs):
                    for kk in range(TOP_K):
                        pltpu.make_async_copy(yb_ref.at[pl.ds(src[kk], 1)],
                                              gbuf_ref.at[s, pl.ds(kk * tc + t, 1)],
                                              sems.at[s]).start(priority=kk % 2)
                return c

            lax.fori_loop(0, tc // ROW_UNROLL, start, 0)

    @pl.when(i == 0)
    def _():
        issue(0, dest_ref, 0)

    @pl.when(i + 1 < pl.num_programs(0))
    def _():
        issue(i + 1, destn_ref, 1 - slot)

    def to_cols(rows):
        pad = jnp.zeros((LANES - rows.shape[0], tc), F32)
        return jnp.concatenate([rows, pad], axis=0).T

    w_col = to_cols(w_ref[...])
    fast = ok_ref[i] != 0
    slow_pos = (lax.broadcasted_iota(I32, (TOP_K, tc), 0) * tc
                + lax.broadcasted_iota(I32, (TOP_K, tc), 1))
    pos_col = to_cols(jnp.where(fast, pos_ref[...], slow_pos).astype(F32))
    pp = _dot(p_ref[...].astype(BF16), wpp_ref[...])

    @pl.when(fast)
    def _():
        pltpu.make_async_copy(yb_ref.at[pl.ds(0, n_rows)], gbuf_ref.at[slot], sems.at[slot]).wait()

    @pl.when(jnp.logical_not(fast))
    def _():
        pltpu.make_async_copy(yb_ref.at[pl.ds(0, TOP_K * tc)], gbuf_ref.at[slot, pl.ds(0, TOP_K * tc)],
                              sems.at[slot]).wait()
        gbuf_ref[slot, pl.ds(TOP_K * tc, n_rows - TOP_K * tc), :] = jnp.zeros(
            (n_rows - TOP_K * tc, gbuf_ref.shape[2]), F32)

    ciota = lax.broadcasted_iota(I32, (tc, n_rows), 1).astype(F32)
    sel = jnp.zeros((tc, n_rows), F32)
    for kk in range(TOP_K):
        sel = jnp.where(ciota == pos_col[:, kk:kk + 1], w_col[:, kk:kk + 1], sel)
    moe = _dot(sel.astype(BF16), gbuf_ref[slot].astype(BF16))
    x2 = x1_ref[...] + moe
    h3 = _rms(x2, gple_ref[...]).astype(BF16)
    gate = jax.nn.sigmoid(_dot(h3, wpg_ref[...]))
    x3 = x2 + gate * pp
    out_ref[...] = _rms(x3, gfin_ref[...])


def _moe_combine(ws, ok, dest_t, pos_t, yb, w_t, x1, p2d, g_ple, w_ple_gate, w_ple_proj, g_final):
    T, D = x1.shape
    tc = min(COMB_TILE, T)
    n_tiles = T // tc
    ple = p2d.shape[1]
    smem_spec = pl.BlockSpec((TOP_K, tc), lambda i, *_: (0, i), memory_space=pltpu.SMEM)
    smem_next = pl.BlockSpec((TOP_K, tc), lambda i, *_: (0, jnp.minimum(i + 1, n_tiles - 1)),
                             memory_space=pltpu.SMEM)
    top_spec = pl.BlockSpec((TOP_K, tc), lambda i, *_: (0, i))
    const = lambda shape: pl.BlockSpec(shape, lambda i, *_: (0,) * len(shape),
                                       pipeline_mode=pl.Buffered(1))
    row = lambda a: a.reshape(1, -1)
    return pl.pallas_call(
        _comb_body,
        grid_spec=pltpu.PrefetchScalarGridSpec(
            num_scalar_prefetch=2,
            grid=(n_tiles,),
            in_specs=[smem_spec, smem_next,
                      pl.BlockSpec(memory_space=pl.ANY),
                      top_spec, top_spec,
                      pl.BlockSpec((tc, D), lambda i, *_: (i, 0)),
                      pl.BlockSpec((tc, ple), lambda i, *_: (i, 0)),
                      const((1, D)), const((D, D)), const((ple, D)), const((1, D))],
            out_specs=pl.BlockSpec((tc, D), lambda i, *_: (i, 0)),
            scratch_shapes=[pltpu.VMEM((2, N_EXPERTS * COMB_WINDOW, D), F32),
                            pltpu.SemaphoreType.DMA((2,))]),
        out_shape=jax.ShapeDtypeStruct((T, D), F32),
        compiler_params=pltpu.CompilerParams(dimension_semantics=("arbitrary",),
                                             vmem_limit_bytes=VMEM_LIMIT_BYTES),
        name="moe_combine",
    )(ws, ok, dest_t, dest_t, yb, pos_t, w_t, x1, p2d, row(g_ple), w_ple_gate.astype(BF16),
      w_ple_proj.astype(BF16), row(g_final))


def _combine_windows(pstart, trun, counts, idx_t, dest_t, tile, n_rows):
    n_tiles = trun.shape[0]
    tcnt = jnp.concatenate([trun[1:], counts[None, :]], axis=0) - trun
    start = pstart[None, :] + trun
    ws = jnp.minimum(start // SUBLANES * SUBLANES, n_rows - COMB_WINDOW).astype(I32)
    ok = jnp.all(start + tcnt - ws <= COMB_WINDOW, axis=1).astype(I32)
    ws_tok = _select_expert(jnp.repeat(ws, tile, axis=0)[None], idx_t)
    pos_t = idx_t * COMB_WINDOW + dest_t - ws_tok
    return ws.reshape(-1), ok, pos_t.astype(I32)


def _select_expert(table, idx_t):
    onehot = idx_t[:, :, None] == jnp.arange(N_EXPERTS, dtype=I32)
    return jnp.sum(jnp.where(onehot, table, 0), axis=-1)


def _group_layout(counts, n_blocks):
    te = EXPERT_TILE
    padded = (counts + te - 1) // te * te
    pends = jnp.cumsum(padded)
    pstart = (pends - padded).astype(I32)
    nused = (pends[-1] // te).astype(I32)
    tail = nused + jnp.arange(N_EXPERTS, dtype=I32)
    zstart = jnp.concatenate([jnp.where(padded > 0, pends - te, -1),
                              jnp.where(tail < n_blocks, tail * te, -1)]).astype(I32)
    blk_src = jnp.minimum(jnp.arange(n_blocks, dtype=I32), jnp.maximum(nused - 1, 0))
    blk_e = jnp.sum((blk_src * te)[:, None] >= pends[None, :], axis=1)
    blk_e = jnp.clip(blk_e, 0, N_EXPERTS - 1).astype(I32)
    blk = jnp.arange(n_blocks, dtype=I32)
    prev_e = jnp.concatenate([blk_e[:1], blk_e[:-1]])
    first = ((blk < nused) & ((blk == 0) | (blk_e != prev_e))).astype(I32)
    experts = jnp.arange(N_EXPERTS, dtype=I32)[None, :]
    later = (experts > blk_e[:, None]) & (counts[None, :] > 0)
    next_e = jnp.min(jnp.where(later, experts, N_EXPERTS), axis=1)
    next_e = jnp.where(next_e < N_EXPERTS, next_e, -1).astype(I32)
    overflow = jnp.any(counts > EXPERT_CAP)
    pstart_blk = jnp.sum(jnp.where(blk_e[:, None] == experts, pstart[None, :], 0), axis=1)
    blk_cap = blk_e * (EXPERT_CAP // te) + (blk_src - pstart_blk // te)
    blk_src = jnp.where(overflow, blk_src, blk_cap).astype(I32)
    pad = jnp.concatenate([jnp.arange(N_EXPERTS, dtype=I32) * EXPERT_CAP + counts,
                           padded - counts]).astype(I32)
    return (pstart, zstart, blk_e, blk_src, nused.reshape(1), first, next_e,
            overflow.astype(I32).reshape(1), pad)


def kernel(x, p, positions, g_mix_norm, w_in, g_ret_norm, w_pool, pool_scale, w_branch, w_out,
           g_ffn_norm, w_router, b_router, w_gate_up, b_gate_up, w_down, b_down,
           g_ple_norm, w_ple_gate, w_ple_proj, g_final):
    B, S, D = x.shape
    depth = w_in.shape[0]
    T = B * S
    xt = x.reshape(T, D)
    cos, sin = _rope_tables(positions.reshape(T, 1))
    n_blocks = (T * TOP_K) // EXPERT_TILE + N_EXPERTS
    for i in range(depth):
        x1, h2, idx_t, w_t, rank_t, cnt, trun, xs = _token_mix(
            xt, cos, sin, g_mix_norm[i], w_in[i], g_ret_norm[i], w_pool[i], pool_scale[i],
            w_branch[i], w_out[i], g_ffn_norm[i], w_router[i], b_router[i], S)
        (pstart, zstart, blk_e, blk_src, nused, first, next_e,
         overflow, pad) = _group_layout(cnt[:, 0], n_blocks)
        dest_t = _select_expert(pstart[None, None, :], idx_t) + rank_t
        xs = _moe_dispatch_fixup(overflow, zstart, pad, dest_t, h2, xs)
        yb = _moe_experts(xs, blk_e, blk_src, nused, first, next_e,
                          w_gate_up[i], b_gate_up[i], w_down[i], b_down[i])
        assert depth == 1
        assert min(MIX_TILE, S) == min(COMB_TILE, T)
        ws, ok, pos_t = _combine_windows(pstart, trun[:, :, 0], cnt[:, 0], idx_t, dest_t,
                                         min(COMB_TILE, T), n_blocks * EXPERT_TILE)
        xt = _moe_combine(ws, ok, dest_t, pos_t, yb, w_t, x1, p[i].reshape(T, -1), g_ple_norm[i],
                          w_ple_gate[i], w_ple_proj[i], g_final)
    return xt.reshape(B, S, D)
```

```python
import functools

import numpy as np
import jax
import jax.numpy as jnp
from jax import lax
from jax.experimental import pallas as pl
from jax.experimental.pallas import tpu as pltpu

F32 = jnp.float32
BF16 = jnp.bfloat16
I32 = jnp.int32

RET_HEADS = 8
RET_HEAD_DIM = 128
ROPE_BASE = 10000.0
GN_EPS = 1e-5
RMS_EPS = 1e-6
POOL_WINDOWS = (2, 4, 8, 16)
N_EXPERTS = 32
TOP_K = 4
SWIGLU_ALPHA = 1.702
SWIGLU_LIMIT = 7.0

LANES = 128
SUBLANES = 8
VMEM_LIMIT_BYTES = 56 * 1024 * 1024

MIX_TILE = 256
ROPE_TILE = 1024
DISP_TILE = 2048
EXPERT_CAP = 2816
EXPERT_TILE = 256
COMB_TILE = 256
POOL_HALO = 16
POOL_PAD = 8
COMB_WINDOW = 64
ROW_UNROLL = 8


def _const_spec(shape):
    nd = len(shape)
    return pl.BlockSpec(shape, lambda *_: (0,) * nd, pipeline_mode=pl.Buffered(1))


def _zero_unused_capacity(e, cnt, zbuf_ref, xs_ref, zsem):
    te = zbuf_ref.shape[0]
    cnt = jnp.minimum(cnt, EXPERT_CAP)
    first = e * EXPERT_CAP + cnt
    n = (-cnt) % te
    head = jnp.minimum(n, (-first) % SUBLANES)
    out = []
    for r in range(SUBLANES - 1):
        out.append((r < head, pltpu.make_async_copy(zbuf_ref.at[pl.ds(0, 1)],
                                                    xs_ref.at[pl.ds(first + r, 1)], zsem)))
    rest = n - head
    pos = first + head
    size = te // 2
    while size >= SUBLANES:
        at = pl.multiple_of(pos + (rest & ~(2 * size - 1)), SUBLANES)
        out.append(((rest & size) != 0,
                    pltpu.make_async_copy(zbuf_ref.at[pl.ds(0, size)], xs_ref.at[pl.ds(at, size)], zsem)))
        size //= 2
    used_end = first + n
    for b in range(EXPERT_CAP // te):
        at = pl.multiple_of(used_end + b * te, te)
        out.append((at < (e + 1) * EXPERT_CAP,
                    pltpu.make_async_copy(zbuf_ref, xs_ref.at[pl.ds(at, te)], zsem)))
    return out


def _rms(x, g):
    return x * lax.rsqrt(jnp.mean(x * x, axis=-1, keepdims=True) + RMS_EPS) * g


def _dot(a, b):
    return jnp.dot(a, b, preferred_element_type=F32)


def _dot_nt(a, b, precision=None):
    return lax.dot_general(a, b, (((1,), (1,)), ((), ())),
                           preferred_element_type=F32, precision=precision)


def _dot_tn(a, b):
    return lax.dot_general(a, b, (((0,), (0,)), ((), ())), preferred_element_type=F32)


def _rope_body(pos_ref, inv_ref, sign_ref, cos_ref, sin_ref):
    ang = pos_ref[...].astype(F32) * inv_ref[...]
    cos_ref[...] = jnp.cos(ang)
    sin_ref[...] = jnp.sin(ang) * sign_ref[...]


def _rope_tables(pos_col):
    T = pos_col.shape[0]
    half = RET_HEAD_DIM // 2
    inv = ROPE_BASE ** (-jnp.arange(half, dtype=F32) / half)
    inv_full = jnp.concatenate([inv, inv]).reshape(1, RET_HEAD_DIM)
    sign = jnp.concatenate([-jnp.ones((half,), F32), jnp.ones((half,), F32)]).reshape(1, RET_HEAD_DIM)
    tile = min(ROPE_TILE, T)
    return pl.pallas_call(
        _rope_body,
        grid=(T // tile,),
        in_specs=[pl.BlockSpec((tile, 1), lambda i: (i, 0)),
                  pl.BlockSpec((1, RET_HEAD_DIM), lambda i: (0, 0)),
                  pl.BlockSpec((1, RET_HEAD_DIM), lambda i: (0, 0))],
        out_specs=[pl.BlockSpec((tile, RET_HEAD_DIM), lambda i: (i, 0)),
                   pl.BlockSpec((tile, RET_HEAD_DIM), lambda i: (i, 0))],
        out_shape=[jax.ShapeDtypeStruct((T, RET_HEAD_DIM), F32),
                   jax.ShapeDtypeStruct((T, RET_HEAD_DIM), F32)],
        name="rope_tables",
    )(pos_col, inv_full, sign)


def _retention_constants(tile):
    h = np.arange(RET_HEADS, dtype=np.float64)
    log_gamma = np.log1p(-np.exp2(-5.0 - h))
    idx = np.arange(tile, dtype=np.float64)
    diff = idx[:, None] - idx[None, :]
    dmat = np.where(diff >= 0, np.exp(log_gamma[:, None, None] * np.maximum(diff, 0.0)[None]), 0.0)
    xi = np.exp(log_gamma[:, None] * (idx + 1.0)[None])
    zeta = np.exp(log_gamma[:, None] * (tile - 1.0 - idx)[None])
    chunk_decay = np.exp(log_gamma * tile)
    xi_b = np.broadcast_to(xi[:, :, None], (RET_HEADS, tile, RET_HEAD_DIM))
    zeta_b = np.broadcast_to(zeta[:, :, None], (RET_HEADS, tile, RET_HEAD_DIM))
    return (jnp.asarray(dmat, F32), jnp.asarray(xi_b, F32), jnp.asarray(zeta_b, F32),
            tuple(float(c) for c in chunk_decay))


def _mix_body(x_ref, cos_ref, sin_ref, gmix_ref, win_ref, dmat_ref, xi_ref, zeta_ref, gret_ref,
              wpool_ref, pscale_ref, wbr_ref, wout_ref, gffn_ref, wr_ref, br_ref, tri_ref,
              x1_ref, h2_ref, idx_ref, w_ref, rank_ref, cnt_ref, trun_ref, xs_hbm,
              state_ref, ue_ref, lv_ref, run_ref, hbuf_ref, dvm_ref, dsm_ref, cvm_ref, csm_ref, zbuf_ref,
              rsem, dsem, zsem, *, tiles_per_seq, chunk_decay, trash):
    tm, d_model = x_ref.shape
    ret_width = RET_HEADS * RET_HEAD_DIM
    i = pl.program_id(0)
    seq_tile = i % tiles_per_seq
    slot = i % 2
    prev = 1 - slot

    def row_copy(src_slot, t, dest):
        return pltpu.make_async_copy(hbuf_ref.at[src_slot, pl.ds(t, 1)], xs_hbm.at[pl.ds(dest, 1)], rsem)

    def wait_tile_rows():
        for _ in range(TOP_K):
            pltpu.make_async_copy(hbuf_ref.at[0], xs_hbm.at[pl.ds(0, tm)], rsem).wait()

    def dest_copy(s_):
        return pltpu.make_async_copy(dvm_ref, dsm_ref.at[s_], dsem)

    @pl.when(seq_tile == 0)
    def _():
        state_ref[...] = jnp.zeros(state_ref.shape, F32)
        ue_ref[0:POOL_PAD + POOL_HALO, :] = jnp.zeros((POOL_PAD + POOL_HALO, ue_ref.shape[1]), F32)

    @pl.when(i == 0)
    def _():
        run_ref[...] = jnp.zeros(run_ref.shape, F32)
        lv_ref[:, 0:POOL_PAD, :] = jnp.zeros((lv_ref.shape[0], POOL_PAD, lv_ref.shape[2]), F32)
        hbuf_ref[1] = jnp.zeros(hbuf_ref.shape[1:], F32)

        def spare(t, c):
            for kk in range(TOP_K):
                dsm_ref[1, kk, t] = trash + kk * tm + t
            return c

        lax.fori_loop(0, tm, spare, 0)

    @pl.when(i > 0)
    def _():
        dest_copy(prev).wait()

    x = x_ref[...]
    hb = _rms(x, gmix_ref[...]).astype(BF16)

    qkvg = _dot(hb, win_ref[:, 0:4 * ret_width])
    cos = cos_ref[...]
    sin = sin_ref[...]

    def rot(a):
        return a * cos + pltpu.roll(a, RET_HEAD_DIM // 2, 1) * sin

    ys = []
    for h in range(RET_HEADS):
        lo = h * RET_HEAD_DIM
        q = rot(qkvg[:, lo:lo + RET_HEAD_DIM]).astype(BF16)
        k = (rot(qkvg[:, ret_width + lo:ret_width + lo + RET_HEAD_DIM])
             * (RET_HEAD_DIM ** -0.5)).astype(BF16)
        v = qkvg[:, 2 * ret_width + lo:2 * ret_width + lo + RET_HEAD_DIM]
        g = qkvg[:, 3 * ret_width + lo:3 * ret_width + lo + RET_HEAD_DIM]
        scores = _dot_nt(q, k) * dmat_ref[h]
        inner = _dot(scores.astype(BF16), v.astype(BF16))
        st = state_ref[h]
        cross = _dot(q, st.astype(BF16)) * xi_ref[h]
        state_ref[h] = st * chunk_decay[h] + _dot_tn(k, (v * zeta_ref[h]).astype(BF16))
        ret = inner + cross
        mu = jnp.mean(ret, axis=-1, keepdims=True)
        dev = ret - mu
        var = jnp.mean(dev * dev, axis=-1, keepdims=True)
        rn = dev * lax.rsqrt(var + GN_EPS) * gret_ref[:, lo:lo + RET_HEAD_DIM]
        ys.append(((g * jax.nn.sigmoid(g)) * rn).astype(BF16))
        per_head = tm // RET_HEADS
        for t in range(h * per_head, (h + 1) * per_head):
            for kk in range(TOP_K):
                row_copy(prev, t, dsm_ref[prev, kk, t]).start(priority=kk % 2)
    y_ret = jnp.concatenate(ys, axis=1)
    branch_a = _dot(y_ret, wbr_ref[0])

    u = _dot(hb, win_ref[:, 4 * ret_width:4 * ret_width + d_model])
    top = POOL_PAD + POOL_HALO
    ext = tm + POOL_HALO
    group = d_model // len(POOL_WINDOWS)
    ue_ref[top:top + tm, :] = u
    s2 = ue_ref[POOL_PAD:POOL_PAD + ext, :] + ue_ref[POOL_PAD - 1:POOL_PAD - 1 + ext, :]
    lv_ref[0, POOL_PAD:POOL_PAD + ext, :] = s2
    s4 = s2[:, group:] + lv_ref[0, POOL_PAD - 2:POOL_PAD - 2 + ext, group:]
    lv_ref[1, POOL_PAD:POOL_PAD + ext, group:] = s4
    s8 = s4[:, group:] + lv_ref[1, POOL_PAD - 4:POOL_PAD - 4 + ext, 2 * group:]
    lv_ref[2, POOL_PAD:POOL_PAD + ext, 2 * group:] = s8
    sums = [lv_ref[0, top:top + tm, 0:group],
            lv_ref[1, top:top + tm, group:2 * group],
            lv_ref[2, top:top + tm, 2 * group:3 * group],
            lv_ref[2, top:top + tm, 3 * group:] + lv_ref[2, top - 8:top - 8 + tm, 3 * group:]]
    ue_ref[POOL_PAD:top, :] = ue_ref[tm + POOL_PAD:tm + top, :]
    pos = seq_tile * tm + lax.broadcasted_iota(I32, (tm, 1), 0)
    outs = []
    for gi, w in enumerate(POOL_WINDOWS):
        ug = u[:, gi * group:(gi + 1) * group]
        inv_count = 1.0 / jnp.minimum(pos + 1, w).astype(F32)
        mixed = (sums[gi] * inv_count - ug).astype(BF16)
        outs.append(_dot(mixed, wpool_ref[gi]))
    y_pool = (jnp.concatenate(outs, axis=1) * pscale_ref[...]).astype(BF16)
    branch_b = _dot(y_pool, wbr_ref[1])

    gates = _dot(hb, win_ref[:, 4 * ret_width + d_model:4 * ret_width + 3 * d_model])
    merged = (jax.nn.sigmoid(gates[:, 0:d_model]) * branch_a
              + jax.nn.sigmoid(gates[:, d_model:2 * d_model]) * branch_b)
    x1 = x + _dot(merged.astype(BF16), wout_ref[...])
    x1_ref[...] = x1

    h2 = _rms(x1, gffn_ref[...])
    h2_ref[...] = h2
    n_exp = br_ref.shape[0]
    h2_hi = h2.astype(BF16)
    h2_lo = (h2 - h2_hi.astype(F32)).astype(BF16)
    both = _dot(h2_hi, wr_ref[...])
    lg = both[:, 0:LANES] + both[:, LANES:2 * LANES] + _dot(h2_lo, wr_ref[:, 0:LANES])
    logits = lg.T[0:n_exp, :] + br_ref[...]
    eiota = lax.broadcasted_iota(I32, (n_exp, tm), 0)
    vals, idxs = [], []
    l = logits
    for _ in range(TOP_K):
        m = jnp.max(l, axis=0, keepdims=True)
        sel = jnp.min(jnp.where(l == m, eiota, n_exp), axis=0, keepdims=True)
        vals.append(m)
        idxs.append(sel)
        l = jnp.where(eiota == sel, -jnp.inf, l)
    exps = [jnp.exp(v - vals[0]) for v in vals]
    denom = exps[0] + exps[1] + exps[2] + exps[3]
    inv_denom = 1.0 / denom
    onehot = jnp.zeros((n_exp, tm), F32)
    for kk in range(TOP_K):
        onehot = onehot + (eiota == idxs[kk]).astype(F32)
    base = _dot(onehot.astype(BF16), tri_ref[...]) + run_ref[:, 0:1]
    tiota = lax.broadcasted_iota(I32, (1, tm), 1)
    for kk in range(TOP_K):
        idx_ref[kk:kk + 1, :] = idxs[kk]
        w_ref[kk:kk + 1, :] = exps[kk] * inv_denom
        rank = jnp.sum(jnp.where(eiota == idxs[kk], base, 0.0), axis=0, keepdims=True).astype(I32)
        rank_ref[kk:kk + 1, :] = rank
        dvm_ref[kk:kk + 1, :] = jnp.where(rank < EXPERT_CAP, idxs[kk] * EXPERT_CAP + rank,
                                          trash + kk * tm + tiota)
    trun_ref[...] = run_ref[...].astype(I32)
    run = run_ref[...] + jnp.sum(onehot, axis=1, keepdims=True)
    run_ref[...] = run
    cnt_ref[...] = run.astype(I32)

    hbuf_ref[slot] = h2
    wait_tile_rows()
    dest_copy(slot).start()

    @pl.when(i == pl.num_programs(0) - 1)
    def _():
        dest_copy(slot).wait()

        def start(j, c):
            ts = [j * ROW_UNROLL + u for u in range(ROW_UNROLL)]
            dests = [[dsm_ref[slot, kk, t] for kk in range(TOP_K)] for t in ts]
            for t, dest in zip(ts, dests):
                for kk in range(TOP_K):
                    row_copy(slot, t, dest[kk]).start(priority=kk % 2)
            return c

        lax.fori_loop(0, tm // ROW_UNROLL, start, 0)
        wait_tile_rows()

        cvm_ref[...] = run.astype(I32)
        counts_copy = pltpu.make_async_copy(cvm_ref, csm_ref, dsem)
        counts_copy.start()
        zbuf_ref[...] = jnp.zeros(zbuf_ref.shape, F32)
        counts_copy.wait()

        def zero_start(e, c):
            for cond, cp in _zero_unused_capacity(e, csm_ref[e, 0], zbuf_ref, xs_hbm, zsem):
                @pl.when(cond)
                def _():
                    cp.start()
            return c

        def zero_wait(e, c):
            for cond, cp in _zero_unused_capacity(e, csm_ref[e, 0], zbuf_ref, xs_hbm, zsem):
                @pl.when(cond)
                def _():
                    cp.wait()
            return c

        lax.fori_loop(0, N_EXPERTS, zero_start, 0)
        lax.fori_loop(0, N_EXPERTS, zero_wait, 0)


def _split_router(w_router):
    d, e = w_router.shape
    hi = w_router.astype(BF16)
    lo = (w_router - hi.astype(F32)).astype(BF16)
    pad = jnp.zeros((d, LANES - e), BF16)
    return jnp.concatenate([hi, pad, lo, pad], axis=1)


def _token_mix(x2d, cos, sin, g_mix, w_in, g_ret, w_pool, pool_scale, w_branch, w_out,
               g_ffn, w_router, b_router, seq_len):
    T, D = x2d.shape
    tm = min(MIX_TILE, seq_len)
    in_width = w_in.shape[1]
    ret_width = RET_HEADS * RET_HEAD_DIM
    assert POOL_WINDOWS == (2, 4, 8, 16)
    dmat, xi_b, zeta_b, chunk_decay = _retention_constants(tm)
    tri = jnp.asarray(np.triu(np.ones((tm, tm), np.float32), 1), BF16)
    row = lambda a: a.reshape(1, -1)
    tile_spec = lambda w: pl.BlockSpec((tm, w), lambda i: (i, 0))
    top_spec = pl.BlockSpec((TOP_K, tm), lambda i: (0, i))
    n_xs = max(N_EXPERTS * EXPERT_CAP, (T * TOP_K // EXPERT_TILE + N_EXPERTS) * EXPERT_TILE)
    body = functools.partial(_mix_body, tiles_per_seq=seq_len // tm, chunk_decay=chunk_decay, trash=n_xs)
    return pl.pallas_call(
        body,
        grid=(T // tm,),
        in_specs=[tile_spec(D), tile_spec(RET_HEAD_DIM), tile_spec(RET_HEAD_DIM),
                  _const_spec((1, D)), _const_spec((D, in_width)),
                  _const_spec(dmat.shape), _const_spec(xi_b.shape), _const_spec(zeta_b.shape),
                  _const_spec((1, ret_width)), _const_spec(w_pool.shape), _const_spec((1, D)),
                  _const_spec(w_branch.shape), _const_spec((D, D)), _const_spec((1, D)),
                  _const_spec((D, 2 * LANES)), _const_spec((N_EXPERTS, 1)), _const_spec((tm, tm))],
        out_specs=[tile_spec(D), tile_spec(D), top_spec, top_spec, top_spec,
                   pl.BlockSpec((N_EXPERTS, LANES), lambda i: (0, 0)),
                   pl.BlockSpec((None, N_EXPERTS, LANES), lambda i: (i, 0, 0)),
                   pl.BlockSpec(memory_space=pl.ANY)],
        out_shape=[jax.ShapeDtypeStruct((T, D), F32), jax.ShapeDtypeStruct((T, D), F32),
                   jax.ShapeDtypeStruct((TOP_K, T), I32), jax.ShapeDtypeStruct((TOP_K, T), F32),
                   jax.ShapeDtypeStruct((TOP_K, T), I32),
                   jax.ShapeDtypeStruct((N_EXPERTS, LANES), I32),
                   jax.ShapeDtypeStruct((T // tm, N_EXPERTS, LANES), I32),
                   jax.ShapeDtypeStruct((n_xs + TOP_K * tm, D), F32)],
        scratch_shapes=[pltpu.VMEM((RET_HEADS, RET_HEAD_DIM, RET_HEAD_DIM), F32),
                        pltpu.VMEM((tm + POOL_PAD + POOL_HALO, D), F32),
                        pltpu.VMEM((3, tm + POOL_PAD + POOL_HALO, D), F32),
                        pltpu.VMEM((N_EXPERTS, LANES), F32),
                        pltpu.VMEM((2, tm, D), F32), pltpu.VMEM((TOP_K, tm), I32),
                        pltpu.SMEM((2, TOP_K, tm), I32),
                        pltpu.VMEM((N_EXPERTS, LANES), I32), pltpu.SMEM((N_EXPERTS, LANES), I32),
                        pltpu.VMEM((EXPERT_TILE, D), F32),
                        pltpu.SemaphoreType.DMA, pltpu.SemaphoreType.DMA, pltpu.SemaphoreType.DMA],
        compiler_params=pltpu.CompilerParams(dimension_semantics=("arbitrary",),
                                             vmem_limit_bytes=VMEM_LIMIT_BYTES),
        name="token_mix",
    )(x2d, cos, sin, row(g_mix), w_in.astype(BF16), dmat, xi_b, zeta_b, row(g_ret),
      w_pool.astype(BF16), row(pool_scale), w_branch.astype(BF16), w_out.astype(BF16),
      row(g_ffn), _split_router(w_router), b_router.reshape(-1, 1), tri)


def _fixup_body(flag_ref, zstart_ref, dest_ref, h2_hbm, xs_in, xs_ref, hbuf_ref, zbuf_ref,
                sem, zsem, hsem):
    del xs_in
    td = hbuf_ref.shape[0]
    te = zbuf_ref.shape[0]
    i = pl.program_id(0)
    overflow = flag_ref[0] != 0

    @pl.when((i == 0) & overflow)
    def _():
        zbuf_ref[...] = jnp.zeros(zbuf_ref.shape, F32)

    def zero_copy(e):
        start = pl.multiple_of(jnp.maximum(zstart_ref[e], 0), te)
        return pltpu.make_async_copy(zbuf_ref, xs_ref.at[pl.ds(start, te)], zsem)

    @pl.when((i == 0) & overflow)
    def _():
        def start(e, c):
            @pl.when(zstart_ref[e] >= 0)
            def _():
                zero_copy(e).start()
            return c

        def wait(e, c):
            @pl.when(zstart_ref[e] >= 0)
            def _():
                zero_copy(e).wait()
            return c

        lax.fori_loop(0, zstart_ref.shape[0], start, 0)
        lax.fori_loop(0, zstart_ref.shape[0], wait, 0)

    @pl.when(overflow)
    def _():
        tile = pltpu.make_async_copy(h2_hbm.at[pl.ds(i * td, td)], hbuf_ref, hsem)
        tile.start()
        tile.wait()

        def start(j, c):
            ts = [j * ROW_UNROLL + u for u in range(ROW_UNROLL)]
            dests = [[dest_ref[t * TOP_K + kk] for kk in range(TOP_K)] for t in ts]
            for t, dest in zip(ts, dests):
                for kk in range(TOP_K):
                    pltpu.make_async_copy(hbuf_ref.at[pl.ds(t, 1)], xs_ref.at[pl.ds(dest[kk], 1)],
                                          sem).start(priority=kk % 2)
            return c

        lax.fori_loop(0, td // ROW_UNROLL, start, 0)
        for kk in range(TOP_K):
            pltpu.make_async_copy(hbuf_ref, xs_ref.at[pl.ds(0, td)], sem).wait()


def _moe_dispatch_fixup(overflow, zstart, dest_t, h2, xs):
    T, D = h2.shape
    td = min(DISP_TILE, T)
    smem_spec = pl.BlockSpec((TOP_K * td,), lambda i, *_: (i,), memory_space=pltpu.SMEM)
    return pl.pallas_call(
        _fixup_body,
        grid_spec=pltpu.PrefetchScalarGridSpec(
            num_scalar_prefetch=2,
            grid=(T // td,),
            in_specs=[smem_spec, pl.BlockSpec(memory_space=pl.ANY), pl.BlockSpec(memory_space=pl.ANY)],
            out_specs=pl.BlockSpec(memory_space=pl.ANY),
            scratch_shapes=[pltpu.VMEM((td, D), F32), pltpu.VMEM((EXPERT_TILE, D), F32),
                            pltpu.SemaphoreType.DMA, pltpu.SemaphoreType.DMA, pltpu.SemaphoreType.DMA]),
        out_shape=jax.ShapeDtypeStruct(xs.shape, xs.dtype),
        input_output_aliases={4: 0},
        compiler_params=pltpu.CompilerParams(dimension_semantics=("arbitrary",),
                                             vmem_limit_bytes=VMEM_LIMIT_BYTES),
        name="moe_dispatch_fixup",
    )(overflow, zstart, dest_t.T.reshape(-1), h2, xs)


def _expert_body(blk_e_ref, blk_src_ref, nused_ref, first_ref, next_e_ref, xs_ref, wgu_hbm, bgu_ref,
                 wd_hbm, bd_ref, yb_ref, wgu_f32_ref, wd_f32_ref, wgu_bf_ref, wd_bf_ref, wsem):
    i = pl.program_id(0)
    d_ff = wd_bf_ref.shape[0]

    def weight_copies(e):
        return (pltpu.make_async_copy(wgu_hbm.at[pl.ds(e, 1)], wgu_f32_ref, wsem.at[0]),
                pltpu.make_async_copy(wd_hbm.at[pl.ds(e, 1)], wd_f32_ref, wsem.at[1]))

    @pl.when(i == 0)
    def _():
        for cp in weight_copies(blk_e_ref[0]):
            cp.start()

    @pl.when(first_ref[i] != 0)
    def _():
        for cp in weight_copies(blk_e_ref[i]):
            cp.wait()
        wgu_bf_ref[...] = wgu_f32_ref[0].astype(BF16)
        wd_bf_ref[...] = wd_f32_ref[0].astype(BF16)

    @pl.when((first_ref[i] != 0) & (next_e_ref[i] >= 0))
    def _():
        for cp in weight_copies(next_e_ref[i]):
            cp.start()

    @pl.when(i < nused_ref[0])
    def _():
        gu = _dot(xs_ref[...].astype(BF16), wgu_bf_ref[...]) + bgu_ref[...]
        gate = jnp.minimum(gu[:, 0:d_ff], SWIGLU_LIMIT)
        up = jnp.clip(gu[:, d_ff:2 * d_ff], -SWIGLU_LIMIT, SWIGLU_LIMIT)
        act = (up + 1.0) * (gate * jax.nn.sigmoid(gate * SWIGLU_ALPHA))
        yb_ref[...] = _dot(act.astype(BF16), wd_bf_ref[...]) + bd_ref[...]

    @pl.when(i >= nused_ref[0])
    def _():
        yb_ref[...] = jnp.zeros(yb_ref.shape, F32)


def _moe_experts(xs, blk_e, blk_src, nused, first, next_e, w_gate_up, b_gate_up, w_down, b_down):
    D = xs.shape[1]
    E, _, two_ff = w_gate_up.shape
    d_ff = two_ff // 2
    te = EXPERT_TILE
    n_rows = blk_e.shape[0] * te
    return pl.pallas_call(
        _expert_body,
        grid_spec=pltpu.PrefetchScalarGridSpec(
            num_scalar_prefetch=5,
            grid=(n_rows // te,),
            in_specs=[pl.BlockSpec((te, D), lambda i, be, bs, *_: (bs[i], 0)),
                      pl.BlockSpec(memory_space=pl.ANY),
                      pl.BlockSpec((None, 1, two_ff), lambda i, be, *_: (be[i], 0, 0)),
                      pl.BlockSpec(memory_space=pl.ANY),
                      pl.BlockSpec((None, 1, D), lambda i, be, *_: (be[i], 0, 0))],
            out_specs=pl.BlockSpec((te, D), lambda i, *_: (i, 0)),
            scratch_shapes=[pltpu.VMEM((1, D, two_ff), F32), pltpu.VMEM((1, d_ff, D), F32),
                            pltpu.VMEM((D, two_ff), BF16), pltpu.VMEM((d_ff, D), BF16),
                            pltpu.SemaphoreType.DMA((2,))]),
        out_shape=jax.ShapeDtypeStruct((n_rows, D), F32),
        compiler_params=pltpu.CompilerParams(dimension_semantics=("arbitrary",),
                                             vmem_limit_bytes=VMEM_LIMIT_BYTES),
        name="moe_experts",
    )(blk_e, blk_src, nused, first, next_e, xs, w_gate_up, b_gate_up.reshape(E, 1, two_ff),
      w_down, b_down.reshape(E, 1, D))


def _comb_body(ws_ref, ok_ref, dest_ref, destn_ref, yb_ref, pos_ref, w_ref, x1_ref, p_ref,
               gple_ref, wpg_ref, wpp_ref, gfin_ref, out_ref, gbuf_ref, sems):
    tc = x1_ref.shape[0]
    n_rows = gbuf_ref.shape[1]
    i = pl.program_id(0)
    slot = i % 2

    def window_copy(tile, e, s):
        src = pl.multiple_of(ws_ref[tile * N_EXPERTS + e], SUBLANES)
        return pltpu.make_async_copy(yb_ref.at[pl.ds(src, COMB_WINDOW)],
                                     gbuf_ref.at[s, pl.ds(e * COMB_WINDOW, COMB_WINDOW)], sems.at[s])

    def issue(tile, dest_r, s):
        @pl.when(ok_ref[tile] != 0)
        def _():
            for e in range(N_EXPERTS):
                window_copy(tile, e, s).start(priority=e % 2)

        @pl.when(ok_ref[tile] == 0)
        def _():
            def start(j, c):
                ts = [j * ROW_UNROLL + u for u in range(ROW_UNROLL)]
                srcs = [[dest_r[kk, t] for kk in range(TOP_K)] for t in ts]
                for t, src in zip(ts, srcs):
                    for kk in range(TOP_K):
                        pltpu.make_async_copy(yb_ref.at[pl.ds(src[kk], 1)],
                                              gbuf_ref.at[s, pl.ds(kk * tc + t, 1)],
                                              sems.at[s]).start(priority=kk % 2)
                return c

            lax.fori_loop(0, tc // ROW_UNROLL, start, 0)

    @pl.when(i == 0)
    def _():
        issue(0, dest_ref, 0)

    @pl.when(i + 1 < pl.num_programs(0))
    def _():
        issue(i + 1, destn_ref, 1 - slot)

    def to_cols(rows):
        pad = jnp.zeros((LANES - rows.shape[0], tc), F32)
        return jnp.concatenate([rows, pad], axis=0).T

    w_col = to_cols(w_ref[...])
    fast = ok_ref[i] != 0
    slow_pos = (lax.broadcasted_iota(I32, (TOP_K, tc), 0) * tc
                + lax.broadcasted_iota(I32, (TOP_K, tc), 1))
    pos_col = to_cols(jnp.where(fast, pos_ref[...], slow_pos).astype(F32))
    pp = _dot(p_ref[...].astype(BF16), wpp_ref[...])

    @pl.when(fast)
    def _():
        pltpu.make_async_copy(yb_ref.at[pl.ds(0, n_rows)], gbuf_ref.at[slot], sems.at[slot]).wait()

    @pl.when(jnp.logical_not(fast))
    def _():
        pltpu.make_async_copy(yb_ref.at[pl.ds(0, TOP_K * tc)], gbuf_ref.at[slot, pl.ds(0, TOP_K * tc)],
                              sems.at[slot]).wait()
        gbuf_ref[slot, pl.ds(TOP_K * tc, n_rows - TOP_K * tc), :] = jnp.zeros(
            (n_rows - TOP_K * tc, gbuf_ref.shape[2]), F32)

    ciota = lax.broadcasted_iota(I32, (tc, n_rows), 1).astype(F32)
    sel = jnp.zeros((tc, n_rows), F32)
    for kk in range(TOP_K):
        sel = jnp.where(ciota == pos_col[:, kk:kk + 1], w_col[:, kk:kk + 1], sel)
    moe = _dot(sel.astype(BF16), gbuf_ref[slot].astype(BF16))
    x2 = x1_ref[...] + moe
    h3 = _rms(x2, gple_ref[...]).astype(BF16)
    gate = jax.nn.sigmoid(_dot(h3, wpg_ref[...]))
    x3 = x2 + gate * pp
    out_ref[...] = _rms(x3, gfin_ref[...])


def _moe_combine(ws, ok, dest_t, pos_t, yb, w_t, x1, p2d, g_ple, w_ple_gate, w_ple_proj, g_final):
    T, D = x1.shape
    tc = min(COMB_TILE, T)
    n_tiles = T // tc
    ple = p2d.shape[1]
    smem_spec = pl.BlockSpec((TOP_K, tc), lambda i, *_: (0, i), memory_space=pltpu.SMEM)
    smem_next = pl.BlockSpec((TOP_K, tc), lambda i, *_: (0, jnp.minimum(i + 1, n_tiles - 1)),
                             memory_space=pltpu.SMEM)
    top_spec = pl.BlockSpec((TOP_K, tc), lambda i, *_: (0, i))
    const = lambda shape: pl.BlockSpec(shape, lambda i, *_: (0,) * len(shape),
                                       pipeline_mode=pl.Buffered(1))
    row = lambda a: a.reshape(1, -1)
    return pl.pallas_call(
        _comb_body,
        grid_spec=pltpu.PrefetchScalarGridSpec(
            num_scalar_prefetch=2,
            grid=(n_tiles,),
            in_specs=[smem_spec, smem_next,
                      pl.BlockSpec(memory_space=pl.ANY),
                      top_spec, top_spec,
                      pl.BlockSpec((tc, D), lambda i, *_: (i, 0)),
                      pl.BlockSpec((tc, ple), lambda i, *_: (i, 0)),
                      const((1, D)), const((D, D)), const((ple, D)), const((1, D))],
            out_specs=pl.BlockSpec((tc, D), lambda i, *_: (i, 0)),
            scratch_shapes=[pltpu.VMEM((2, N_EXPERTS * COMB_WINDOW, D), F32),
                            pltpu.SemaphoreType.DMA((2,))]),
        out_shape=jax.ShapeDtypeStruct((T, D), F32),
        compiler_params=pltpu.CompilerParams(dimension_semantics=("arbitrary",),
                                             vmem_limit_bytes=VMEM_LIMIT_BYTES),
        name="moe_combine",
    )(ws, ok, dest_t, dest_t, yb, pos_t, w_t, x1, p2d, row(g_ple), w_ple_gate.astype(BF16),
      w_ple_proj.astype(BF16), row(g_final))


def _combine_windows(pstart, trun, counts, idx_t, dest_t, tile, n_rows):
    n_tiles = trun.shape[0]
    tcnt = jnp.concatenate([trun[1:], counts[None, :]], axis=0) - trun
    start = pstart[None, :] + trun
    ws = jnp.minimum(start // SUBLANES * SUBLANES, n_rows - COMB_WINDOW).astype(I32)
    ok = jnp.all(start + tcnt - ws <= COMB_WINDOW, axis=1).astype(I32)
    ws_tok = _select_expert(jnp.repeat(ws, tile, axis=0)[None], idx_t)
    pos_t = idx_t * COMB_WINDOW + dest_t - ws_tok
    return ws.reshape(-1), ok, pos_t.astype(I32)


def _select_expert(table, idx_t):
    onehot = idx_t[:, :, None] == jnp.arange(N_EXPERTS, dtype=I32)
    return jnp.sum(jnp.where(onehot, table, 0), axis=-1)


def _group_layout(counts, n_blocks):
    te = EXPERT_TILE
    padded = (counts + te - 1) // te * te
    pends = jnp.cumsum(padded)
    pstart = (pends - padded).astype(I32)
    nused = (pends[-1] // te).astype(I32)
    tail = nused + jnp.arange(N_EXPERTS, dtype=I32)
    zstart = jnp.concatenate([jnp.where(padded > 0, pends - te, -1),
                              jnp.where(tail < n_blocks, tail * te, -1)]).astype(I32)
    blk_src = jnp.minimum(jnp.arange(n_blocks, dtype=I32), jnp.maximum(nused - 1, 0))
    blk_e = jnp.sum((blk_src * te)[:, None] >= pends[None, :], axis=1)
    blk_e = jnp.clip(blk_e, 0, N_EXPERTS - 1).astype(I32)
    blk = jnp.arange(n_blocks, dtype=I32)
    prev_e = jnp.concatenate([blk_e[:1], blk_e[:-1]])
    first = ((blk < nused) & ((blk == 0) | (blk_e != prev_e))).astype(I32)
    experts = jnp.arange(N_EXPERTS, dtype=I32)[None, :]
    later = (experts > blk_e[:, None]) & (counts[None, :] > 0)
    next_e = jnp.min(jnp.where(later, experts, N_EXPERTS), axis=1)
    next_e = jnp.where(next_e < N_EXPERTS, next_e, -1).astype(I32)
    overflow = jnp.any(counts > EXPERT_CAP)
    pstart_blk = jnp.sum(jnp.where(blk_e[:, None] == experts, pstart[None, :], 0), axis=1)
    blk_cap = blk_e * (EXPERT_CAP // te) + (blk_src - pstart_blk // te)
    blk_src = jnp.where(overflow, blk_src, blk_cap).astype(I32)
    return (pstart, zstart, blk_e, blk_src, nused.reshape(1), first, next_e,
            overflow.astype(I32).reshape(1))


def kernel(x, p, positions, g_mix_norm, w_in, g_ret_norm, w_pool, pool_scale, w_branch, w_out,
           g_ffn_norm, w_router, b_router, w_gate_up, b_gate_up, w_down, b_down,
           g_ple_norm, w_ple_gate, w_ple_proj, g_final):
    B, S, D = x.shape
    depth = w_in.shape[0]
    T = B * S
    xt = x.reshape(T, D)
    cos, sin = _rope_tables(positions.reshape(T, 1))
    n_blocks = (T * TOP_K) // EXPERT_TILE + N_EXPERTS
    for i in range(depth):
        x1, h2, idx_t, w_t, rank_t, cnt, trun, xs = _token_mix(
            xt, cos, sin, g_mix_norm[i], w_in[i], g_ret_norm[i], w_pool[i], pool_scale[i],
            w_branch[i], w_out[i], g_ffn_norm[i], w_router[i], b_router[i], S)
        (pstart, zstart, blk_e, blk_src, nused, first, next_e,
         overflow) = _group_layout(cnt[:, 0], n_blocks)
        dest_t = _select_expert(pstart[None, None, :], idx_t) + rank_t
        xs = _moe_dispatch_fixup(overflow, zstart, dest_t, h2, xs)
        yb = _moe_experts(xs, blk_e, blk_src, nused, first, next_e,
                          w_gate_up[i], b_gate_up[i], w_down[i], b_down[i])
        assert depth == 1
        assert min(MIX_TILE, S) == min(COMB_TILE, T)
        ws, ok, pos_t = _combine_windows(pstart, trun[:, :, 0], cnt[:, 0], idx_t, dest_t,
                                         min(COMB_TILE, T), n_blocks * EXPERT_TILE)
        xt = _moe_combine(ws, ok, dest_t, pos_t, yb, w_t, x1, p[i].reshape(T, -1), g_ple_norm[i],
                          w_ple_gate[i], w_ple_proj[i], g_final)
    return xt.reshape(B, S, D)
```

```python
import functools

import numpy as np
import jax
import jax.numpy as jnp
from jax import lax
from jax.experimental import pallas as pl
from jax.experimental.pallas import tpu as pltpu

F32 = jnp.float32
BF16 = jnp.bfloat16
I32 = jnp.int32

RET_HEADS = 8
RET_HEAD_DIM = 128
ROPE_BASE = 10000.0
GN_EPS = 1e-5
RMS_EPS = 1e-6
POOL_WINDOWS = (2, 4, 8, 16)
N_EXPERTS = 32
TOP_K = 4
SWIGLU_ALPHA = 1.702
SWIGLU_LIMIT = 7.0

LANES = 128
SUBLANES = 8
VMEM_LIMIT_BYTES = 56 * 1024 * 1024

MIX_TILE = 256
ROPE_TILE = 1024
DISP_TILE = 2048
EXPERT_CAP = 2816
EXPERT_TILE = 256
COMB_TILE = 256
POOL_HALO = 16
POOL_PAD = 8
COMB_WINDOW = 64
ROW_UNROLL = 8


def _const_spec(shape):
    nd = len(shape)
    return pl.BlockSpec(shape, lambda *_: (0,) * nd, pipeline_mode=pl.Buffered(1))


def _zero_unused_capacity(e, cnt, zbuf_ref, xs_ref, zsem):
    te = zbuf_ref.shape[0]
    cnt = jnp.minimum(cnt, EXPERT_CAP)
    first = e * EXPERT_CAP + cnt
    n = (-cnt) % te
    head = jnp.minimum(n, (-first) % SUBLANES)
    out = []
    for r in range(SUBLANES - 1):
        out.append((r < head, pltpu.make_async_copy(zbuf_ref.at[pl.ds(0, 1)],
                                                    xs_ref.at[pl.ds(first + r, 1)], zsem)))
    rest = n - head
    pos = first + head
    size = te // 2
    while size >= SUBLANES:
        at = pl.multiple_of(pos + (rest & ~(2 * size - 1)), SUBLANES)
        out.append(((rest & size) != 0,
                    pltpu.make_async_copy(zbuf_ref.at[pl.ds(0, size)], xs_ref.at[pl.ds(at, size)], zsem)))
        size //= 2
    used_end = first + n
    for b in range(EXPERT_CAP // te):
        at = pl.multiple_of(used_end + b * te, te)
        out.append((at < (e + 1) * EXPERT_CAP,
                    pltpu.make_async_copy(zbuf_ref, xs_ref.at[pl.ds(at, te)], zsem)))
    return out


def _rms(x, g):
    return x * lax.rsqrt(jnp.mean(x * x, axis=-1, keepdims=True) + RMS_EPS) * g


def _dot(a, b):
    return jnp.dot(a, b, preferred_element_type=F32)


def _dot_nt(a, b, precision=None):
    return lax.dot_general(a, b, (((1,), (1,)), ((), ())),
                           preferred_element_type=F32, precision=precision)


def _dot_tn(a, b):
    return lax.dot_general(a, b, (((0,), (0,)), ((), ())), preferred_element_type=F32)


def _rope_body(pos_ref, inv_ref, sign_ref, cos_ref, sin_ref):
    ang = pos_ref[...].astype(F32) * inv_ref[...]
    cos_ref[...] = jnp.cos(ang)
    sin_ref[...] = jnp.sin(ang) * sign_ref[...]


def _rope_tables(pos_col):
    T = pos_col.shape[0]
    half = RET_HEAD_DIM // 2
    inv = ROPE_BASE ** (-jnp.arange(half, dtype=F32) / half)
    inv_full = jnp.concatenate([inv, inv]).reshape(1, RET_HEAD_DIM)
    sign = jnp.concatenate([-jnp.ones((half,), F32), jnp.ones((half,), F32)]).reshape(1, RET_HEAD_DIM)
    tile = min(ROPE_TILE, T)
    return pl.pallas_call(
        _rope_body,
        grid=(T // tile,),
        in_specs=[pl.BlockSpec((tile, 1), lambda i: (i, 0)),
                  pl.BlockSpec((1, RET_HEAD_DIM), lambda i: (0, 0)),
                  pl.BlockSpec((1, RET_HEAD_DIM), lambda i: (0, 0))],
        out_specs=[pl.BlockSpec((tile, RET_HEAD_DIM), lambda i: (i, 0)),
                   pl.BlockSpec((tile, RET_HEAD_DIM), lambda i: (i, 0))],
        out_shape=[jax.ShapeDtypeStruct((T, RET_HEAD_DIM), F32),
                   jax.ShapeDtypeStruct((T, RET_HEAD_DIM), F32)],
        name="rope_tables",
    )(pos_col, inv_full, sign)


def _retention_constants(tile):
    h = np.arange(RET_HEADS, dtype=np.float64)
    log_gamma = np.log1p(-np.exp2(-5.0 - h))
    idx = np.arange(tile, dtype=np.float64)
    diff = idx[:, None] - idx[None, :]
    dmat = np.where(diff >= 0, np.exp(log_gamma[:, None, None] * np.maximum(diff, 0.0)[None]), 0.0)
    xi = np.exp(log_gamma[:, None] * (idx + 1.0)[None])
    zeta = np.exp(log_gamma[:, None] * (tile - 1.0 - idx)[None])
    chunk_decay = np.exp(log_gamma * tile)
    xi_b = np.broadcast_to(xi[:, :, None], (RET_HEADS, tile, RET_HEAD_DIM))
    zeta_b = np.broadcast_to(zeta[:, :, None], (RET_HEADS, tile, RET_HEAD_DIM))
    return (jnp.asarray(dmat, F32), jnp.asarray(xi_b, F32), jnp.asarray(zeta_b, F32),
            tuple(float(c) for c in chunk_decay))


def _mix_body(x_ref, cos_ref, sin_ref, gmix_ref, win_ref, dmat_ref, xi_ref, zeta_ref, gret_ref,
              wpool_ref, pscale_ref, wbr_ref, wout_ref, gffn_ref, wr_ref, br_ref, tri_ref,
              x1_ref, h2_ref, idx_ref, w_ref, rank_ref, cnt_ref, trun_ref, xs_hbm,
              state_ref, ue_ref, lv_ref, run_ref, hbuf_ref, dvm_ref, dsm_ref, cvm_ref, csm_ref, zbuf_ref,
              rsem, dsem, zsem, *, tiles_per_seq, chunk_decay, trash):
    tm, d_model = x_ref.shape
    ret_width = RET_HEADS * RET_HEAD_DIM
    i = pl.program_id(0)
    seq_tile = i % tiles_per_seq
    slot = i % 2
    prev = 1 - slot

    def row_copy(src_slot, t, dest):
        return pltpu.make_async_copy(hbuf_ref.at[src_slot, pl.ds(t, 1)], xs_hbm.at[pl.ds(dest, 1)], rsem)

    def wait_tile_rows():
        for _ in range(TOP_K):
            pltpu.make_async_copy(hbuf_ref.at[0], xs_hbm.at[pl.ds(0, tm)], rsem).wait()

    def dest_copy(s_):
        return pltpu.make_async_copy(dvm_ref, dsm_ref.at[s_], dsem)

    @pl.when(seq_tile == 0)
    def _():
        state_ref[...] = jnp.zeros(state_ref.shape, F32)
        ue_ref[0:POOL_PAD + POOL_HALO, :] = jnp.zeros((POOL_PAD + POOL_HALO, ue_ref.shape[1]), F32)

    @pl.when(i == 0)
    def _():
        run_ref[...] = jnp.zeros(run_ref.shape, F32)
        lv_ref[:, 0:POOL_PAD, :] = jnp.zeros((lv_ref.shape[0], POOL_PAD, lv_ref.shape[2]), F32)
        hbuf_ref[1] = jnp.zeros(hbuf_ref.shape[1:], F32)
        for kk in range(TOP_K):
            dvm_ref[kk:kk + 1, :] = trash + kk * tm + lax.broadcasted_iota(I32, (1, tm), 1)
        dest_copy(1).start()

    dest_copy(prev).wait()

    x = x_ref[...]
    hb = _rms(x, gmix_ref[...]).astype(BF16)

    qkvg = _dot(hb, win_ref[:, 0:4 * ret_width])
    cos = cos_ref[...]
    sin = sin_ref[...]

    def rot(a):
        return a * cos + pltpu.roll(a, RET_HEAD_DIM // 2, 1) * sin

    ys = []
    for h in range(RET_HEADS):
        lo = h * RET_HEAD_DIM
        q = rot(qkvg[:, lo:lo + RET_HEAD_DIM]).astype(BF16)
        k = (rot(qkvg[:, ret_width + lo:ret_width + lo + RET_HEAD_DIM])
             * (RET_HEAD_DIM ** -0.5)).astype(BF16)
        v = qkvg[:, 2 * ret_width + lo:2 * ret_width + lo + RET_HEAD_DIM]
        g = qkvg[:, 3 * ret_width + lo:3 * ret_width + lo + RET_HEAD_DIM]
        scores = _dot_nt(q, k) * dmat_ref[h]
        inner = _dot(scores.astype(BF16), v.astype(BF16))
        st = state_ref[h]
        cross = _dot(q, st.astype(BF16)) * xi_ref[h]
        state_ref[h] = st * chunk_decay[h] + _dot_tn(k, (v * zeta_ref[h]).astype(BF16))
        ret = inner + cross
        mu = jnp.mean(ret, axis=-1, keepdims=True)
        dev = ret - mu
        var = jnp.mean(dev * dev, axis=-1, keepdims=True)
        rn = dev * lax.rsqrt(var + GN_EPS) * gret_ref[:, lo:lo + RET_HEAD_DIM]
        ys.append(((g * jax.nn.sigmoid(g)) * rn).astype(BF16))
        per_head = tm // RET_HEADS
        for t in range(h * per_head, (h + 1) * per_head):
            for kk in range(TOP_K):
                row_copy(prev, t, dsm_ref[prev, kk, t]).start(priority=kk % 2)
    y_ret = jnp.concatenate(ys, axis=1)
    branch_a = _dot(y_ret, wbr_ref[0])

    u = _dot(hb, win_ref[:, 4 * ret_width:4 * ret_width + d_model])
    top = POOL_PAD + POOL_HALO
    ext = tm + POOL_HALO
    group = d_model // len(POOL_WINDOWS)
    ue_ref[top:top + tm, :] = u
    s2 = ue_ref[POOL_PAD:POOL_PAD + ext, :] + ue_ref[POOL_PAD - 1:POOL_PAD - 1 + ext, :]
    lv_ref[0, POOL_PAD:POOL_PAD + ext, :] = s2
    s4 = s2[:, group:] + lv_ref[0, POOL_PAD - 2:POOL_PAD - 2 + ext, group:]
    lv_ref[1, POOL_PAD:POOL_PAD + ext, group:] = s4
    s8 = s4[:, group:] + lv_ref[1, POOL_PAD - 4:POOL_PAD - 4 + ext, 2 * group:]
    lv_ref[2, POOL_PAD:POOL_PAD + ext, 2 * group:] = s8
    sums = [lv_ref[0, top:top + tm, 0:group],
            lv_ref[1, top:top + tm, group:2 * group],
            lv_ref[2, top:top + tm, 2 * group:3 * group],
            lv_ref[2, top:top + tm, 3 * group:] + lv_ref[2, top - 8:top - 8 + tm, 3 * group:]]
    ue_ref[POOL_PAD:top, :] = ue_ref[tm + POOL_PAD:tm + top, :]
    pos = seq_tile * tm + lax.broadcasted_iota(I32, (tm, 1), 0)
    outs = []
    for gi, w in enumerate(POOL_WINDOWS):
        ug = u[:, gi * group:(gi + 1) * group]
        inv_count = 1.0 / jnp.minimum(pos + 1, w).astype(F32)
        mixed = (sums[gi] * inv_count - ug).astype(BF16)
        outs.append(_dot(mixed, wpool_ref[gi]))
    y_pool = (jnp.concatenate(outs, axis=1) * pscale_ref[...]).astype(BF16)
    branch_b = _dot(y_pool, wbr_ref[1])

    gates = _dot(hb, win_ref[:, 4 * ret_width + d_model:4 * ret_width + 3 * d_model])
    merged = (jax.nn.sigmoid(gates[:, 0:d_model]) * branch_a
              + jax.nn.sigmoid(gates[:, d_model:2 * d_model]) * branch_b)
    x1 = x + _dot(merged.astype(BF16), wout_ref[...])
    x1_ref[...] = x1

    h2 = _rms(x1, gffn_ref[...])
    h2_ref[...] = h2
    n_exp = br_ref.shape[0]
    h2_hi = h2.astype(BF16)
    h2_lo = (h2 - h2_hi.astype(F32)).astype(BF16)
    both = _dot(h2_hi, wr_ref[...])
    lg = both[:, 0:LANES] + both[:, LANES:2 * LANES] + _dot(h2_lo, wr_ref[:, 0:LANES])
    logits = lg.T[0:n_exp, :] + br_ref[...]
    eiota = lax.broadcasted_iota(I32, (n_exp, tm), 0)
    vals, idxs = [], []
    l = logits
    for _ in range(TOP_K):
        m = jnp.max(l, axis=0, keepdims=True)
        sel = jnp.min(jnp.where(l == m, eiota, n_exp), axis=0, keepdims=True)
        vals.append(m)
        idxs.append(sel)
        l = jnp.where(eiota == sel, -jnp.inf, l)
    exps = [jnp.exp(v - vals[0]) for v in vals]
    denom = exps[0] + exps[1] + exps[2] + exps[3]
    inv_denom = 1.0 / denom
    onehot = jnp.zeros((n_exp, tm), F32)
    for kk in range(TOP_K):
        onehot = onehot + (eiota == idxs[kk]).astype(F32)
    base = _dot(onehot.astype(BF16), tri_ref[...]) + run_ref[:, 0:1]
    tiota = lax.broadcasted_iota(I32, (1, tm), 1)
    for kk in range(TOP_K):
        idx_ref[kk:kk + 1, :] = idxs[kk]
        w_ref[kk:kk + 1, :] = exps[kk] * inv_denom
        rank = jnp.sum(jnp.where(eiota == idxs[kk], base, 0.0), axis=0, keepdims=True).astype(I32)
        rank_ref[kk:kk + 1, :] = rank
        dvm_ref[kk:kk + 1, :] = jnp.where(rank < EXPERT_CAP, idxs[kk] * EXPERT_CAP + rank,
                                          trash + kk * tm + tiota)
    dest_copy(slot).start()
    trun_ref[...] = run_ref[...].astype(I32)
    run = run_ref[...] + jnp.sum(onehot, axis=1, keepdims=True)
    run_ref[...] = run
    cnt_ref[...] = run.astype(I32)

    hbuf_ref[slot] = h2
    wait_tile_rows()

    @pl.when(i == pl.num_programs(0) - 1)
    def _():
        dest_copy(slot).wait()

        def start(j, c):
            ts = [j * ROW_UNROLL + u for u in range(ROW_UNROLL)]
            dests = [[dsm_ref[slot, kk, t] for kk in range(TOP_K)] for t in ts]
            for t, dest in zip(ts, dests):
                for kk in range(TOP_K):
                    row_copy(slot, t, dest[kk]).start(priority=kk % 2)
            return c

        lax.fori_loop(0, tm // ROW_UNROLL, start, 0)
        wait_tile_rows()

        cvm_ref[...] = run.astype(I32)
        counts_copy = pltpu.make_async_copy(cvm_ref, csm_ref, dsem)
        counts_copy.start()
        zbuf_ref[...] = jnp.zeros(zbuf_ref.shape, F32)
        counts_copy.wait()

        def zero_start(e, c):
            for cond, cp in _zero_unused_capacity(e, csm_ref[e, 0], zbuf_ref, xs_hbm, zsem):
                @pl.when(cond)
                def _():
                    cp.start()
            return c

        def zero_wait(e, c):
            for cond, cp in _zero_unused_capacity(e, csm_ref[e, 0], zbuf_ref, xs_hbm, zsem):
                @pl.when(cond)
                def _():
                    cp.wait()
            return c

        lax.fori_loop(0, N_EXPERTS, zero_start, 0)
        lax.fori_loop(0, N_EXPERTS, zero_wait, 0)


def _split_router(w_router):
    d, e = w_router.shape
    hi = w_router.astype(BF16)
    lo = (w_router - hi.astype(F32)).astype(BF16)
    pad = jnp.zeros((d, LANES - e), BF16)
    return jnp.concatenate([hi, pad, lo, pad], axis=1)


def _token_mix(x2d, cos, sin, g_mix, w_in, g_ret, w_pool, pool_scale, w_branch, w_out,
               g_ffn, w_router, b_router, seq_len):
    T, D = x2d.shape
    tm = min(MIX_TILE, seq_len)
    in_width = w_in.shape[1]
    ret_width = RET_HEADS * RET_HEAD_DIM
    assert POOL_WINDOWS == (2, 4, 8, 16)
    dmat, xi_b, zeta_b, chunk_decay = _retention_constants(tm)
    tri = jnp.asarray(np.triu(np.ones((tm, tm), np.float32), 1), BF16)
    row = lambda a: a.reshape(1, -1)
    tile_spec = lambda w: pl.BlockSpec((tm, w), lambda i: (i, 0))
    top_spec = pl.BlockSpec((TOP_K, tm), lambda i: (0, i))
    n_xs = max(N_EXPERTS * EXPERT_CAP, (T * TOP_K // EXPERT_TILE + N_EXPERTS) * EXPERT_TILE)
    body = functools.partial(_mix_body, tiles_per_seq=seq_len // tm, chunk_decay=chunk_decay, trash=n_xs)
    return pl.pallas_call(
        body,
        grid=(T // tm,),
        in_specs=[tile_spec(D), tile_spec(RET_HEAD_DIM), tile_spec(RET_HEAD_DIM),
                  _const_spec((1, D)), _const_spec((D, in_width)),
                  _const_spec(dmat.shape), _const_spec(xi_b.shape), _const_spec(zeta_b.shape),
                  _const_spec((1, ret_width)), _const_spec(w_pool.shape), _const_spec((1, D)),
                  _const_spec(w_branch.shape), _const_spec((D, D)), _const_spec((1, D)),
                  _const_spec((D, 2 * LANES)), _const_spec((N_EXPERTS, 1)), _const_spec((tm, tm))],
        out_specs=[tile_spec(D), tile_spec(D), top_spec, top_spec, top_spec,
                   pl.BlockSpec((N_EXPERTS, LANES), lambda i: (0, 0)),
                   pl.BlockSpec((None, N_EXPERTS, LANES), lambda i: (i, 0, 0)),
                   pl.BlockSpec(memory_space=pl.ANY)],
        out_shape=[jax.ShapeDtypeStruct((T, D), F32), jax.ShapeDtypeStruct((T, D), F32),
                   jax.ShapeDtypeStruct((TOP_K, T), I32), jax.ShapeDtypeStruct((TOP_K, T), F32),
                   jax.ShapeDtypeStruct((TOP_K, T), I32),
                   jax.ShapeDtypeStruct((N_EXPERTS, LANES), I32),
                   jax.ShapeDtypeStruct((T // tm, N_EXPERTS, LANES), I32),
                   jax.ShapeDtypeStruct((n_xs + TOP_K * tm, D), F32)],
        scratch_shapes=[pltpu.VMEM((RET_HEADS, RET_HEAD_DIM, RET_HEAD_DIM), F32),
                        pltpu.VMEM((tm + POOL_PAD + POOL_HALO, D), F32),
                        pltpu.VMEM((3, tm + POOL_PAD + POOL_HALO, D), F32),
                        pltpu.VMEM((N_EXPERTS, LANES), F32),
                        pltpu.VMEM((2, tm, D), F32), pltpu.VMEM((TOP_K, tm), I32),
                        pltpu.SMEM((2, TOP_K, tm), I32),
                        pltpu.VMEM((N_EXPERTS, LANES), I32), pltpu.SMEM((N_EXPERTS, LANES), I32),
                        pltpu.VMEM((EXPERT_TILE, D), F32),
                        pltpu.SemaphoreType.DMA, pltpu.SemaphoreType.DMA, pltpu.SemaphoreType.DMA],
        compiler_params=pltpu.CompilerParams(dimension_semantics=("arbitrary",),
                                             vmem_limit_bytes=VMEM_LIMIT_BYTES),
        name="token_mix",
    )(x2d, cos, sin, row(g_mix), w_in.astype(BF16), dmat, xi_b, zeta_b, row(g_ret),
      w_pool.astype(BF16), row(pool_scale), w_branch.astype(BF16), w_out.astype(BF16),
      row(g_ffn), _split_router(w_router), b_router.reshape(-1, 1), tri)


def _fixup_body(flag_ref, zstart_ref, dest_ref, h2_hbm, xs_in, xs_ref, hbuf_ref, zbuf_ref,
                sem, zsem, hsem):
    del xs_in
    td = hbuf_ref.shape[0]
    te = zbuf_ref.shape[0]
    i = pl.program_id(0)
    overflow = flag_ref[0] != 0

    @pl.when((i == 0) & overflow)
    def _():
        zbuf_ref[...] = jnp.zeros(zbuf_ref.shape, F32)

    def zero_copy(e):
        start = pl.multiple_of(jnp.maximum(zstart_ref[e], 0), te)
        return pltpu.make_async_copy(zbuf_ref, xs_ref.at[pl.ds(start, te)], zsem)

    @pl.when((i == 0) & overflow)
    def _():
        def start(e, c):
            @pl.when(zstart_ref[e] >= 0)
            def _():
                zero_copy(e).start()
            return c

        def wait(e, c):
            @pl.when(zstart_ref[e] >= 0)
            def _():
                zero_copy(e).wait()
            return c

        lax.fori_loop(0, zstart_ref.shape[0], start, 0)
        lax.fori_loop(0, zstart_ref.shape[0], wait, 0)

    @pl.when(overflow)
    def _():
        tile = pltpu.make_async_copy(h2_hbm.at[pl.ds(i * td, td)], hbuf_ref, hsem)
        tile.start()
        tile.wait()

        def start(j, c):
            ts = [j * ROW_UNROLL + u for u in range(ROW_UNROLL)]
            dests = [[dest_ref[kk, t] for kk in range(TOP_K)] for t in ts]
            for t, dest in zip(ts, dests):
                for kk in range(TOP_K):
                    pltpu.make_async_copy(hbuf_ref.at[pl.ds(t, 1)], xs_ref.at[pl.ds(dest[kk], 1)],
                                          sem).start(priority=kk % 2)
            return c

        lax.fori_loop(0, td // ROW_UNROLL, start, 0)
        for kk in range(TOP_K):
            pltpu.make_async_copy(hbuf_ref, xs_ref.at[pl.ds(0, td)], sem).wait()


def _moe_dispatch_fixup(overflow, zstart, dest_t, h2, xs):
    T, D = h2.shape
    td = min(DISP_TILE, T)
    smem_spec = pl.BlockSpec((TOP_K, td), lambda i, *_: (0, i), memory_space=pltpu.SMEM)
    return pl.pallas_call(
        _fixup_body,
        grid_spec=pltpu.PrefetchScalarGridSpec(
            num_scalar_prefetch=2,
            grid=(T // td,),
            in_specs=[smem_spec, pl.BlockSpec(memory_space=pl.ANY), pl.BlockSpec(memory_space=pl.ANY)],
            out_specs=pl.BlockSpec(memory_space=pl.ANY),
            scratch_shapes=[pltpu.VMEM((td, D), F32), pltpu.VMEM((EXPERT_TILE, D), F32),
                            pltpu.SemaphoreType.DMA, pltpu.SemaphoreType.DMA, pltpu.SemaphoreType.DMA]),
        out_shape=jax.ShapeDtypeStruct(xs.shape, xs.dtype),
        input_output_aliases={4: 0},
        compiler_params=pltpu.CompilerParams(dimension_semantics=("arbitrary",),
                                             vmem_limit_bytes=VMEM_LIMIT_BYTES),
        name="moe_dispatch_fixup",
    )(overflow, zstart, dest_t, h2, xs)


def _expert_body(blk_e_ref, blk_src_ref, nused_ref, first_ref, next_e_ref, xs_ref, wgu_hbm, bgu_ref,
                 wd_hbm, bd_ref, yb_ref, wgu_f32_ref, wd_f32_ref, wgu_bf_ref, wd_bf_ref, wsem):
    i = pl.program_id(0)
    d_ff = wd_bf_ref.shape[0]

    def weight_copies(e):
        return (pltpu.make_async_copy(wgu_hbm.at[pl.ds(e, 1)], wgu_f32_ref, wsem.at[0]),
                pltpu.make_async_copy(wd_hbm.at[pl.ds(e, 1)], wd_f32_ref, wsem.at[1]))

    @pl.when(i == 0)
    def _():
        for cp in weight_copies(blk_e_ref[0]):
            cp.start()

    @pl.when(first_ref[i] != 0)
    def _():
        for cp in weight_copies(blk_e_ref[i]):
            cp.wait()
        wgu_bf_ref[...] = wgu_f32_ref[0].astype(BF16)
        wd_bf_ref[...] = wd_f32_ref[0].astype(BF16)

    @pl.when((first_ref[i] != 0) & (next_e_ref[i] >= 0))
    def _():
        for cp in weight_copies(next_e_ref[i]):
            cp.start(priority=1)

    @pl.when(i < nused_ref[0])
    def _():
        gu = _dot(xs_ref[...].astype(BF16), wgu_bf_ref[...]) + bgu_ref[...]
        gate = jnp.minimum(gu[:, 0:d_ff], SWIGLU_LIMIT)
        up = jnp.clip(gu[:, d_ff:2 * d_ff], -SWIGLU_LIMIT, SWIGLU_LIMIT)
        act = (up + 1.0) * (gate * jax.nn.sigmoid(gate * SWIGLU_ALPHA))
        yb_ref[...] = _dot(act.astype(BF16), wd_bf_ref[...]) + bd_ref[...]

    @pl.when(i >= nused_ref[0])
    def _():
        yb_ref[...] = jnp.zeros(yb_ref.shape, F32)


def _moe_experts(xs, blk_e, blk_src, nused, first, next_e, w_gate_up, b_gate_up, w_down, b_down):
    D = xs.shape[1]
    E, _, two_ff = w_gate_up.shape
    d_ff = two_ff // 2
    te = EXPERT_TILE
    n_rows = blk_e.shape[0] * te
    return pl.pallas_call(
        _expert_body,
        grid_spec=pltpu.PrefetchScalarGridSpec(
            num_scalar_prefetch=5,
            grid=(n_rows // te,),
            in_specs=[pl.BlockSpec((te, D), lambda i, be, bs, *_: (bs[i], 0)),
                      pl.BlockSpec(memory_space=pl.ANY),
                      pl.BlockSpec((None, 1, two_ff), lambda i, be, *_: (be[i], 0, 0)),
                      pl.BlockSpec(memory_space=pl.ANY),
                      pl.BlockSpec((None, 1, D), lambda i, be, *_: (be[i], 0, 0))],
            out_specs=pl.BlockSpec((te, D), lambda i, *_: (i, 0)),
            scratch_shapes=[pltpu.VMEM((1, D, two_ff), F32), pltpu.VMEM((1, d_ff, D), F32),
                            pltpu.VMEM((D, two_ff), BF16), pltpu.VMEM((d_ff, D), BF16),
                            pltpu.SemaphoreType.DMA((2,))]),
        out_shape=jax.ShapeDtypeStruct((n_rows, D), F32),
        compiler_params=pltpu.CompilerParams(dimension_semantics=("arbitrary",),
                                             vmem_limit_bytes=VMEM_LIMIT_BYTES),
        name="moe_experts",
    )(blk_e, blk_src, nused, first, next_e, xs, w_gate_up, b_gate_up.reshape(E, 1, two_ff),
      w_down, b_down.reshape(E, 1, D))


def _comb_body(ws_ref, ok_ref, dest_ref, destn_ref, yb_ref, pos_ref, w_ref, x1_ref, p_ref,
               gple_ref, wpg_ref, wpp_ref, gfin_ref, out_ref, gbuf_ref, sems):
    tc = x1_ref.shape[0]
    n_rows = gbuf_ref.shape[1]
    i = pl.program_id(0)
    slot = i % 2

    def window_copy(tile, e, s):
        src = pl.multiple_of(ws_ref[tile * N_EXPERTS + e], SUBLANES)
        return pltpu.make_async_copy(yb_ref.at[pl.ds(src, COMB_WINDOW)],
                                     gbuf_ref.at[s, pl.ds(e * COMB_WINDOW, COMB_WINDOW)], sems.at[s])

    def issue(tile, dest_r, s):
        @pl.when(ok_ref[tile] != 0)
        def _():
            for e in range(N_EXPERTS):
                window_copy(tile, e, s).start(priority=e % 2)

        @pl.when(ok_ref[tile] == 0)
        def _():
            def start(j, c):
                ts = [j * ROW_UNROLL + u for u in range(ROW_UNROLL)]
                srcs = [[dest_r[kk, t] for kk in range(TOP_K)] for t in ts]
                for t, src in zip(ts, srcs):
                    for kk in range(TOP_K):
                        pltpu.make_async_copy(yb_ref.at[pl.ds(src[kk], 1)],
                                              gbuf_ref.at[s, pl.ds(kk * tc + t, 1)],
                                              sems.at[s]).start(priority=kk % 2)
                return c

            lax.fori_loop(0, tc // ROW_UNROLL, start, 0)

    @pl.when(i == 0)
    def _():
        issue(0, dest_ref, 0)

    @pl.when(i + 1 < pl.num_programs(0))
    def _():
        issue(i + 1, destn_ref, 1 - slot)

    def to_cols(rows):
        pad = jnp.zeros((LANES - rows.shape[0], tc), F32)
        return jnp.concatenate([rows, pad], axis=0).T

    w_col = to_cols(w_ref[...])
    fast = ok_ref[i] != 0
    slow_pos = (lax.broadcasted_iota(I32, (TOP_K, tc), 0) * tc
                + lax.broadcasted_iota(I32, (TOP_K, tc), 1))
    pos_col = to_cols(jnp.where(fast, pos_ref[...], slow_pos).astype(F32))
    pp = _dot(p_ref[...].astype(BF16), wpp_ref[...])

    @pl.when(fast)
    def _():
        pltpu.make_async_copy(yb_ref.at[pl.ds(0, n_rows)], gbuf_ref.at[slot], sems.at[slot]).wait()

    @pl.when(jnp.logical_not(fast))
    def _():
        pltpu.make_async_copy(yb_ref.at[pl.ds(0, TOP_K * tc)], gbuf_ref.at[slot, pl.ds(0, TOP_K * tc)],
                              sems.at[slot]).wait()
        gbuf_ref[slot, pl.ds(TOP_K * tc, n_rows - TOP_K * tc), :] = jnp.zeros(
            (n_rows - TOP_K * tc, gbuf_ref.shape[2]), F32)

    ciota = lax.broadcasted_iota(I32, (tc, n_rows), 1).astype(F32)
    sel = jnp.zeros((tc, n_rows), F32)
    for kk in range(TOP_K):
        sel = jnp.where(ciota == pos_col[:, kk:kk + 1], w_col[:, kk:kk + 1], sel)
    moe = _dot(sel.astype(BF16), gbuf_ref[slot].astype(BF16))
    x2 = x1_ref[...] + moe
    h3 = _rms(x2, gple_ref[...]).astype(BF16)
    gate = jax.nn.sigmoid(_dot(h3, wpg_ref[...]))
    x3 = x2 + gate * pp
    out_ref[...] = _rms(x3, gfin_ref[...])


def _moe_combine(ws, ok, dest_t, pos_t, yb, w_t, x1, p2d, g_ple, w_ple_gate, w_ple_proj, g_final):
    T, D = x1.shape
    tc = min(COMB_TILE, T)
    n_tiles = T // tc
    ple = p2d.shape[1]
    smem_spec = pl.BlockSpec((TOP_K, tc), lambda i, *_: (0, i), memory_space=pltpu.SMEM)
    smem_next = pl.BlockSpec((TOP_K, tc), lambda i, *_: (0, jnp.minimum(i + 1, n_tiles - 1)),
                             memory_space=pltpu.SMEM)
    top_spec = pl.BlockSpec((TOP_K, tc), lambda i, *_: (0, i))
    const = lambda shape: pl.BlockSpec(shape, lambda i, *_: (0,) * len(shape),
                                       pipeline_mode=pl.Buffered(1))
    row = lambda a: a.reshape(1, -1)
    return pl.pallas_call(
        _comb_body,
        grid_spec=pltpu.PrefetchScalarGridSpec(
            num_scalar_prefetch=2,
            grid=(n_tiles,),
            in_specs=[smem_spec, smem_next,
                      pl.BlockSpec(memory_space=pl.ANY),
                      top_spec, top_spec,
                      pl.BlockSpec((tc, D), lambda i, *_: (i, 0)),
                      pl.BlockSpec((tc, ple), lambda i, *_: (i, 0)),
                      const((1, D)), const((D, D)), const((ple, D)), const((1, D))],
            out_specs=pl.BlockSpec((tc, D), lambda i, *_: (i, 0)),
            scratch_shapes=[pltpu.VMEM((2, N_EXPERTS * COMB_WINDOW, D), F32),
                            pltpu.SemaphoreType.DMA((2,))]),
        out_shape=jax.ShapeDtypeStruct((T, D), F32),
        compiler_params=pltpu.CompilerParams(dimension_semantics=("arbitrary",),
                                             vmem_limit_bytes=VMEM_LIMIT_BYTES),
        name="moe_combine",
    )(ws, ok, dest_t, dest_t, yb, pos_t, w_t, x1, p2d, row(g_ple), w_ple_gate.astype(BF16),
      w_ple_proj.astype(BF16), row(g_final))


def _combine_windows(pstart, trun, counts, idx_t, dest_t, tile, n_rows):
    n_tiles = trun.shape[0]
    tcnt = jnp.concatenate([trun[1:], counts[None, :]], axis=0) - trun
    start = pstart[None, :] + trun
    ws = jnp.minimum(start // SUBLANES * SUBLANES, n_rows - COMB_WINDOW).astype(I32)
    ok = jnp.all(start + tcnt - ws <= COMB_WINDOW, axis=1).astype(I32)
    ws_tok = _select_expert(jnp.repeat(ws, tile, axis=0)[None], idx_t)
    pos_t = idx_t * COMB_WINDOW + dest_t - ws_tok
    return ws.reshape(-1), ok, pos_t.astype(I32)


def _select_expert(table, idx_t):
    onehot = idx_t[:, :, None] == jnp.arange(N_EXPERTS, dtype=I32)
    return jnp.sum(jnp.where(onehot, table, 0), axis=-1)


def _group_layout(counts, n_blocks):
    te = EXPERT_TILE
    padded = (counts + te - 1) // te * te
    pends = jnp.cumsum(padded)
    pstart = (pends - padded).astype(I32)
    nused = (pends[-1] // te).astype(I32)
    tail = nused + jnp.arange(N_EXPERTS, dtype=I32)
    zstart = jnp.concatenate([jnp.where(padded > 0, pends - te, -1),
                              jnp.where(tail < n_blocks, tail * te, -1)]).astype(I32)
    blk_src = jnp.minimum(jnp.arange(n_blocks, dtype=I32), jnp.maximum(nused - 1, 0))
    blk_e = jnp.sum((blk_src * te)[:, None] >= pends[None, :], axis=1)
    blk_e = jnp.clip(blk_e, 0, N_EXPERTS - 1).astype(I32)
    blk = jnp.arange(n_blocks, dtype=I32)
    prev_e = jnp.concatenate([blk_e[:1], blk_e[:-1]])
    first = ((blk < nused) & ((blk == 0) | (blk_e != prev_e))).astype(I32)
    experts = jnp.arange(N_EXPERTS, dtype=I32)[None, :]
    later = (experts > blk_e[:, None]) & (counts[None, :] > 0)
    next_e = jnp.min(jnp.where(later, experts, N_EXPERTS), axis=1)
    next_e = jnp.where(next_e < N_EXPERTS, next_e, -1).astype(I32)
    overflow = jnp.any(counts > EXPERT_CAP)
    pstart_blk = jnp.sum(jnp.where(blk_e[:, None] == experts, pstart[None, :], 0), axis=1)
    blk_cap = blk_e * (EXPERT_CAP // te) + (blk_src - pstart_blk // te)
    blk_src = jnp.where(overflow, blk_src, blk_cap).astype(I32)
    return (pstart, zstart, blk_e, blk_src, nused.reshape(1), first, next_e,
            overflow.astype(I32).reshape(1))


def kernel(x, p, positions, g_mix_norm, w_in, g_ret_norm, w_pool, pool_scale, w_branch, w_out,
           g_ffn_norm, w_router, b_router, w_gate_up, b_gate_up, w_down, b_down,
           g_ple_norm, w_ple_gate, w_ple_proj, g_final):
    B, S, D = x.shape
    depth = w_in.shape[0]
    T = B * S
    xt = x.reshape(T, D)
    cos, sin = _rope_tables(positions.reshape(T, 1))
    n_blocks = (T * TOP_K) // EXPERT_TILE + N_EXPERTS
    for i in range(depth):
        x1, h2, idx_t, w_t, rank_t, cnt, trun, xs = _token_mix(
            xt, cos, sin, g_mix_norm[i], w_in[i], g_ret_norm[i], w_pool[i], pool_scale[i],
            w_branch[i], w_out[i], g_ffn_norm[i], w_router[i], b_router[i], S)
        (pstart, zstart, blk_e, blk_src, nused, first, next_e,
         overflow) = _group_layout(cnt[:, 0], n_blocks)
        dest_t = _select_expert(pstart[None, None, :], idx_t) + rank_t
        xs = _moe_dispatch_fixup(overflow, zstart, dest_t, h2, xs)
        yb = _moe_experts(xs, blk_e, blk_src, nused, first, next_e,
                          w_gate_up[i], b_gate_up[i], w_down[i], b_down[i])
        assert depth == 1
        assert min(MIX_TILE, S) == min(COMB_TILE, T)
        ws, ok, pos_t = _combine_windows(pstart, trun[:, :, 0], cnt[:, 0], idx_t, dest_t,
                                         min(COMB_TILE, T), n_blocks * EXPERT_TILE)
        xt = _moe_combine(ws, ok, dest_t, pos_t, yb, w_t, x1, p[i].reshape(T, -1), g_ple_norm[i],
                          w_ple_gate[i], w_ple_proj[i], g_final)
    return xt.reshape(B, S, D)
```

```python
import functools

import numpy as np
import jax
import jax.numpy as jnp
from jax import lax
from jax.experimental import pallas as pl
from jax.experimental.pallas import tpu as pltpu

F32 = jnp.float32
BF16 = jnp.bfloat16
I32 = jnp.int32

RET_HEADS = 8
RET_HEAD_DIM = 128
ROPE_BASE = 10000.0
GN_EPS = 1e-5
RMS_EPS = 1e-6
POOL_WINDOWS = (2, 4, 8, 16)
N_EXPERTS = 32
TOP_K = 4
SWIGLU_ALPHA = 1.702
SWIGLU_LIMIT = 7.0

LANES = 128
SUBLANES = 8
VMEM_LIMIT_BYTES = 56 * 1024 * 1024

MIX_TILE = 256
ROPE_TILE = 1024
DISP_TILE = 2048
EXPERT_CAP = 2560
DISPATCH_LAG_SLOTS = 3
EXPERT_TILE = 256
COMB_TILE = 256
POOL_HALO = 16
POOL_PAD = 8
COMB_WINDOW = 64
ROW_UNROLL = 8


def _const_spec(shape):
    nd = len(shape)
    return pl.BlockSpec(shape, lambda *_: (0,) * nd, pipeline_mode=pl.Buffered(1))


def _zero_unused_capacity(e, cnt, zbuf_ref, xs_ref, zsem):
    te = zbuf_ref.shape[0]
    cnt = jnp.minimum(cnt, EXPERT_CAP)
    first = e * EXPERT_CAP + cnt
    n = (-cnt) % te
    head = jnp.minimum(n, (-first) % SUBLANES)
    out = []
    for r in range(SUBLANES - 1):
        out.append((r < head, pltpu.make_async_copy(zbuf_ref.at[pl.ds(0, 1)],
                                                    xs_ref.at[pl.ds(first + r, 1)], zsem)))
    rest = n - head
    pos = first + head
    size = te // 2
    while size >= SUBLANES:
        at = pl.multiple_of(pos + (rest & ~(2 * size - 1)), SUBLANES)
        out.append(((rest & size) != 0,
                    pltpu.make_async_copy(zbuf_ref.at[pl.ds(0, size)], xs_ref.at[pl.ds(at, size)], zsem)))
        size //= 2
    used_end = first + n
    for b in range(EXPERT_CAP // te):
        at = pl.multiple_of(used_end + b * te, te)
        out.append((at < (e + 1) * EXPERT_CAP,
                    pltpu.make_async_copy(zbuf_ref, xs_ref.at[pl.ds(at, te)], zsem)))
    return out


def _rms(x, g):
    return x * lax.rsqrt(jnp.mean(x * x, axis=-1, keepdims=True) + RMS_EPS) * g


def _dot(a, b):
    return jnp.dot(a, b, preferred_element_type=F32)


def _dot_nt(a, b, precision=None):
    return lax.dot_general(a, b, (((1,), (1,)), ((), ())),
                           preferred_element_type=F32, precision=precision)


def _dot_tn(a, b):
    return lax.dot_general(a, b, (((0,), (0,)), ((), ())), preferred_element_type=F32)


def _rope_body(pos_ref, inv_ref, sign_ref, cos_ref, sin_ref):
    ang = pos_ref[...].astype(F32) * inv_ref[...]
    cos_ref[...] = jnp.cos(ang)
    sin_ref[...] = jnp.sin(ang) * sign_ref[...]


def _rope_tables(pos_col):
    T = pos_col.shape[0]
    half = RET_HEAD_DIM // 2
    inv = ROPE_BASE ** (-jnp.arange(half, dtype=F32) / half)
    inv_full = jnp.concatenate([inv, inv]).reshape(1, RET_HEAD_DIM)
    sign = jnp.concatenate([-jnp.ones((half,), F32), jnp.ones((half,), F32)]).reshape(1, RET_HEAD_DIM)
    tile = min(ROPE_TILE, T)
    return pl.pallas_call(
        _rope_body,
        grid=(T // tile,),
        in_specs=[pl.BlockSpec((tile, 1), lambda i: (i, 0)),
                  pl.BlockSpec((1, RET_HEAD_DIM), lambda i: (0, 0)),
                  pl.BlockSpec((1, RET_HEAD_DIM), lambda i: (0, 0))],
        out_specs=[pl.BlockSpec((tile, RET_HEAD_DIM), lambda i: (i, 0)),
                   pl.BlockSpec((tile, RET_HEAD_DIM), lambda i: (i, 0))],
        out_shape=[jax.ShapeDtypeStruct((T, RET_HEAD_DIM), F32),
                   jax.ShapeDtypeStruct((T, RET_HEAD_DIM), F32)],
        name="rope_tables",
    )(pos_col, inv_full, sign)


def _retention_constants(tile):
    h = np.arange(RET_HEADS, dtype=np.float64)
    log_gamma = np.log1p(-np.exp2(-5.0 - h))
    idx = np.arange(tile, dtype=np.float64)
    diff = idx[:, None] - idx[None, :]
    dmat = np.where(diff >= 0, np.exp(log_gamma[:, None, None] * np.maximum(diff, 0.0)[None]), 0.0)
    xi = np.exp(log_gamma[:, None] * (idx + 1.0)[None])
    zeta = np.exp(log_gamma[:, None] * (tile - 1.0 - idx)[None])
    chunk_decay = np.exp(log_gamma * tile)
    xi_b = np.broadcast_to(xi[:, :, None], (RET_HEADS, tile, RET_HEAD_DIM))
    zeta_b = np.broadcast_to(zeta[:, :, None], (RET_HEADS, tile, RET_HEAD_DIM))
    return (jnp.asarray(dmat, F32), jnp.asarray(xi_b, F32), jnp.asarray(zeta_b, F32),
            tuple(float(c) for c in chunk_decay))


def _mix_body(x_ref, cos_ref, sin_ref, gmix_ref, win_ref, dmat_ref, xi_ref, zeta_ref, gret_ref,
              wpool_ref, pscale_ref, wbr_ref, wout_ref, gffn_ref, wr_ref, br_ref, tri_ref,
              x1_ref, h2_ref, idx_ref, w_ref, rank_ref, cnt_ref, trun_ref, xs_hbm,
              state_ref, ue_ref, lv_ref, run_ref, hbuf_ref, dvm_ref, dsm_ref, cvm_ref, csm_ref,
              zbuf_ref, rsem, dsem, csem, zsem, *, tiles_per_seq, chunk_decay, trash):
    tm, d_model = x_ref.shape
    ret_width = RET_HEADS * RET_HEAD_DIM
    i = pl.program_id(0)
    seq_tile = i % tiles_per_seq
    slot = i % DISPATCH_LAG_SLOTS
    prev = (i + 2) % DISPATCH_LAG_SLOTS
    old = (i + 1) % DISPATCH_LAG_SLOTS

    def row_copy(src_slot, t, dest):
        return pltpu.make_async_copy(hbuf_ref.at[src_slot, pl.ds(t, 1)], xs_hbm.at[pl.ds(dest, 1)], rsem)

    def wait_tile_rows():
        for _ in range(TOP_K):
            pltpu.make_async_copy(hbuf_ref.at[0], xs_hbm.at[pl.ds(0, tm)], rsem).wait()

    def dest_copy(s_):
        return pltpu.make_async_copy(dvm_ref.at[s_], dsm_ref.at[s_], dsem.at[s_])

    @pl.when(seq_tile == 0)
    def _():
        state_ref[...] = jnp.zeros(state_ref.shape, F32)
        ue_ref[0:POOL_PAD + POOL_HALO, :] = jnp.zeros((POOL_PAD + POOL_HALO, ue_ref.shape[1]), F32)

    @pl.when(i == 0)
    def _():
        run_ref[...] = jnp.zeros(run_ref.shape, F32)
        lv_ref[:, 0:POOL_PAD, :] = jnp.zeros((lv_ref.shape[0], POOL_PAD, lv_ref.shape[2]), F32)
        hbuf_ref[1] = jnp.zeros(hbuf_ref.shape[1:], F32)
        hbuf_ref[2] = jnp.zeros(hbuf_ref.shape[1:], F32)
        for s_ in (1, 2):
            for kk in range(TOP_K):
                dvm_ref[s_, kk:kk + 1, :] = trash + kk * tm + lax.broadcasted_iota(I32, (1, tm), 1)
            dest_copy(s_).start()

    dest_copy(old).wait()

    x = x_ref[...]
    hb = _rms(x, gmix_ref[...]).astype(BF16)

    qkvg = _dot(hb, win_ref[:, 0:4 * ret_width])
    cos = cos_ref[...]
    sin = sin_ref[...]

    def rot(a):
        return a * cos + pltpu.roll(a, RET_HEAD_DIM // 2, 1) * sin

    ys = []
    for h in range(RET_HEADS):
        lo = h * RET_HEAD_DIM
        q = rot(qkvg[:, lo:lo + RET_HEAD_DIM]).astype(BF16)
        k = (rot(qkvg[:, ret_width + lo:ret_width + lo + RET_HEAD_DIM])
             * (RET_HEAD_DIM ** -0.5)).astype(BF16)
        v = qkvg[:, 2 * ret_width + lo:2 * ret_width + lo + RET_HEAD_DIM]
        g = qkvg[:, 3 * ret_width + lo:3 * ret_width + lo + RET_HEAD_DIM]
        scores = _dot_nt(q, k) * dmat_ref[h]
        inner = _dot(scores.astype(BF16), v.astype(BF16))
        st = state_ref[h]
        cross = _dot(q, st.astype(BF16)) * xi_ref[h]
        state_ref[h] = st * chunk_decay[h] + _dot_tn(k, (v * zeta_ref[h]).astype(BF16))
        ret = inner + cross
        mu = jnp.mean(ret, axis=-1, keepdims=True)
        dev = ret - mu
        var = jnp.mean(dev * dev, axis=-1, keepdims=True)
        rn = dev * lax.rsqrt(var + GN_EPS) * gret_ref[:, lo:lo + RET_HEAD_DIM]
        ys.append(((g * jax.nn.sigmoid(g)) * rn).astype(BF16))
        per_head = tm // RET_HEADS
        for t in range(h * per_head, (h + 1) * per_head):
            for kk in range(TOP_K):
                row_copy(old, t, dsm_ref[old, kk, t]).start(priority=kk % 2)
    y_ret = jnp.concatenate(ys, axis=1)
    branch_a = _dot(y_ret, wbr_ref[0])

    u = _dot(hb, win_ref[:, 4 * ret_width:4 * ret_width + d_model])
    top = POOL_PAD + POOL_HALO
    ext = tm + POOL_HALO
    group = d_model // len(POOL_WINDOWS)
    ue_ref[top:top + tm, :] = u
    s2 = ue_ref[POOL_PAD:POOL_PAD + ext, :] + ue_ref[POOL_PAD - 1:POOL_PAD - 1 + ext, :]
    lv_ref[0, POOL_PAD:POOL_PAD + ext, :] = s2
    s4 = s2[:, group:] + lv_ref[0, POOL_PAD - 2:POOL_PAD - 2 + ext, group:]
    lv_ref[1, POOL_PAD:POOL_PAD + ext, group:] = s4
    s8 = s4[:, group:] + lv_ref[1, POOL_PAD - 4:POOL_PAD - 4 + ext, 2 * group:]
    lv_ref[2, POOL_PAD:POOL_PAD + ext, 2 * group:] = s8
    sums = [lv_ref[0, top:top + tm, 0:group],
            lv_ref[1, top:top + tm, group:2 * group],
            lv_ref[2, top:top + tm, 2 * group:3 * group],
            lv_ref[2, top:top + tm, 3 * group:] + lv_ref[2, top - 8:top - 8 + tm, 3 * group:]]
    ue_ref[POOL_PAD:top, :] = ue_ref[tm + POOL_PAD:tm + top, :]
    pos = seq_tile * tm + lax.broadcasted_iota(I32, (tm, 1), 0)
    outs = []
    for gi, w in enumerate(POOL_WINDOWS):
        ug = u[:, gi * group:(gi + 1) * group]
        inv_count = 1.0 / jnp.minimum(pos + 1, w).astype(F32)
        mixed = (sums[gi] * inv_count - ug).astype(BF16)
        outs.append(_dot(mixed, wpool_ref[gi]))
    y_pool = (jnp.concatenate(outs, axis=1) * pscale_ref[...]).astype(BF16)
    branch_b = _dot(y_pool, wbr_ref[1])

    gates = _dot(hb, win_ref[:, 4 * ret_width + d_model:4 * ret_width + 3 * d_model])
    merged = (jax.nn.sigmoid(gates[:, 0:d_model]) * branch_a
              + jax.nn.sigmoid(gates[:, d_model:2 * d_model]) * branch_b)
    x1 = x + _dot(merged.astype(BF16), wout_ref[...])
    x1_ref[...] = x1

    h2 = _rms(x1, gffn_ref[...])
    h2_ref[...] = h2
    n_exp = br_ref.shape[0]
    h2_hi = h2.astype(BF16)
    h2_lo = (h2 - h2_hi.astype(F32)).astype(BF16)
    both = _dot(h2_hi, wr_ref[...])
    lg = both[:, 0:LANES] + both[:, LANES:2 * LANES] + _dot(h2_lo, wr_ref[:, 0:LANES])
    logits = lg.T[0:n_exp, :] + br_ref[...]
    eiota = lax.broadcasted_iota(I32, (n_exp, tm), 0)
    vals, idxs = [], []
    l = logits
    for _ in range(TOP_K):
        m = jnp.max(l, axis=0, keepdims=True)
        sel = jnp.min(jnp.where(l == m, eiota, n_exp), axis=0, keepdims=True)
        vals.append(m)
        idxs.append(sel)
        l = jnp.where(eiota == sel, -jnp.inf, l)
    exps = [jnp.exp(v - vals[0]) for v in vals]
    denom = exps[0] + exps[1] + exps[2] + exps[3]
    inv_denom = 1.0 / denom
    onehot = jnp.zeros((n_exp, tm), F32)
    for kk in range(TOP_K):
        onehot = onehot + (eiota == idxs[kk]).astype(F32)
    base = _dot(onehot.astype(BF16), tri_ref[...]) + run_ref[:, 0:1]
    tiota = lax.broadcasted_iota(I32, (1, tm), 1)
    for kk in range(TOP_K):
        idx_ref[kk:kk + 1, :] = idxs[kk]
        w_ref[kk:kk + 1, :] = exps[kk] * inv_denom
        rank = jnp.sum(jnp.where(eiota == idxs[kk], base, 0.0), axis=0, keepdims=True).astype(I32)
        rank_ref[kk:kk + 1, :] = rank
        dvm_ref[slot, kk:kk + 1, :] = jnp.where(rank < EXPERT_CAP, idxs[kk] * EXPERT_CAP + rank,
                                                trash + kk * tm + tiota)
    dest_copy(slot).start()
    trun_ref[...] = run_ref[...].astype(I32)
    run = run_ref[...] + jnp.sum(onehot, axis=1, keepdims=True)
    run_ref[...] = run
    cnt_ref[...] = run.astype(I32)

    hbuf_ref[slot] = h2
    wait_tile_rows()

    @pl.when(i == pl.num_programs(0) - 1)
    def _():
        for s_ in (prev, slot):
            dest_copy(s_).wait()

            def start(j, c):
                ts = [j * ROW_UNROLL + u for u in range(ROW_UNROLL)]
                dests = [[dsm_ref[s_, kk, t] for kk in range(TOP_K)] for t in ts]
                for t, dest in zip(ts, dests):
                    for kk in range(TOP_K):
                        row_copy(s_, t, dest[kk]).start(priority=kk % 2)
                return c

            lax.fori_loop(0, tm // ROW_UNROLL, start, 0)
            wait_tile_rows()

        cvm_ref[...] = run.astype(I32)
        counts_copy = pltpu.make_async_copy(cvm_ref, csm_ref, csem)
        counts_copy.start()
        zbuf_ref[...] = jnp.zeros(zbuf_ref.shape, F32)
        counts_copy.wait()

        def zero_start(e, c):
            for cond, cp in _zero_unused_capacity(e, csm_ref[e, 0], zbuf_ref, xs_hbm, zsem):
                @pl.when(cond)
                def _():
                    cp.start()
            return c

        def zero_wait(e, c):
            for cond, cp in _zero_unused_capacity(e, csm_ref[e, 0], zbuf_ref, xs_hbm, zsem):
                @pl.when(cond)
                def _():
                    cp.wait()
            return c

        lax.fori_loop(0, N_EXPERTS, zero_start, 0)
        lax.fori_loop(0, N_EXPERTS, zero_wait, 0)


def _split_router(w_router):
    d, e = w_router.shape
    hi = w_router.astype(BF16)
    lo = (w_router - hi.astype(F32)).astype(BF16)
    pad = jnp.zeros((d, LANES - e), BF16)
    return jnp.concatenate([hi, pad, lo, pad], axis=1)


def _token_mix(x2d, cos, sin, g_mix, w_in, g_ret, w_pool, pool_scale, w_branch, w_out,
               g_ffn, w_router, b_router, seq_len):
    T, D = x2d.shape
    tm = min(MIX_TILE, seq_len)
    in_width = w_in.shape[1]
    ret_width = RET_HEADS * RET_HEAD_DIM
    assert T // tm >= DISPATCH_LAG_SLOTS - 1
    assert POOL_WINDOWS == (2, 4, 8, 16)
    dmat, xi_b, zeta_b, chunk_decay = _retention_constants(tm)
    tri = jnp.asarray(np.triu(np.ones((tm, tm), np.float32), 1), BF16)
    row = lambda a: a.reshape(1, -1)
    tile_spec = lambda w: pl.BlockSpec((tm, w), lambda i: (i, 0))
    top_spec = pl.BlockSpec((TOP_K, tm), lambda i: (0, i))
    n_xs = max(N_EXPERTS * EXPERT_CAP, (T * TOP_K // EXPERT_TILE + N_EXPERTS) * EXPERT_TILE)
    body = functools.partial(_mix_body, tiles_per_seq=seq_len // tm, chunk_decay=chunk_decay, trash=n_xs)
    return pl.pallas_call(
        body,
        grid=(T // tm,),
        in_specs=[tile_spec(D), tile_spec(RET_HEAD_DIM), tile_spec(RET_HEAD_DIM),
                  _const_spec((1, D)), _const_spec((D, in_width)),
                  _const_spec(dmat.shape), _const_spec(xi_b.shape), _const_spec(zeta_b.shape),
                  _const_spec((1, ret_width)), _const_spec(w_pool.shape), _const_spec((1, D)),
                  _const_spec(w_branch.shape), _const_spec((D, D)), _const_spec((1, D)),
                  _const_spec((D, 2 * LANES)), _const_spec((N_EXPERTS, 1)), _const_spec((tm, tm))],
        out_specs=[tile_spec(D), tile_spec(D), top_spec, top_spec, top_spec,
                   pl.BlockSpec((N_EXPERTS, LANES), lambda i: (0, 0)),
                   pl.BlockSpec((None, N_EXPERTS, LANES), lambda i: (i, 0, 0)),
                   pl.BlockSpec(memory_space=pl.ANY)],
        out_shape=[jax.ShapeDtypeStruct((T, D), F32), jax.ShapeDtypeStruct((T, D), F32),
                   jax.ShapeDtypeStruct((TOP_K, T), I32), jax.ShapeDtypeStruct((TOP_K, T), F32),
                   jax.ShapeDtypeStruct((TOP_K, T), I32),
                   jax.ShapeDtypeStruct((N_EXPERTS, LANES), I32),
                   jax.ShapeDtypeStruct((T // tm, N_EXPERTS, LANES), I32),
                   jax.ShapeDtypeStruct((n_xs + TOP_K * tm, D), F32)],
        scratch_shapes=[pltpu.VMEM((RET_HEADS, RET_HEAD_DIM, RET_HEAD_DIM), F32),
                        pltpu.VMEM((tm + POOL_PAD + POOL_HALO, D), F32),
                        pltpu.VMEM((3, tm + POOL_PAD + POOL_HALO, D), F32),
                        pltpu.VMEM((N_EXPERTS, LANES), F32),
                        pltpu.VMEM((DISPATCH_LAG_SLOTS, tm, D), F32),
                        pltpu.VMEM((DISPATCH_LAG_SLOTS, TOP_K, tm), I32),
                        pltpu.SMEM((DISPATCH_LAG_SLOTS, TOP_K, tm), I32),
                        pltpu.VMEM((N_EXPERTS, LANES), I32), pltpu.SMEM((N_EXPERTS, LANES), I32),
                        pltpu.VMEM((EXPERT_TILE, D), F32),
                        pltpu.SemaphoreType.DMA, pltpu.SemaphoreType.DMA((DISPATCH_LAG_SLOTS,)),
                        pltpu.SemaphoreType.DMA, pltpu.SemaphoreType.DMA],
        compiler_params=pltpu.CompilerParams(dimension_semantics=("arbitrary",),
                                             vmem_limit_bytes=VMEM_LIMIT_BYTES),
        name="token_mix",
    )(x2d, cos, sin, row(g_mix), w_in.astype(BF16), dmat, xi_b, zeta_b, row(g_ret),
      w_pool.astype(BF16), row(pool_scale), w_branch.astype(BF16), w_out.astype(BF16),
      row(g_ffn), _split_router(w_router), b_router.reshape(-1, 1), tri)


def _fixup_body(flag_ref, zstart_ref, dest_ref, h2_hbm, xs_in, xs_ref, hbuf_ref, zbuf_ref,
                sem, zsem, hsem):
    del xs_in
    td = hbuf_ref.shape[0]
    te = zbuf_ref.shape[0]
    i = pl.program_id(0)
    overflow = flag_ref[0] != 0

    @pl.when((i == 0) & overflow)
    def _():
        zbuf_ref[...] = jnp.zeros(zbuf_ref.shape, F32)

    def zero_copy(e):
        start = pl.multiple_of(jnp.maximum(zstart_ref[e], 0), te)
        return pltpu.make_async_copy(zbuf_ref, xs_ref.at[pl.ds(start, te)], zsem)

    @pl.when((i == 0) & overflow)
    def _():
        def start(e, c):
            @pl.when(zstart_ref[e] >= 0)
            def _():
                zero_copy(e).start()
            return c

        def wait(e, c):
            @pl.when(zstart_ref[e] >= 0)
            def _():
                zero_copy(e).wait()
            return c

        lax.fori_loop(0, zstart_ref.shape[0], start, 0)
        lax.fori_loop(0, zstart_ref.shape[0], wait, 0)

    @pl.when(overflow)
    def _():
        tile = pltpu.make_async_copy(h2_hbm.at[pl.ds(i * td, td)], hbuf_ref, hsem)
        tile.start()
        tile.wait()

        def start(j, c):
            ts = [j * ROW_UNROLL + u for u in range(ROW_UNROLL)]
            dests = [[dest_ref[kk, t] for kk in range(TOP_K)] for t in ts]
            for t, dest in zip(ts, dests):
                for kk in range(TOP_K):
                    pltpu.make_async_copy(hbuf_ref.at[pl.ds(t, 1)], xs_ref.at[pl.ds(dest[kk], 1)],
                                          sem).start(priority=kk % 2)
            return c

        lax.fori_loop(0, td // ROW_UNROLL, start, 0)
        for kk in range(TOP_K):
            pltpu.make_async_copy(hbuf_ref, xs_ref.at[pl.ds(0, td)], sem).wait()


def _moe_dispatch_fixup(overflow, zstart, dest_t, h2, xs):
    T, D = h2.shape
    td = min(DISP_TILE, T)
    smem_spec = pl.BlockSpec((TOP_K, td), lambda i, *_: (0, i), memory_space=pltpu.SMEM)
    return pl.pallas_call(
        _fixup_body,
        grid_spec=pltpu.PrefetchScalarGridSpec(
            num_scalar_prefetch=2,
            grid=(T // td,),
            in_specs=[smem_spec, pl.BlockSpec(memory_space=pl.ANY), pl.BlockSpec(memory_space=pl.ANY)],
            out_specs=pl.BlockSpec(memory_space=pl.ANY),
            scratch_shapes=[pltpu.VMEM((td, D), F32), pltpu.VMEM((EXPERT_TILE, D), F32),
                            pltpu.SemaphoreType.DMA, pltpu.SemaphoreType.DMA, pltpu.SemaphoreType.DMA]),
        out_shape=jax.ShapeDtypeStruct(xs.shape, xs.dtype),
        input_output_aliases={4: 0},
        compiler_params=pltpu.CompilerParams(dimension_semantics=("arbitrary",),
                                             vmem_limit_bytes=VMEM_LIMIT_BYTES),
        name="moe_dispatch_fixup",
    )(overflow, zstart, dest_t, h2, xs)


def _expert_body(blk_e_ref, blk_src_ref, nused_ref, first_ref, next_e_ref, xs_ref, wgu_hbm, bgu_ref,
                 wd_hbm, bd_ref, yb_ref, wgu_f32_ref, wd_f32_ref, wgu_bf_ref, wd_bf_ref, wsem):
    i = pl.program_id(0)
    d_ff = wd_bf_ref.shape[0]

    def weight_copies(e):
        return (pltpu.make_async_copy(wgu_hbm.at[pl.ds(e, 1)], wgu_f32_ref, wsem.at[0]),
                pltpu.make_async_copy(wd_hbm.at[pl.ds(e, 1)], wd_f32_ref, wsem.at[1]))

    @pl.when(i == 0)
    def _():
        for cp in weight_copies(blk_e_ref[0]):
            cp.start()

    @pl.when(first_ref[i] != 0)
    def _():
        for cp in weight_copies(blk_e_ref[i]):
            cp.wait()
        wgu_bf_ref[...] = wgu_f32_ref[0].astype(BF16)
        wd_bf_ref[...] = wd_f32_ref[0].astype(BF16)

    @pl.when((first_ref[i] != 0) & (next_e_ref[i] >= 0))
    def _():
        for cp in weight_copies(next_e_ref[i]):
            cp.start(priority=1)

    @pl.when(i < nused_ref[0])
    def _():
        gu = _dot(xs_ref[...].astype(BF16), wgu_bf_ref[...]) + bgu_ref[...]
        gate = jnp.minimum(gu[:, 0:d_ff], SWIGLU_LIMIT)
        up = jnp.clip(gu[:, d_ff:2 * d_ff], -SWIGLU_LIMIT, SWIGLU_LIMIT)
        act = (up + 1.0) * (gate * jax.nn.sigmoid(gate * SWIGLU_ALPHA))
        yb_ref[...] = _dot(act.astype(BF16), wd_bf_ref[...]) + bd_ref[...]

    @pl.when(i >= nused_ref[0])
    def _():
        yb_ref[...] = jnp.zeros(yb_ref.shape, F32)


def _moe_experts(xs, blk_e, blk_src, nused, first, next_e, w_gate_up, b_gate_up, w_down, b_down):
    D = xs.shape[1]
    E, _, two_ff = w_gate_up.shape
    d_ff = two_ff // 2
    te = EXPERT_TILE
    n_rows = blk_e.shape[0] * te
    return pl.pallas_call(
        _expert_body,
        grid_spec=pltpu.PrefetchScalarGridSpec(
            num_scalar_prefetch=5,
            grid=(n_rows // te,),
            in_specs=[pl.BlockSpec((te, D), lambda i, be, bs, *_: (bs[i], 0)),
                      pl.BlockSpec(memory_space=pl.ANY),
                      pl.BlockSpec((None, 1, two_ff), lambda i, be, *_: (be[i], 0, 0)),
                      pl.BlockSpec(memory_space=pl.ANY),
                      pl.BlockSpec((None, 1, D), lambda i, be, *_: (be[i], 0, 0))],
            out_specs=pl.BlockSpec((te, D), lambda i, *_: (i, 0)),
            scratch_shapes=[pltpu.VMEM((1, D, two_ff), F32), pltpu.VMEM((1, d_ff, D), F32),
                            pltpu.VMEM((D, two_ff), BF16), pltpu.VMEM((d_ff, D), BF16),
                            pltpu.SemaphoreType.DMA((2,))]),
        out_shape=jax.ShapeDtypeStruct((n_rows, D), F32),
        compiler_params=pltpu.CompilerParams(dimension_semantics=("arbitrary",),
                                             vmem_limit_bytes=VMEM_LIMIT_BYTES),
        name="moe_experts",
    )(blk_e, blk_src, nused, first, next_e, xs, w_gate_up, b_gate_up.reshape(E, 1, two_ff),
      w_down, b_down.reshape(E, 1, D))


def _comb_body(ws_ref, ok_ref, dest_ref, destn_ref, yb_ref, pos_ref, w_ref, x1_ref, p_ref,
               gple_ref, wpg_ref, wpp_ref, gfin_ref, out_ref, gbuf_ref, sems):
    tc = x1_ref.shape[0]
    n_rows = gbuf_ref.shape[1]
    i = pl.program_id(0)
    slot = i % 2

    def window_copy(tile, e, s):
        src = pl.multiple_of(ws_ref[tile * N_EXPERTS + e], SUBLANES)
        return pltpu.make_async_copy(yb_ref.at[pl.ds(src, COMB_WINDOW)],
                                     gbuf_ref.at[s, pl.ds(e * COMB_WINDOW, COMB_WINDOW)], sems.at[s])

    def issue(tile, dest_r, s):
        @pl.when(ok_ref[tile] != 0)
        def _():
            for e in range(N_EXPERTS):
                window_copy(tile, e, s).start(priority=e % 2)

        @pl.when(ok_ref[tile] == 0)
        def _():
            def start(j, c):
                ts = [j * ROW_UNROLL + u for u in range(ROW_UNROLL)]
                srcs = [[dest_r[kk, t] for kk in range(TOP_K)] for t in ts]
                for t, src in zip(ts, srcs):
                    for kk in range(TOP_K):
                        pltpu.make_async_copy(yb_ref.at[pl.ds(src[kk], 1)],
                                              gbuf_ref.at[s, pl.ds(kk * tc + t, 1)],
                                              sems.at[s]).start(priority=kk % 2)
                return c

            lax.fori_loop(0, tc // ROW_UNROLL, start, 0)

    @pl.when(i == 0)
    def _():
        issue(0, dest_ref, 0)

    @pl.when(i + 1 < pl.num_programs(0))
    def _():
        issue(i + 1, destn_ref, 1 - slot)

    def to_cols(rows):
        pad = jnp.zeros((LANES - rows.shape[0], tc), F32)
        return jnp.concatenate([rows, pad], axis=0).T

    w_col = to_cols(w_ref[...])
    fast = ok_ref[i] != 0
    slow_pos = (lax.broadcasted_iota(I32, (TOP_K, tc), 0) * tc
                + lax.broadcasted_iota(I32, (TOP_K, tc), 1))
    pos_col = to_cols(jnp.where(fast, pos_ref[...], slow_pos).astype(F32))
    pp = _dot(p_ref[...].astype(BF16), wpp_ref[...])

    @pl.when(fast)
    def _():
        pltpu.make_async_copy(yb_ref.at[pl.ds(0, n_rows)], gbuf_ref.at[slot], sems.at[slot]).wait()

    @pl.when(jnp.logical_not(fast))
    def _():
        pltpu.make_async_copy(yb_ref.at[pl.ds(0, TOP_K * tc)], gbuf_ref.at[slot, pl.ds(0, TOP_K * tc)],
                              sems.at[slot]).wait()
        gbuf_ref[slot, pl.ds(TOP_K * tc, n_rows - TOP_K * tc), :] = jnp.zeros(
            (n_rows - TOP_K * tc, gbuf_ref.shape[2]), F32)

    ciota = lax.broadcasted_iota(I32, (tc, n_rows), 1).astype(F32)
    sel = jnp.zeros((tc, n_rows), F32)
    for kk in range(TOP_K):
        sel = jnp.where(ciota == pos_col[:, kk:kk + 1], w_col[:, kk:kk + 1], sel)
    moe = _dot(sel.astype(BF16), gbuf_ref[slot].astype(BF16))
    x2 = x1_ref[...] + moe
    h3 = _rms(x2, gple_ref[...]).astype(BF16)
    gate = jax.nn.sigmoid(_dot(h3, wpg_ref[...]))
    x3 = x2 + gate * pp
    out_ref[...] = _rms(x3, gfin_ref[...])


def _moe_combine(ws, ok, dest_t, pos_t, yb, w_t, x1, p2d, g_ple, w_ple_gate, w_ple_proj, g_final):
    T, D = x1.shape
    tc = min(COMB_TILE, T)
    n_tiles = T // tc
    ple = p2d.shape[1]
    smem_spec = pl.BlockSpec((TOP_K, tc), lambda i, *_: (0, i), memory_space=pltpu.SMEM)
    smem_next = pl.BlockSpec((TOP_K, tc), lambda i, *_: (0, jnp.minimum(i + 1, n_tiles - 1)),
                             memory_space=pltpu.SMEM)
    top_spec = pl.BlockSpec((TOP_K, tc), lambda i, *_: (0, i))
    const = lambda shape: pl.BlockSpec(shape, lambda i, *_: (0,) * len(shape),
                                       pipeline_mode=pl.Buffered(1))
    row = lambda a: a.reshape(1, -1)
    return pl.pallas_call(
        _comb_body,
        grid_spec=pltpu.PrefetchScalarGridSpec(
            num_scalar_prefetch=2,
            grid=(n_tiles,),
            in_specs=[smem_spec, smem_next,
                      pl.BlockSpec(memory_space=pl.ANY),
                      top_spec, top_spec,
                      pl.BlockSpec((tc, D), lambda i, *_: (i, 0)),
                      pl.BlockSpec((tc, ple), lambda i, *_: (i, 0)),
                      const((1, D)), const((D, D)), const((ple, D)), const((1, D))],
            out_specs=pl.BlockSpec((tc, D), lambda i, *_: (i, 0)),
            scratch_shapes=[pltpu.VMEM((2, N_EXPERTS * COMB_WINDOW, D), F32),
                            pltpu.SemaphoreType.DMA((2,))]),
        out_shape=jax.ShapeDtypeStruct((T, D), F32),
        compiler_params=pltpu.CompilerParams(dimension_semantics=("arbitrary",),
                                             vmem_limit_bytes=VMEM_LIMIT_BYTES),
        name="moe_combine",
    )(ws, ok, dest_t, dest_t, yb, pos_t, w_t, x1, p2d, row(g_ple), w_ple_gate.astype(BF16),
      w_ple_proj.astype(BF16), row(g_final))


def _combine_windows(pstart, trun, counts, idx_t, dest_t, tile, n_rows):
    n_tiles = trun.shape[0]
    tcnt = jnp.concatenate([trun[1:], counts[None, :]], axis=0) - trun
    start = pstart[None, :] + trun
    ws = jnp.minimum(start // SUBLANES * SUBLANES, n_rows - COMB_WINDOW).astype(I32)
    ok = jnp.all(start + tcnt - ws <= COMB_WINDOW, axis=1).astype(I32)
    ws_tok = _select_expert(jnp.repeat(ws, tile, axis=0)[None], idx_t)
    pos_t = idx_t * COMB_WINDOW + dest_t - ws_tok
    return ws.reshape(-1), ok, pos_t.astype(I32)


def _select_expert(table, idx_t):
    onehot = idx_t[:, :, None] == jnp.arange(N_EXPERTS, dtype=I32)
    return jnp.sum(jnp.where(onehot, table, 0), axis=-1)


def _group_layout(counts, n_blocks):
    te = EXPERT_TILE
    padded = (counts + te - 1) // te * te
    pends = jnp.cumsum(padded)
    pstart = (pends - padded).astype(I32)
    nused = (pends[-1] // te).astype(I32)
    tail = nused + jnp.arange(N_EXPERTS, dtype=I32)
    zstart = jnp.concatenate([jnp.where(padded > 0, pends - te, -1),
                              jnp.where(tail < n_blocks, tail * te, -1)]).astype(I32)
    blk_src = jnp.minimum(jnp.arange(n_blocks, dtype=I32), jnp.maximum(nused - 1, 0))
    blk_e = jnp.sum((blk_src * te)[:, None] >= pends[None, :], axis=1)
    blk_e = jnp.clip(blk_e, 0, N_EXPERTS - 1).astype(I32)
    blk = jnp.arange(n_blocks, dtype=I32)
    prev_e = jnp.concatenate([blk_e[:1], blk_e[:-1]])
    first = ((blk < nused) & ((blk == 0) | (blk_e != prev_e))).astype(I32)
    experts = jnp.arange(N_EXPERTS, dtype=I32)[None, :]
    later = (experts > blk_e[:, None]) & (counts[None, :] > 0)
    next_e = jnp.min(jnp.where(later, experts, N_EXPERTS), axis=1)
    next_e = jnp.where(next_e < N_EXPERTS, next_e, -1).astype(I32)
    overflow = jnp.any(counts > EXPERT_CAP)
    pstart_blk = jnp.sum(jnp.where(blk_e[:, None] == experts, pstart[None, :], 0), axis=1)
    blk_cap = blk_e * (EXPERT_CAP // te) + (blk_src - pstart_blk // te)
    blk_src = jnp.where(overflow, blk_src, blk_cap).astype(I32)
    return (pstart, zstart, blk_e, blk_src, nused.reshape(1), first, next_e,
            overflow.astype(I32).reshape(1))


def kernel(x, p, positions, g_mix_norm, w_in, g_ret_norm, w_pool, pool_scale, w_branch, w_out,
           g_ffn_norm, w_router, b_router, w_gate_up, b_gate_up, w_down, b_down,
           g_ple_norm, w_ple_gate, w_ple_proj, g_final):
    B, S, D = x.shape
    depth = w_in.shape[0]
    T = B * S
    xt = x.reshape(T, D)
    cos, sin = _rope_tables(positions.reshape(T, 1))
    n_blocks = (T * TOP_K) // EXPERT_TILE + N_EXPERTS
    for i in range(depth):
        x1, h2, idx_t, w_t, rank_t, cnt, trun, xs = _token_mix(
            xt, cos, sin, g_mix_norm[i], w_in[i], g_ret_norm[i], w_pool[i], pool_scale[i],
            w_branch[i], w_out[i], g_ffn_norm[i], w_router[i], b_router[i], S)
        (pstart, zstart, blk_e, blk_src, nused, first, next_e,
         overflow) = _group_layout(cnt[:, 0], n_blocks)
        dest_t = _select_expert(pstart[None, None, :], idx_t) + rank_t
        xs = _moe_dispatch_fixup(overflow, zstart, dest_t, h2, xs)
        yb = _moe_experts(xs, blk_e, blk_src, nused, first, next_e,
                          w_gate_up[i], b_gate_up[i], w_down[i], b_down[i])
        assert depth == 1
        assert min(MIX_TILE, S) == min(COMB_TILE, T)
        ws, ok, pos_t = _combine_windows(pstart, trun[:, :, 0], cnt[:, 0], idx_t, dest_t,
                                         min(COMB_TILE, T), n_blocks * EXPERT_TILE)
        xt = _moe_combine(ws, ok, dest_t, pos_t, yb, w_t, x1, p[i].reshape(T, -1), g_ple_norm[i],
                          w_ple_gate[i], w_ple_proj[i], g_final)
    return xt.reshape(B, S, D)
```

```python
import functools

import numpy as np
import jax
import jax.numpy as jnp
from jax import lax
from jax.experimental import pallas as pl
from jax.experimental.pallas import tpu as pltpu

F32 = jnp.float32
BF16 = jnp.bfloat16
I32 = jnp.int32

RET_HEADS = 8
RET_HEAD_DIM = 128
ROPE_BASE = 10000.0
GN_EPS = 1e-5
RMS_EPS = 1e-6
POOL_WINDOWS = (2, 4, 8, 16)
N_EXPERTS = 32
TOP_K = 4
SWIGLU_ALPHA = 1.702
SWIGLU_LIMIT = 7.0

LANES = 128
SUBLANES = 8
VMEM_LIMIT_BYTES = 56 * 1024 * 1024

MIX_TILE = 256
ROPE_TILE = 1024
DISP_TILE = 2048
EXPERT_CAP = 2560
DISPATCH_LAG_SLOTS = 3
EXPERT_TILE = 256
COMB_TILE = 256
POOL_HALO = 16
POOL_PAD = 8
COMB_WINDOW = 64
ROW_UNROLL = 8


def _const_spec(shape):
    nd = len(shape)
    return pl.BlockSpec(shape, lambda *_: (0,) * nd, pipeline_mode=pl.Buffered(1))


def _zero_unused_capacity(e, cnt, zbuf_ref, xs_ref, zsem):
    te = zbuf_ref.shape[0]
    cnt = jnp.minimum(cnt, EXPERT_CAP)
    first = e * EXPERT_CAP + cnt
    n = (-cnt) % te
    head = jnp.minimum(n, (-first) % SUBLANES)
    out = []
    for r in range(SUBLANES - 1):
        out.append((r < head, pltpu.make_async_copy(zbuf_ref.at[pl.ds(0, 1)],
                                                    xs_ref.at[pl.ds(first + r, 1)], zsem)))
    rest = n - head
    pos = first + head
    size = te // 2
    while size >= SUBLANES:
        at = pl.multiple_of(pos + (rest & ~(2 * size - 1)), SUBLANES)
        out.append(((rest & size) != 0,
                    pltpu.make_async_copy(zbuf_ref.at[pl.ds(0, size)], xs_ref.at[pl.ds(at, size)], zsem)))
        size //= 2
    used_end = first + n
    for b in range(EXPERT_CAP // te):
        at = pl.multiple_of(used_end + b * te, te)
        out.append((at < (e + 1) * EXPERT_CAP,
                    pltpu.make_async_copy(zbuf_ref, xs_ref.at[pl.ds(at, te)], zsem)))
    return out


def _rms(x, g):
    return x * lax.rsqrt(jnp.mean(x * x, axis=-1, keepdims=True) + RMS_EPS) * g


def _dot(a, b):
    return jnp.dot(a, b, preferred_element_type=F32)


def _dot_nt(a, b, precision=None):
    return lax.dot_general(a, b, (((1,), (1,)), ((), ())),
                           preferred_element_type=F32, precision=precision)


def _dot_tn(a, b):
    return lax.dot_general(a, b, (((0,), (0,)), ((), ())), preferred_element_type=F32)


def _rope_body(pos_ref, inv_ref, sign_ref, cos_ref, sin_ref):
    ang = pos_ref[...].astype(F32) * inv_ref[...]
    cos_ref[...] = jnp.cos(ang)
    sin_ref[...] = jnp.sin(ang) * sign_ref[...]


def _rope_tables(pos_col):
    T = pos_col.shape[0]
    half = RET_HEAD_DIM // 2
    inv = ROPE_BASE ** (-jnp.arange(half, dtype=F32) / half)
    inv_full = jnp.concatenate([inv, inv]).reshape(1, RET_HEAD_DIM)
    sign = jnp.concatenate([-jnp.ones((half,), F32), jnp.ones((half,), F32)]).reshape(1, RET_HEAD_DIM)
    tile = min(ROPE_TILE, T)
    return pl.pallas_call(
        _rope_body,
        grid=(T // tile,),
        in_specs=[pl.BlockSpec((tile, 1), lambda i: (i, 0)),
                  pl.BlockSpec((1, RET_HEAD_DIM), lambda i: (0, 0)),
                  pl.BlockSpec((1, RET_HEAD_DIM), lambda i: (0, 0))],
        out_specs=[pl.BlockSpec((tile, RET_HEAD_DIM), lambda i: (i, 0)),
                   pl.BlockSpec((tile, RET_HEAD_DIM), lambda i: (i, 0))],
        out_shape=[jax.ShapeDtypeStruct((T, RET_HEAD_DIM), F32),
                   jax.ShapeDtypeStruct((T, RET_HEAD_DIM), F32)],
        name="rope_tables",
    )(pos_col, inv_full, sign)


def _retention_constants(tile):
    h = np.arange(RET_HEADS, dtype=np.float64)
    log_gamma = np.log1p(-np.exp2(-5.0 - h))
    idx = np.arange(tile, dtype=np.float64)
    diff = idx[:, None] - idx[None, :]
    dmat = np.where(diff >= 0, np.exp(log_gamma[:, None, None] * np.maximum(diff, 0.0)[None]), 0.0)
    xi = np.exp(log_gamma[:, None] * (idx + 1.0)[None])
    zeta = np.exp(log_gamma[:, None] * (tile - 1.0 - idx)[None])
    chunk_decay = np.exp(log_gamma * tile)
    xi_b = np.broadcast_to(xi[:, :, None], (RET_HEADS, tile, RET_HEAD_DIM))
    zeta_b = np.broadcast_to(zeta[:, :, None], (RET_HEADS, tile, RET_HEAD_DIM))
    return (jnp.asarray(dmat, F32), jnp.asarray(xi_b, F32), jnp.asarray(zeta_b, F32),
            tuple(float(c) for c in chunk_decay))


def _mix_body(x_ref, cos_ref, sin_ref, gmix_ref, win_ref, dmat_ref, xi_ref, zeta_ref, gret_ref,
              wpool_ref, pscale_ref, wbr_ref, wout_ref, gffn_ref, wr_ref, br_ref, tri_ref,
              x1_ref, h2_ref, idx_ref, w_ref, rank_ref, cnt_ref, trun_ref, xs_hbm,
              state_ref, ue_ref, lv_ref, run_ref, hbuf_ref, dvm_ref, dsm_ref, cvm_ref, csm_ref,
              zbuf_ref, rsem, dsem, csem, zsem, *, tiles_per_seq, chunk_decay, trash):
    tm, d_model = x_ref.shape
    ret_width = RET_HEADS * RET_HEAD_DIM
    i = pl.program_id(0)
    seq_tile = i % tiles_per_seq
    slot = i % DISPATCH_LAG_SLOTS
    prev = (i + DISPATCH_LAG_SLOTS - 1) % DISPATCH_LAG_SLOTS
    old = (i + DISPATCH_LAG_SLOTS - 2) % DISPATCH_LAG_SLOTS

    def row_copy(src_slot, t, dest):
        return pltpu.make_async_copy(hbuf_ref.at[src_slot, pl.ds(t, 1)], xs_hbm.at[pl.ds(dest, 1)], rsem)

    def wait_tile_rows():
        for _ in range(TOP_K):
            pltpu.make_async_copy(hbuf_ref.at[0], xs_hbm.at[pl.ds(0, tm)], rsem).wait()

    def dest_copy(s_):
        return pltpu.make_async_copy(dvm_ref.at[s_], dsm_ref.at[s_], dsem.at[s_])

    @pl.when(seq_tile == 0)
    def _():
        state_ref[...] = jnp.zeros(state_ref.shape, F32)
        ue_ref[0:POOL_PAD + POOL_HALO, :] = jnp.zeros((POOL_PAD + POOL_HALO, ue_ref.shape[1]), F32)

    @pl.when(i == 0)
    def _():
        run_ref[...] = jnp.zeros(run_ref.shape, F32)
        lv_ref[:, 0:POOL_PAD, :] = jnp.zeros((lv_ref.shape[0], POOL_PAD, lv_ref.shape[2]), F32)
        for s_ in range(1, DISPATCH_LAG_SLOTS):
            hbuf_ref[s_] = jnp.zeros(hbuf_ref.shape[1:], F32)
            for kk in range(TOP_K):
                dvm_ref[s_, kk:kk + 1, :] = trash + kk * tm + lax.broadcasted_iota(I32, (1, tm), 1)
            dest_copy(s_).start()

    dest_copy(old).wait()

    x = x_ref[...]
    hb = _rms(x, gmix_ref[...]).astype(BF16)

    qkvg = _dot(hb, win_ref[:, 0:4 * ret_width])
    cos = cos_ref[...]
    sin = sin_ref[...]

    def rot(a):
        return a * cos + pltpu.roll(a, RET_HEAD_DIM // 2, 1) * sin

    ys = []
    for h in range(RET_HEADS):
        lo = h * RET_HEAD_DIM
        q = rot(qkvg[:, lo:lo + RET_HEAD_DIM]).astype(BF16)
        k = (rot(qkvg[:, ret_width + lo:ret_width + lo + RET_HEAD_DIM])
             * (RET_HEAD_DIM ** -0.5)).astype(BF16)
        v = qkvg[:, 2 * ret_width + lo:2 * ret_width + lo + RET_HEAD_DIM]
        g = qkvg[:, 3 * ret_width + lo:3 * ret_width + lo + RET_HEAD_DIM]
        scores = _dot_nt(q, k) * dmat_ref[h]
        inner = _dot(scores.astype(BF16), v.astype(BF16))
        st = state_ref[h]
        cross = _dot(q, st.astype(BF16)) * xi_ref[h]
        state_ref[h] = st * chunk_decay[h] + _dot_tn(k, (v * zeta_ref[h]).astype(BF16))
        ret = inner + cross
        mu = jnp.mean(ret, axis=-1, keepdims=True)
        dev = ret - mu
        var = jnp.mean(dev * dev, axis=-1, keepdims=True)
        rn = dev * lax.rsqrt(var + GN_EPS) * gret_ref[:, lo:lo + RET_HEAD_DIM]
        ys.append(((g * jax.nn.sigmoid(g)) * rn).astype(BF16))
        per_head = tm // RET_HEADS
        for t in range(h * per_head, (h + 1) * per_head):
            for kk in range(TOP_K):
                row_copy(old, t, dsm_ref[old, kk, t]).start(priority=kk % 2)
    y_ret = jnp.concatenate(ys, axis=1)
    branch_a = _dot(y_ret, wbr_ref[0])

    u = _dot(hb, win_ref[:, 4 * ret_width:4 * ret_width + d_model])
    top = POOL_PAD + POOL_HALO
    ext = tm + POOL_HALO
    group = d_model // len(POOL_WINDOWS)
    ue_ref[top:top + tm, :] = u
    s2 = ue_ref[POOL_PAD:POOL_PAD + ext, :] + ue_ref[POOL_PAD - 1:POOL_PAD - 1 + ext, :]
    lv_ref[0, POOL_PAD:POOL_PAD + ext, :] = s2
    s4 = s2[:, group:] + lv_ref[0, POOL_PAD - 2:POOL_PAD - 2 + ext, group:]
    lv_ref[1, POOL_PAD:POOL_PAD + ext, group:] = s4
    s8 = s4[:, group:] + lv_ref[1, POOL_PAD - 4:POOL_PAD - 4 + ext, 2 * group:]
    lv_ref[2, POOL_PAD:POOL_PAD + ext, 2 * group:] = s8
    sums = [lv_ref[0, top:top + tm, 0:group],
            lv_ref[1, top:top + tm, group:2 * group],
            lv_ref[2, top:top + tm, 2 * group:3 * group],
            lv_ref[2, top:top + tm, 3 * group:] + lv_ref[2, top - 8:top - 8 + tm, 3 * group:]]
    ue_ref[POOL_PAD:top, :] = ue_ref[tm + POOL_PAD:tm + top, :]
    pos = seq_tile * tm + lax.broadcasted_iota(I32, (tm, 1), 0)
    outs = []
    for gi, w in enumerate(POOL_WINDOWS):
        ug = u[:, gi * group:(gi + 1) * group]
        inv_count = 1.0 / jnp.minimum(pos + 1, w).astype(F32)
        mixed = (sums[gi] * inv_count - ug).astype(BF16)
        outs.append(_dot(mixed, wpool_ref[gi]))
    y_pool = (jnp.concatenate(outs, axis=1) * pscale_ref[...]).astype(BF16)
    branch_b = _dot(y_pool, wbr_ref[1])

    gates = _dot(hb, win_ref[:, 4 * ret_width + d_model:4 * ret_width + 3 * d_model])
    merged = (jax.nn.sigmoid(gates[:, 0:d_model]) * branch_a
              + jax.nn.sigmoid(gates[:, d_model:2 * d_model]) * branch_b)
    x1 = x + _dot(merged.astype(BF16), wout_ref[...])
    x1_ref[...] = x1

    h2 = _rms(x1, gffn_ref[...])
    h2_ref[...] = h2
    n_exp = br_ref.shape[0]
    h2_hi = h2.astype(BF16)
    h2_lo = (h2 - h2_hi.astype(F32)).astype(BF16)
    both = _dot(h2_hi, wr_ref[...])
    lg = both[:, 0:LANES] + both[:, LANES:2 * LANES] + _dot(h2_lo, wr_ref[:, 0:LANES])
    logits = lg.T[0:n_exp, :] + br_ref[...]
    eiota = lax.broadcasted_iota(I32, (n_exp, tm), 0)
    vals, idxs = [], []
    l = logits
    for _ in range(TOP_K):
        m = jnp.max(l, axis=0, keepdims=True)
        sel = jnp.min(jnp.where(l == m, eiota, n_exp), axis=0, keepdims=True)
        vals.append(m)
        idxs.append(sel)
        l = jnp.where(eiota == sel, -jnp.inf, l)
    exps = [jnp.exp(v - vals[0]) for v in vals]
    denom = exps[0] + exps[1] + exps[2] + exps[3]
    inv_denom = 1.0 / denom
    onehot = jnp.zeros((n_exp, tm), F32)
    for kk in range(TOP_K):
        onehot = onehot + (eiota == idxs[kk]).astype(F32)
    base = _dot(onehot.astype(BF16), tri_ref[...]) + run_ref[:, 0:1]
    tiota = lax.broadcasted_iota(I32, (1, tm), 1)
    for kk in range(TOP_K):
        idx_ref[kk:kk + 1, :] = idxs[kk]
        w_ref[kk:kk + 1, :] = exps[kk] * inv_denom
        rank = jnp.sum(jnp.where(eiota == idxs[kk], base, 0.0), axis=0, keepdims=True).astype(I32)
        rank_ref[kk:kk + 1, :] = rank
        dvm_ref[slot, kk:kk + 1, :] = jnp.where(rank < EXPERT_CAP, idxs[kk] * EXPERT_CAP + rank,
                                                trash + kk * tm + tiota)
    dest_copy(slot).start()
    trun_ref[...] = run_ref[...].astype(I32)
    run = run_ref[...] + jnp.sum(onehot, axis=1, keepdims=True)
    run_ref[...] = run
    cnt_ref[...] = run.astype(I32)

    hbuf_ref[slot] = h2
    wait_tile_rows()

    @pl.when(i == pl.num_programs(0) - 1)
    def _():
        cvm_ref[...] = run.astype(I32)
        counts_copy = pltpu.make_async_copy(cvm_ref, csm_ref, csem)
        counts_copy.start()
        zbuf_ref[...] = jnp.zeros(zbuf_ref.shape, F32)
        counts_copy.wait()

        def zero_start(e, c):
            for cond, cp in _zero_unused_capacity(e, csm_ref[e, 0], zbuf_ref, xs_hbm, zsem):
                @pl.when(cond)
                def _():
                    cp.start()
            return c

        def zero_wait(e, c):
            for cond, cp in _zero_unused_capacity(e, csm_ref[e, 0], zbuf_ref, xs_hbm, zsem):
                @pl.when(cond)
                def _():
                    cp.wait()
            return c

        lax.fori_loop(0, N_EXPERTS, zero_start, 0)

        for s_ in (prev, slot):
            dest_copy(s_).wait()

            def start(j, c):
                ts = [j * ROW_UNROLL + u for u in range(ROW_UNROLL)]
                dests = [[dsm_ref[s_, kk, t] for kk in range(TOP_K)] for t in ts]
                for t, dest in zip(ts, dests):
                    for kk in range(TOP_K):
                        row_copy(s_, t, dest[kk]).start(priority=kk % 2)
                return c

            lax.fori_loop(0, tm // ROW_UNROLL, start, 0)
            wait_tile_rows()

        lax.fori_loop(0, N_EXPERTS, zero_wait, 0)


def _split_router(w_router):
    d, e = w_router.shape
    hi = w_router.astype(BF16)
    lo = (w_router - hi.astype(F32)).astype(BF16)
    pad = jnp.zeros((d, LANES - e), BF16)
    return jnp.concatenate([hi, pad, lo, pad], axis=1)


def _token_mix(x2d, cos, sin, g_mix, w_in, g_ret, w_pool, pool_scale, w_branch, w_out,
               g_ffn, w_router, b_router, seq_len):
    T, D = x2d.shape
    tm = min(MIX_TILE, seq_len)
    in_width = w_in.shape[1]
    ret_width = RET_HEADS * RET_HEAD_DIM
    assert T // tm >= DISPATCH_LAG_SLOTS - 1
    assert POOL_WINDOWS == (2, 4, 8, 16)
    dmat, xi_b, zeta_b, chunk_decay = _retention_constants(tm)
    tri = jnp.asarray(np.triu(np.ones((tm, tm), np.float32), 1), BF16)
    row = lambda a: a.reshape(1, -1)
    tile_spec = lambda w: pl.BlockSpec((tm, w), lambda i: (i, 0))
    top_spec = pl.BlockSpec((TOP_K, tm), lambda i: (0, i))
    n_xs = max(N_EXPERTS * EXPERT_CAP, (T * TOP_K // EXPERT_TILE + N_EXPERTS) * EXPERT_TILE)
    body = functools.partial(_mix_body, tiles_per_seq=seq_len // tm, chunk_decay=chunk_decay, trash=n_xs)
    return pl.pallas_call(
        body,
        grid=(T // tm,),
        in_specs=[tile_spec(D), tile_spec(RET_HEAD_DIM), tile_spec(RET_HEAD_DIM),
                  _const_spec((1, D)), _const_spec((D, in_width)),
                  _const_spec(dmat.shape), _const_spec(xi_b.shape), _const_spec(zeta_b.shape),
                  _const_spec((1, ret_width)), _const_spec(w_pool.shape), _const_spec((1, D)),
                  _const_spec(w_branch.shape), _const_spec((D, D)), _const_spec((1, D)),
                  _const_spec((D, 2 * LANES)), _const_spec((N_EXPERTS, 1)), _const_spec((tm, tm))],
        out_specs=[tile_spec(D), tile_spec(D), top_spec, top_spec, top_spec,
                   pl.BlockSpec((N_EXPERTS, LANES), lambda i: (0, 0)),
                   pl.BlockSpec((None, N_EXPERTS, LANES), lambda i: (i, 0, 0)),
                   pl.BlockSpec(memory_space=pl.ANY)],
        out_shape=[jax.ShapeDtypeStruct((T, D), F32), jax.ShapeDtypeStruct((T, D), F32),
                   jax.ShapeDtypeStruct((TOP_K, T), I32), jax.ShapeDtypeStruct((TOP_K, T), F32),
                   jax.ShapeDtypeStruct((TOP_K, T), I32),
                   jax.ShapeDtypeStruct((N_EXPERTS, LANES), I32),
                   jax.ShapeDtypeStruct((T // tm, N_EXPERTS, LANES), I32),
                   jax.ShapeDtypeStruct((n_xs + TOP_K * tm, D), F32)],
        scratch_shapes=[pltpu.VMEM((RET_HEADS, RET_HEAD_DIM, RET_HEAD_DIM), F32),
                        pltpu.VMEM((tm + POOL_PAD + POOL_HALO, D), F32),
                        pltpu.VMEM((3, tm + POOL_PAD + POOL_HALO, D), F32),
                        pltpu.VMEM((N_EXPERTS, LANES), F32),
                        pltpu.VMEM((DISPATCH_LAG_SLOTS, tm, D), F32),
                        pltpu.VMEM((DISPATCH_LAG_SLOTS, TOP_K, tm), I32),
                        pltpu.SMEM((DISPATCH_LAG_SLOTS, TOP_K, tm), I32),
                        pltpu.VMEM((N_EXPERTS, LANES), I32), pltpu.SMEM((N_EXPERTS, LANES), I32),
                        pltpu.VMEM((EXPERT_TILE, D), F32),
                        pltpu.SemaphoreType.DMA, pltpu.SemaphoreType.DMA((DISPATCH_LAG_SLOTS,)),
                        pltpu.SemaphoreType.DMA, pltpu.SemaphoreType.DMA],
        compiler_params=pltpu.CompilerParams(dimension_semantics=("arbitrary",),
                                             vmem_limit_bytes=VMEM_LIMIT_BYTES),
        name="token_mix",
    )(x2d, cos, sin, row(g_mix), w_in.astype(BF16), dmat, xi_b, zeta_b, row(g_ret),
      w_pool.astype(BF16), row(pool_scale), w_branch.astype(BF16), w_out.astype(BF16),
      row(g_ffn), _split_router(w_router), b_router.reshape(-1, 1), tri)


def _fixup_body(flag_ref, zstart_ref, dest_ref, h2_hbm, xs_in, xs_ref, hbuf_ref, zbuf_ref,
                sem, zsem, hsem):
    del xs_in
    td = hbuf_ref.shape[0]
    te = zbuf_ref.shape[0]
    i = pl.program_id(0)
    overflow = flag_ref[0] != 0

    @pl.when((i == 0) & overflow)
    def _():
        zbuf_ref[...] = jnp.zeros(zbuf_ref.shape, F32)

    def zero_copy(e):
        start = pl.multiple_of(jnp.maximum(zstart_ref[e], 0), te)
        return pltpu.make_async_copy(zbuf_ref, xs_ref.at[pl.ds(start, te)], zsem)

    @pl.when((i == 0) & overflow)
    def _():
        def start(e, c):
            @pl.when(zstart_ref[e] >= 0)
            def _():
                zero_copy(e).start()
            return c

        def wait(e, c):
            @pl.when(zstart_ref[e] >= 0)
            def _():
                zero_copy(e).wait()
            return c

        lax.fori_loop(0, zstart_ref.shape[0], start, 0)
        lax.fori_loop(0, zstart_ref.shape[0], wait, 0)

    @pl.when(overflow)
    def _():
        tile = pltpu.make_async_copy(h2_hbm.at[pl.ds(i * td, td)], hbuf_ref, hsem)
        tile.start()
        tile.wait()

        def start(j, c):
            ts = [j * ROW_UNROLL + u for u in range(ROW_UNROLL)]
            dests = [[dest_ref[kk, t] for kk in range(TOP_K)] for t in ts]
            for t, dest in zip(ts, dests):
                for kk in range(TOP_K):
                    pltpu.make_async_copy(hbuf_ref.at[pl.ds(t, 1)], xs_ref.at[pl.ds(dest[kk], 1)],
                                          sem).start(priority=kk % 2)
            return c

        lax.fori_loop(0, td // ROW_UNROLL, start, 0)
        for kk in range(TOP_K):
            pltpu.make_async_copy(hbuf_ref, xs_ref.at[pl.ds(0, td)], sem).wait()


def _moe_dispatch_fixup(overflow, zstart, dest_t, h2, xs):
    T, D = h2.shape
    td = min(DISP_TILE, T)
    smem_spec = pl.BlockSpec((TOP_K, td), lambda i, *_: (0, i), memory_space=pltpu.SMEM)
    return pl.pallas_call(
        _fixup_body,
        grid_spec=pltpu.PrefetchScalarGridSpec(
            num_scalar_prefetch=2,
            grid=(T // td,),
            in_specs=[smem_spec, pl.BlockSpec(memory_space=pl.ANY), pl.BlockSpec(memory_space=pl.ANY)],
            out_specs=pl.BlockSpec(memory_space=pl.ANY),
            scratch_shapes=[pltpu.VMEM((td, D), F32), pltpu.VMEM((EXPERT_TILE, D), F32),
                            pltpu.SemaphoreType.DMA, pltpu.SemaphoreType.DMA, pltpu.SemaphoreType.DMA]),
        out_shape=jax.ShapeDtypeStruct(xs.shape, xs.dtype),
        input_output_aliases={4: 0},
        compiler_params=pltpu.CompilerParams(dimension_semantics=("arbitrary",),
                                             vmem_limit_bytes=VMEM_LIMIT_BYTES),
        name="moe_dispatch_fixup",
    )(overflow, zstart, dest_t, h2, xs)


def _expert_body(blk_e_ref, blk_src_ref, nused_ref, first_ref, next_e_ref, xs_ref, wgu_hbm, bgu_ref,
                 wd_hbm, bd_ref, yb_ref, wgu_f32_ref, wd_f32_ref, wgu_bf_ref, wd_bf_ref, wsem):
    i = pl.program_id(0)
    d_ff = wd_bf_ref.shape[0]

    def weight_copies(e):
        return (pltpu.make_async_copy(wgu_hbm.at[pl.ds(e, 1)], wgu_f32_ref, wsem.at[0]),
                pltpu.make_async_copy(wd_hbm.at[pl.ds(e, 1)], wd_f32_ref, wsem.at[1]))

    @pl.when(i == 0)
    def _():
        for cp in weight_copies(blk_e_ref[0]):
            cp.start()

    @pl.when(first_ref[i] != 0)
    def _():
        for cp in weight_copies(blk_e_ref[i]):
            cp.wait()
        wgu_bf_ref[...] = wgu_f32_ref[0].astype(BF16)
        wd_bf_ref[...] = wd_f32_ref[0].astype(BF16)

    @pl.when((first_ref[i] != 0) & (next_e_ref[i] >= 0))
    def _():
        for cp in weight_copies(next_e_ref[i]):
            cp.start(priority=1)

    @pl.when(i < nused_ref[0])
    def _():
        gu = _dot(xs_ref[...].astype(BF16), wgu_bf_ref[...]) + bgu_ref[...]
        gate = jnp.minimum(gu[:, 0:d_ff], SWIGLU_LIMIT)
        up = jnp.clip(gu[:, d_ff:2 * d_ff], -SWIGLU_LIMIT, SWIGLU_LIMIT)
        act = (up + 1.0) * (gate * jax.nn.sigmoid(gate * SWIGLU_ALPHA))
        yb_ref[...] = _dot(act.astype(BF16), wd_bf_ref[...]) + bd_ref[...]

    @pl.when(i >= nused_ref[0])
    def _():
        yb_ref[...] = jnp.zeros(yb_ref.shape, F32)


def _moe_experts(xs, blk_e, blk_src, nused, first, next_e, w_gate_up, b_gate_up, w_down, b_down):
    D = xs.shape[1]
    E, _, two_ff = w_gate_up.shape
    d_ff = two_ff // 2
    te = EXPERT_TILE
    n_rows = blk_e.shape[0] * te
    return pl.pallas_call(
        _expert_body,
        grid_spec=pltpu.PrefetchScalarGridSpec(
            num_scalar_prefetch=5,
            grid=(n_rows // te,),
            in_specs=[pl.BlockSpec((te, D), lambda i, be, bs, *_: (bs[i], 0)),
                      pl.BlockSpec(memory_space=pl.ANY),
                      pl.BlockSpec((None, 1, two_ff), lambda i, be, *_: (be[i], 0, 0)),
                      pl.BlockSpec(memory_space=pl.ANY),
                      pl.BlockSpec((None, 1, D), lambda i, be, *_: (be[i], 0, 0))],
            out_specs=pl.BlockSpec((te, D), lambda i, *_: (i, 0)),
            scratch_shapes=[pltpu.VMEM((1, D, two_ff), F32), pltpu.VMEM((1, d_ff, D), F32),
                            pltpu.VMEM((D, two_ff), BF16), pltpu.VMEM((d_ff, D), BF16),
                            pltpu.SemaphoreType.DMA((2,))]),
        out_shape=jax.ShapeDtypeStruct((n_rows, D), F32),
        compiler_params=pltpu.CompilerParams(dimension_semantics=("arbitrary",),
                                             vmem_limit_bytes=VMEM_LIMIT_BYTES),
        name="moe_experts",
    )(blk_e, blk_src, nused, first, next_e, xs, w_gate_up, b_gate_up.reshape(E, 1, two_ff),
      w_down, b_down.reshape(E, 1, D))


def _comb_body(ws_ref, ok_ref, dest_ref, destn_ref, yb_ref, pos_ref, w_ref, x1_ref, p_ref,
               gple_ref, wpg_ref, wpp_ref, gfin_ref, out_ref, gbuf_ref, sems):
    tc = x1_ref.shape[0]
    n_rows = gbuf_ref.shape[1]
    i = pl.program_id(0)
    slot = i % 2

    def window_copy(tile, e, s):
        src = pl.multiple_of(ws_ref[tile * N_EXPERTS + e], SUBLANES)
        return pltpu.make_async_copy(yb_ref.at[pl.ds(src, COMB_WINDOW)],
                                     gbuf_ref.at[s, pl.ds(e * COMB_WINDOW, COMB_WINDOW)], sems.at[s])

    def issue(tile, dest_r, s):
        @pl.when(ok_ref[tile] != 0)
        def _():
            for e in range(N_EXPERTS):
                window_copy(tile, e, s).start(priority=e % 2)

        @pl.when(ok_ref[tile] == 0)
        def _():
            def start(j, c):
                ts = [j * ROW_UNROLL + u for u in range(ROW_UNROLL)]
                srcs = [[dest_r[kk, t] for kk in range(TOP_K)] for t in ts]
                for t, src in zip(ts, srcs):
                    for kk in range(TOP_K):
                        pltpu.make_async_copy(yb_ref.at[pl.ds(src[kk], 1)],
                                              gbuf_ref.at[s, pl.ds(kk * tc + t, 1)],
                                              sems.at[s]).start(priority=kk % 2)
                return c

            lax.fori_loop(0, tc // ROW_UNROLL, start, 0)

    @pl.when(i == 0)
    def _():
        issue(0, dest_ref, 0)

    @pl.when(i + 1 < pl.num_programs(0))
    def _():
        issue(i + 1, destn_ref, 1 - slot)

    def to_cols(rows):
        pad = jnp.zeros((LANES - rows.shape[0], tc), F32)
        return jnp.concatenate([rows, pad], axis=0).T

    w_col = to_cols(w_ref[...])
    fast = ok_ref[i] != 0
    slow_pos = (lax.broadcasted_iota(I32, (TOP_K, tc), 0) * tc
                + lax.broadcasted_iota(I32, (TOP_K, tc), 1))
    pos_col = to_cols(jnp.where(fast, pos_ref[...], slow_pos).astype(F32))
    pp = _dot(p_ref[...].astype(BF16), wpp_ref[...])

    @pl.when(fast)
    def _():
        pltpu.make_async_copy(yb_ref.at[pl.ds(0, n_rows)], gbuf_ref.at[slot], sems.at[slot]).wait()

    @pl.when(jnp.logical_not(fast))
    def _():
        pltpu.make_async_copy(yb_ref.at[pl.ds(0, TOP_K * tc)], gbuf_ref.at[slot, pl.ds(0, TOP_K * tc)],
                              sems.at[slot]).wait()
        gbuf_ref[slot, pl.ds(TOP_K * tc, n_rows - TOP_K * tc), :] = jnp.zeros(
            (n_rows - TOP_K * tc, gbuf_ref.shape[2]), F32)

    ciota = lax.broadcasted_iota(I32, (tc, n_rows), 1).astype(F32)
    sel = jnp.zeros((tc, n_rows), F32)
    for kk in range(TOP_K):
        sel = jnp.where(ciota == pos_col[:, kk:kk + 1], w_col[:, kk:kk + 1], sel)
    moe = _dot(sel.astype(BF16), gbuf_ref[slot].astype(BF16))
    x2 = x1_ref[...] + moe
    h3 = _rms(x2, gple_ref[...]).astype(BF16)
    gate = jax.nn.sigmoid(_dot(h3, wpg_ref[...]))
    x3 = x2 + gate * pp
    out_ref[...] = _rms(x3, gfin_ref[...])


def _moe_combine(ws, ok, dest_t, pos_t, yb, w_t, x1, p2d, g_ple, w_ple_gate, w_ple_proj, g_final):
    T, D = x1.shape
    tc = min(COMB_TILE, T)
    n_tiles = T // tc
    ple = p2d.shape[1]
    smem_spec = pl.BlockSpec((TOP_K, tc), lambda i, *_: (0, i), memory_space=pltpu.SMEM)
    smem_next = pl.BlockSpec((TOP_K, tc), lambda i, *_: (0, jnp.minimum(i + 1, n_tiles - 1)),
                             memory_space=pltpu.SMEM)
    top_spec = pl.BlockSpec((TOP_K, tc), lambda i, *_: (0, i))
    const = lambda shape: pl.BlockSpec(shape, lambda i, *_: (0,) * len(shape),
                                       pipeline_mode=pl.Buffered(1))
    row = lambda a: a.reshape(1, -1)
    return pl.pallas_call(
        _comb_body,
        grid_spec=pltpu.PrefetchScalarGridSpec(
            num_scalar_prefetch=2,
            grid=(n_tiles,),
            in_specs=[smem_spec, smem_next,
                      pl.BlockSpec(memory_space=pl.ANY),
                      top_spec, top_spec,
                      pl.BlockSpec((tc, D), lambda i, *_: (i, 0)),
                      pl.BlockSpec((tc, ple), lambda i, *_: (i, 0)),
                      const((1, D)), const((D, D)), const((ple, D)), const((1, D))],
            out_specs=pl.BlockSpec((tc, D), lambda i, *_: (i, 0)),
            scratch_shapes=[pltpu.VMEM((2, N_EXPERTS * COMB_WINDOW, D), F32),
                            pltpu.SemaphoreType.DMA((2,))]),
        out_shape=jax.ShapeDtypeStruct((T, D), F32),
        compiler_params=pltpu.CompilerParams(dimension_semantics=("arbitrary",),
                                             vmem_limit_bytes=VMEM_LIMIT_BYTES),
        name="moe_combine",
    )(ws, ok, dest_t, dest_t, yb, pos_t, w_t, x1, p2d, row(g_ple), w_ple_gate.astype(BF16),
      w_ple_proj.astype(BF16), row(g_final))


def _combine_windows(pstart, trun, counts, idx_t, dest_t, tile, n_rows):
    n_tiles = trun.shape[0]
    tcnt = jnp.concatenate([trun[1:], counts[None, :]], axis=0) - trun
    start = pstart[None, :] + trun
    ws = jnp.minimum(start // SUBLANES * SUBLANES, n_rows - COMB_WINDOW).astype(I32)
    ok = jnp.all(start + tcnt - ws <= COMB_WINDOW, axis=1).astype(I32)
    ws_tok = _select_expert(jnp.repeat(ws, tile, axis=0)[None], idx_t)
    pos_t = idx_t * COMB_WINDOW + dest_t - ws_tok
    return ws.reshape(-1), ok, pos_t.astype(I32)


def _select_expert(table, idx_t):
    onehot = idx_t[:, :, None] == jnp.arange(N_EXPERTS, dtype=I32)
    return jnp.sum(jnp.where(onehot, table, 0), axis=-1)


def _group_layout(counts, n_blocks):
    te = EXPERT_TILE
    padded = (counts + te - 1) // te * te
    pends = jnp.cumsum(padded)
    pstart = (pends - padded).astype(I32)
    nused = (pends[-1] // te).astype(I32)
    tail = nused + jnp.arange(N_EXPERTS, dtype=I32)
    zstart = jnp.concatenate([jnp.where(padded > 0, pends - te, -1),
                              jnp.where(tail < n_blocks, tail * te, -1)]).astype(I32)
    blk_src = jnp.minimum(jnp.arange(n_blocks, dtype=I32), jnp.maximum(nused - 1, 0))
    blk_e = jnp.sum((blk_src * te)[:, None] >= pends[None, :], axis=1)
    blk_e = jnp.clip(blk_e, 0, N_EXPERTS - 1).astype(I32)
    blk = jnp.arange(n_blocks, dtype=I32)
    prev_e = jnp.concatenate([blk_e[:1], blk_e[:-1]])
    first = ((blk < nused) & ((blk == 0) | (blk_e != prev_e))).astype(I32)
    experts = jnp.arange(N_EXPERTS, dtype=I32)[None, :]
    later = (experts > blk_e[:, None]) & (counts[None, :] > 0)
    next_e = jnp.min(jnp.where(later, experts, N_EXPERTS), axis=1)
    next_e = jnp.where(next_e < N_EXPERTS, next_e, -1).astype(I32)
    overflow = jnp.any(counts > EXPERT_CAP)
    pstart_blk = jnp.sum(jnp.where(blk_e[:, None] == experts, pstart[None, :], 0), axis=1)
    blk_cap = blk_e * (EXPERT_CAP // te) + (blk_src - pstart_blk // te)
    blk_src = jnp.where(overflow, blk_src, blk_cap).astype(I32)
    return (pstart, zstart, blk_e, blk_src, nused.reshape(1), first, next_e,
            overflow.astype(I32).reshape(1))


def kernel(x, p, positions, g_mix_norm, w_in, g_ret_norm, w_pool, pool_scale, w_branch, w_out,
           g_ffn_norm, w_router, b_router, w_gate_up, b_gate_up, w_down, b_down,
           g_ple_norm, w_ple_gate, w_ple_proj, g_final):
    B, S, D = x.shape
    depth = w_in.shape[0]
    T = B * S
    xt = x.reshape(T, D)
    cos, sin = _rope_tables(positions.reshape(T, 1))
    n_blocks = (T * TOP_K) // EXPERT_TILE + N_EXPERTS
    for i in range(depth):
        x1, h2, idx_t, w_t, rank_t, cnt, trun, xs = _token_mix(
            xt, cos, sin, g_mix_norm[i], w_in[i], g_ret_norm[i], w_pool[i], pool_scale[i],
            w_branch[i], w_out[i], g_ffn_norm[i], w_router[i], b_router[i], S)
        (pstart, zstart, blk_e, blk_src, nused, first, next_e,
         overflow) = _group_layout(cnt[:, 0], n_blocks)
        dest_t = _select_expert(pstart[None, None, :], idx_t) + rank_t
        xs = _moe_dispatch_fixup(overflow, zstart, dest_t, h2, xs)
        yb = _moe_experts(xs, blk_e, blk_src, nused, first, next_e,
                          w_gate_up[i], b_gate_up[i], w_down[i], b_down[i])
        assert depth == 1
        assert min(MIX_TILE, S) == min(COMB_TILE, T)
        ws, ok, pos_t = _combine_windows(pstart, trun[:, :, 0], cnt[:, 0], idx_t, dest_t,
                                         min(COMB_TILE, T), n_blocks * EXPERT_TILE)
        xt = _moe_combine(ws, ok, dest_t, pos_t, yb, w_t, x1, p[i].reshape(T, -1), g_ple_norm[i],
                          w_ple_gate[i], w_ple_proj[i], g_final)
    return xt.reshape(B, S, D)
```

```python
import functools

import numpy as np
import jax
import jax.numpy as jnp
from jax import lax
from jax.experimental import pallas as pl
from jax.experimental.pallas import tpu as pltpu

F32 = jnp.float32
BF16 = jnp.bfloat16
I32 = jnp.int32

RET_HEADS = 8
RET_HEAD_DIM = 128
ROPE_BASE = 10000.0
GN_EPS = 1e-5
RMS_EPS = 1e-6
POOL_WINDOWS = (2, 4, 8, 16)
N_EXPERTS = 32
TOP_K = 4
SWIGLU_ALPHA = 1.702
SWIGLU_LIMIT = 7.0

LANES = 128
SUBLANES = 8
VMEM_LIMIT_BYTES = 56 * 1024 * 1024

MIX_TILE = 256
ROPE_TILE = 1024
DISP_TILE = 2048
EXPERT_CAP = 2560
DISPATCH_LAG_SLOTS = 3
EXPERT_TILE = 256
COMB_TILE = 256
POOL_HALO = 16
POOL_PAD = 8
COMB_WINDOW = 64
ROW_UNROLL = 8


def _const_spec(shape):
    nd = len(shape)
    return pl.BlockSpec(shape, lambda *_: (0,) * nd, pipeline_mode=pl.Buffered(1))


def _zero_unused_capacity(e, cnt, zbuf_ref, xs_ref, zsem):
    te = zbuf_ref.shape[0]
    cnt = jnp.minimum(cnt, EXPERT_CAP)
    first = e * EXPERT_CAP + cnt
    n = (-cnt) % te
    head = jnp.minimum(n, (-first) % SUBLANES)
    out = []
    for r in range(SUBLANES - 1):
        out.append((r < head, pltpu.make_async_copy(zbuf_ref.at[pl.ds(0, 1)],
                                                    xs_ref.at[pl.ds(first + r, 1)], zsem)))
    rest = n - head
    pos = first + head
    size = te // 2
    while size >= SUBLANES:
        at = pl.multiple_of(pos + (rest & ~(2 * size - 1)), SUBLANES)
        out.append(((rest & size) != 0,
                    pltpu.make_async_copy(zbuf_ref.at[pl.ds(0, size)], xs_ref.at[pl.ds(at, size)], zsem)))
        size //= 2
    used_end = first + n
    for b in range(EXPERT_CAP // te):
        at = pl.multiple_of(used_end + b * te, te)
        out.append((at < (e + 1) * EXPERT_CAP,
                    pltpu.make_async_copy(zbuf_ref, xs_ref.at[pl.ds(at, te)], zsem)))
    return out


def _rms(x, g):
    return x * lax.rsqrt(jnp.mean(x * x, axis=-1, keepdims=True) + RMS_EPS) * g


def _dot(a, b):
    return jnp.dot(a, b, preferred_element_type=F32)


def _dot_nt(a, b, precision=None):
    return lax.dot_general(a, b, (((1,), (1,)), ((), ())),
                           preferred_element_type=F32, precision=precision)


def _dot_tn(a, b):
    return lax.dot_general(a, b, (((0,), (0,)), ((), ())), preferred_element_type=F32)


def _rope_body(pos_ref, inv_ref, sign_ref, cos_ref, sin_ref):
    half_rows = pos_ref.shape[0] // 2
    half = RET_HEAD_DIM // 2
    lane = lax.broadcasted_iota(I32, (half_rows, RET_HEAD_DIM), 1)
    first = lane < half
    pos = jnp.where(first, pos_ref[pl.ds(0, half_rows, stride=2), :],
                    pos_ref[pl.ds(1, half_rows, stride=2), :]).astype(F32)
    ang = pos * inv_ref[...]
    cos = jnp.cos(ang)
    sin = jnp.sin(ang)
    cos_sw = pltpu.roll(cos, half, 1)
    sin_sw = pltpu.roll(sin, half, 1)
    sign = sign_ref[...]
    cos_ref[pl.ds(0, half_rows, stride=2), :] = jnp.where(first, cos, cos_sw)
    cos_ref[pl.ds(1, half_rows, stride=2), :] = jnp.where(first, cos_sw, cos)
    sin_ref[pl.ds(0, half_rows, stride=2), :] = jnp.where(first, sin, sin_sw) * sign
    sin_ref[pl.ds(1, half_rows, stride=2), :] = jnp.where(first, sin_sw, sin) * sign


def _rope_tables(pos_col):
    T = pos_col.shape[0]
    half = RET_HEAD_DIM // 2
    inv = ROPE_BASE ** (-jnp.arange(half, dtype=F32) / half)
    inv_full = jnp.concatenate([inv, inv]).reshape(1, RET_HEAD_DIM)
    sign = jnp.concatenate([-jnp.ones((half,), F32), jnp.ones((half,), F32)]).reshape(1, RET_HEAD_DIM)
    tile = min(ROPE_TILE, T)
    return pl.pallas_call(
        _rope_body,
        grid=(T // tile,),
        in_specs=[pl.BlockSpec((tile, 1), lambda i: (i, 0)),
                  pl.BlockSpec((1, RET_HEAD_DIM), lambda i: (0, 0)),
                  pl.BlockSpec((1, RET_HEAD_DIM), lambda i: (0, 0))],
        out_specs=[pl.BlockSpec((tile, RET_HEAD_DIM), lambda i: (i, 0)),
                   pl.BlockSpec((tile, RET_HEAD_DIM), lambda i: (i, 0))],
        out_shape=[jax.ShapeDtypeStruct((T, RET_HEAD_DIM), F32),
                   jax.ShapeDtypeStruct((T, RET_HEAD_DIM), F32)],
        name="rope_tables",
    )(pos_col, inv_full, sign)


def _retention_constants(tile):
    h = np.arange(RET_HEADS, dtype=np.float64)
    log_gamma = np.log1p(-np.exp2(-5.0 - h))
    idx = np.arange(tile, dtype=np.float64)
    diff = idx[:, None] - idx[None, :]
    dmat = np.where(diff >= 0, np.exp(log_gamma[:, None, None] * np.maximum(diff, 0.0)[None]), 0.0)
    xi = np.exp(log_gamma[:, None] * (idx + 1.0)[None])
    zeta = np.exp(log_gamma[:, None] * (tile - 1.0 - idx)[None])
    chunk_decay = np.exp(log_gamma * tile)
    xi_b = np.broadcast_to(xi[:, :, None], (RET_HEADS, tile, RET_HEAD_DIM))
    zeta_b = np.broadcast_to(zeta[:, :, None], (RET_HEADS, tile, RET_HEAD_DIM))
    return (jnp.asarray(dmat, F32), jnp.asarray(xi_b, F32), jnp.asarray(zeta_b, F32),
            tuple(float(c) for c in chunk_decay))


def _mix_body(x_ref, cos_ref, sin_ref, gmix_ref, win_ref, dmat_ref, xi_ref, zeta_ref, gret_ref,
              wpool_ref, pscale_ref, wbr_ref, wout_ref, gffn_ref, wr_ref, br_ref, tri_ref,
              x1_ref, h2_ref, idx_ref, w_ref, rank_ref, cnt_ref, trun_ref, xs_hbm,
              state_ref, ue_ref, lv_ref, run_ref, hbuf_ref, dvm_ref, dsm_ref, cvm_ref, csm_ref,
              zbuf_ref, rsem, dsem, csem, zsem, *, tiles_per_seq, chunk_decay, trash):
    tm, d_model = x_ref.shape
    ret_width = RET_HEADS * RET_HEAD_DIM
    i = pl.program_id(0)
    seq_tile = i % tiles_per_seq
    slot = i % DISPATCH_LAG_SLOTS
    prev = (i + DISPATCH_LAG_SLOTS - 1) % DISPATCH_LAG_SLOTS
    old = (i + DISPATCH_LAG_SLOTS - 2) % DISPATCH_LAG_SLOTS

    def row_copy(src_slot, t, dest):
        return pltpu.make_async_copy(hbuf_ref.at[src_slot, pl.ds(t, 1)], xs_hbm.at[pl.ds(dest, 1)], rsem)

    def wait_tile_rows():
        for _ in range(TOP_K):
            pltpu.make_async_copy(hbuf_ref.at[0], xs_hbm.at[pl.ds(0, tm)], rsem).wait()

    def dest_copy(s_):
        return pltpu.make_async_copy(dvm_ref.at[s_], dsm_ref.at[s_], dsem.at[s_])

    @pl.when(seq_tile == 0)
    def _():
        state_ref[...] = jnp.zeros(state_ref.shape, F32)
        ue_ref[0:POOL_PAD + POOL_HALO, :] = jnp.zeros((POOL_PAD + POOL_HALO, ue_ref.shape[1]), F32)

    @pl.when(i == 0)
    def _():
        run_ref[...] = jnp.zeros(run_ref.shape, F32)
        lv_ref[:, 0:POOL_PAD, :] = jnp.zeros((lv_ref.shape[0], POOL_PAD, lv_ref.shape[2]), F32)
        for s_ in range(1, DISPATCH_LAG_SLOTS):
            hbuf_ref[s_] = jnp.zeros(hbuf_ref.shape[1:], F32)
            for kk in range(TOP_K):
                dvm_ref[s_, kk:kk + 1, :] = trash + kk * tm + lax.broadcasted_iota(I32, (1, tm), 1)
            dest_copy(s_).start()

    dest_copy(old).wait()

    x = x_ref[...]
    hb = _rms(x, gmix_ref[...]).astype(BF16)

    qkvg = _dot(hb, win_ref[:, 0:4 * ret_width])
    cos = cos_ref[...]
    sin = sin_ref[...]

    def rot(a):
        return a * cos + pltpu.roll(a, RET_HEAD_DIM // 2, 1) * sin

    ys = []
    for h in range(RET_HEADS):
        lo = h * RET_HEAD_DIM
        q = rot(qkvg[:, lo:lo + RET_HEAD_DIM]).astype(BF16)
        k = (rot(qkvg[:, ret_width + lo:ret_width + lo + RET_HEAD_DIM])
             * (RET_HEAD_DIM ** -0.5)).astype(BF16)
        v = qkvg[:, 2 * ret_width + lo:2 * ret_width + lo + RET_HEAD_DIM]
        g = qkvg[:, 3 * ret_width + lo:3 * ret_width + lo + RET_HEAD_DIM]
        scores = _dot_nt(q, k) * dmat_ref[h]
        inner = _dot(scores.astype(BF16), v.astype(BF16))
        st = state_ref[h]
        cross = _dot(q, st.astype(BF16)) * xi_ref[h]
        state_ref[h] = st * chunk_decay[h] + _dot_tn(k, (v * zeta_ref[h]).astype(BF16))
        ret = inner + cross
        mu = jnp.mean(ret, axis=-1, keepdims=True)
        dev = ret - mu
        var = jnp.mean(dev * dev, axis=-1, keepdims=True)
        rn = dev * lax.rsqrt(var + GN_EPS) * gret_ref[:, lo:lo + RET_HEAD_DIM]
        ys.append(((g * jax.nn.sigmoid(g)) * rn).astype(BF16))
        per_head = tm // RET_HEADS
        for t in range(h * per_head, (h + 1) * per_head):
            for kk in range(TOP_K):
                row_copy(old, t, dsm_ref[old, kk, t]).start(priority=kk % 2)
    y_ret = jnp.concatenate(ys, axis=1)
    branch_a = _dot(y_ret, wbr_ref[0])

    u = _dot(hb, win_ref[:, 4 * ret_width:4 * ret_width + d_model])
    top = POOL_PAD + POOL_HALO
    ext = tm + POOL_HALO
    group = d_model // len(POOL_WINDOWS)
    ue_ref[top:top + tm, :] = u
    s2 = ue_ref[POOL_PAD:POOL_PAD + ext, :] + ue_ref[POOL_PAD - 1:POOL_PAD - 1 + ext, :]
    lv_ref[0, POOL_PAD:POOL_PAD + ext, :] = s2
    s4 = s2[:, group:] + lv_ref[0, POOL_PAD - 2:POOL_PAD - 2 + ext, group:]
    lv_ref[1, POOL_PAD:POOL_PAD + ext, group:] = s4
    s8 = s4[:, group:] + lv_ref[1, POOL_PAD - 4:POOL_PAD - 4 + ext, 2 * group:]
    lv_ref[2, POOL_PAD:POOL_PAD + ext, 2 * group:] = s8
    sums = [lv_ref[0, top:top + tm, 0:group],
            lv_ref[1, top:top + tm, group:2 * group],
            lv_ref[2, top:top + tm, 2 * group:3 * group],
            lv_ref[2, top:top + tm, 3 * group:] + lv_ref[2, top - 8:top - 8 + tm, 3 * group:]]
    ue_ref[POOL_PAD:top, :] = ue_ref[tm + POOL_PAD:tm + top, :]
    pos = seq_tile * tm + lax.broadcasted_iota(I32, (tm, 1), 0)
    outs = []
    for gi, w in enumerate(POOL_WINDOWS):
        ug = u[:, gi * group:(gi + 1) * group]
        inv_count = 1.0 / jnp.minimum(pos + 1, w).astype(F32)
        mixed = (sums[gi] * inv_count - ug).astype(BF16)
        outs.append(_dot(mixed, wpool_ref[gi]))
    y_pool = (jnp.concatenate(outs, axis=1) * pscale_ref[...]).astype(BF16)
    branch_b = _dot(y_pool, wbr_ref[1])

    gates = _dot(hb, win_ref[:, 4 * ret_width + d_model:4 * ret_width + 3 * d_model])
    merged = (jax.nn.sigmoid(gates[:, 0:d_model]) * branch_a
              + jax.nn.sigmoid(gates[:, d_model:2 * d_model]) * branch_b)
    x1 = x + _dot(merged.astype(BF16), wout_ref[...])
    x1_ref[...] = x1

    h2 = _rms(x1, gffn_ref[...])
    h2_ref[...] = h2
    n_exp = br_ref.shape[0]
    h2_hi = h2.astype(BF16)
    h2_lo = (h2 - h2_hi.astype(F32)).astype(BF16)
    both = _dot(h2_hi, wr_ref[...])
    lg = both[:, 0:LANES] + both[:, LANES:2 * LANES] + _dot(h2_lo, wr_ref[:, 0:LANES])
    logits = lg.T[0:n_exp, :] + br_ref[...]
    eiota = lax.broadcasted_iota(I32, (n_exp, tm), 0)
    vals, idxs = [], []
    l = logits
    for _ in range(TOP_K):
        m = jnp.max(l, axis=0, keepdims=True)
        sel = jnp.min(jnp.where(l == m, eiota, n_exp), axis=0, keepdims=True)
        vals.append(m)
        idxs.append(sel)
        l = jnp.where(eiota == sel, -jnp.inf, l)
    exps = [jnp.exp(v - vals[0]) for v in vals]
    denom = exps[0] + exps[1] + exps[2] + exps[3]
    inv_denom = 1.0 / denom
    onehot = jnp.zeros((n_exp, tm), F32)
    for kk in range(TOP_K):
        onehot = onehot + (eiota == idxs[kk]).astype(F32)
    base = _dot(onehot.astype(BF16), tri_ref[...]) + run_ref[:, 0:1]
    tiota = lax.broadcasted_iota(I32, (1, tm), 1)
    for kk in range(TOP_K):
        idx_ref[kk:kk + 1, :] = idxs[kk]
        w_ref[kk:kk + 1, :] = exps[kk] * inv_denom
        rank = jnp.sum(jnp.where(eiota == idxs[kk], base, 0.0), axis=0, keepdims=True).astype(I32)
        rank_ref[kk:kk + 1, :] = rank
        dvm_ref[slot, kk:kk + 1, :] = jnp.where(rank < EXPERT_CAP, idxs[kk] * EXPERT_CAP + rank,
                                                trash + kk * tm + tiota)
    dest_copy(slot).start()
    trun_ref[...] = run_ref[...].astype(I32)
    run = run_ref[...] + jnp.sum(onehot, axis=1, keepdims=True)
    run_ref[...] = run
    cnt_ref[...] = run.astype(I32)

    hbuf_ref[slot] = h2
    wait_tile_rows()

    @pl.when(i == pl.num_programs(0) - 1)
    def _():
        cvm_ref[...] = run.astype(I32)
        counts_copy = pltpu.make_async_copy(cvm_ref, csm_ref, csem)
        counts_copy.start()
        zbuf_ref[...] = jnp.zeros(zbuf_ref.shape, F32)
        counts_copy.wait()

        def zero_start(e, c):
            for cond, cp in _zero_unused_capacity(e, csm_ref[e, 0], zbuf_ref, xs_hbm, zsem):
                @pl.when(cond)
                def _():
                    cp.start()
            return c

        def zero_wait(e, c):
            for cond, cp in _zero_unused_capacity(e, csm_ref[e, 0], zbuf_ref, xs_hbm, zsem):
                @pl.when(cond)
                def _():
                    cp.wait()
            return c

        lax.fori_loop(0, N_EXPERTS, zero_start, 0)

        for s_ in (prev, slot):
            dest_copy(s_).wait()

            def start(j, c):
                ts = [j * ROW_UNROLL + u for u in range(ROW_UNROLL)]
                dests = [[dsm_ref[s_, kk, t] for kk in range(TOP_K)] for t in ts]
                for t, dest in zip(ts, dests):
                    for kk in range(TOP_K):
                        row_copy(s_, t, dest[kk]).start(priority=kk % 2)
                return c

            lax.fori_loop(0, tm // ROW_UNROLL, start, 0)
            wait_tile_rows()

        lax.fori_loop(0, N_EXPERTS, zero_wait, 0)


def _split_router(w_router):
    d, e = w_router.shape
    hi = w_router.astype(BF16)
    lo = (w_router - hi.astype(F32)).astype(BF16)
    pad = jnp.zeros((d, LANES - e), BF16)
    return jnp.concatenate([hi, pad, lo, pad], axis=1)


def _token_mix(x2d, cos, sin, g_mix, w_in, g_ret, w_pool, pool_scale, w_branch, w_out,
               g_ffn, w_router, b_router, seq_len):
    T, D = x2d.shape
    tm = min(MIX_TILE, seq_len)
    in_width = w_in.shape[1]
    ret_width = RET_HEADS * RET_HEAD_DIM
    assert T // tm >= DISPATCH_LAG_SLOTS - 1
    assert POOL_WINDOWS == (2, 4, 8, 16)
    dmat, xi_b, zeta_b, chunk_decay = _retention_constants(tm)
    tri = jnp.asarray(np.triu(np.ones((tm, tm), np.float32), 1), BF16)
    row = lambda a: a.reshape(1, -1)
    tile_spec = lambda w: pl.BlockSpec((tm, w), lambda i: (i, 0))
    top_spec = pl.BlockSpec((TOP_K, tm), lambda i: (0, i))
    n_xs = max(N_EXPERTS * EXPERT_CAP, (T * TOP_K // EXPERT_TILE + N_EXPERTS) * EXPERT_TILE)
    body = functools.partial(_mix_body, tiles_per_seq=seq_len // tm, chunk_decay=chunk_decay, trash=n_xs)
    return pl.pallas_call(
        body,
        grid=(T // tm,),
        in_specs=[tile_spec(D), tile_spec(RET_HEAD_DIM), tile_spec(RET_HEAD_DIM),
                  _const_spec((1, D)), _const_spec((D, in_width)),
                  _const_spec(dmat.shape), _const_spec(xi_b.shape), _const_spec(zeta_b.shape),
                  _const_spec((1, ret_width)), _const_spec(w_pool.shape), _const_spec((1, D)),
                  _const_spec(w_branch.shape), _const_spec((D, D)), _const_spec((1, D)),
                  _const_spec((D, 2 * LANES)), _const_spec((N_EXPERTS, 1)), _const_spec((tm, tm))],
        out_specs=[tile_spec(D), tile_spec(D), top_spec, top_spec, top_spec,
                   pl.BlockSpec((N_EXPERTS, LANES), lambda i: (0, 0)),
                   pl.BlockSpec((None, N_EXPERTS, LANES), lambda i: (i, 0, 0)),
                   pl.BlockSpec(memory_space=pl.ANY)],
        out_shape=[jax.ShapeDtypeStruct((T, D), F32), jax.ShapeDtypeStruct((T, D), F32),
                   jax.ShapeDtypeStruct((TOP_K, T), I32), jax.ShapeDtypeStruct((TOP_K, T), F32),
                   jax.ShapeDtypeStruct((TOP_K, T), I32),
                   jax.ShapeDtypeStruct((N_EXPERTS, LANES), I32),
                   jax.ShapeDtypeStruct((T // tm, N_EXPERTS, LANES), I32),
                   jax.ShapeDtypeStruct((n_xs + TOP_K * tm, D), F32)],
        scratch_shapes=[pltpu.VMEM((RET_HEADS, RET_HEAD_DIM, RET_HEAD_DIM), F32),
                        pltpu.VMEM((tm + POOL_PAD + POOL_HALO, D), F32),
                        pltpu.VMEM((3, tm + POOL_PAD + POOL_HALO, D), F32),
                        pltpu.VMEM((N_EXPERTS, LANES), F32),
                        pltpu.VMEM((DISPATCH_LAG_SLOTS, tm, D), F32),
                        pltpu.VMEM((DISPATCH_LAG_SLOTS, TOP_K, tm), I32),
                        pltpu.SMEM((DISPATCH_LAG_SLOTS, TOP_K, tm), I32),
                        pltpu.VMEM((N_EXPERTS, LANES), I32), pltpu.SMEM((N_EXPERTS, LANES), I32),
                        pltpu.VMEM((EXPERT_TILE, D), F32),
                        pltpu.SemaphoreType.DMA, pltpu.SemaphoreType.DMA((DISPATCH_LAG_SLOTS,)),
                        pltpu.SemaphoreType.DMA, pltpu.SemaphoreType.DMA],
        compiler_params=pltpu.CompilerParams(dimension_semantics=("arbitrary",),
                                             vmem_limit_bytes=VMEM_LIMIT_BYTES),
        name="token_mix",
    )(x2d, cos, sin, row(g_mix), w_in.astype(BF16), dmat, xi_b, zeta_b, row(g_ret),
      w_pool.astype(BF16), row(pool_scale), w_branch.astype(BF16), w_out.astype(BF16),
      row(g_ffn), _split_router(w_router), b_router.reshape(-1, 1), tri)


def _fixup_body(flag_ref, zstart_ref, dest_ref, h2_hbm, xs_in, xs_ref, hbuf_ref, zbuf_ref,
                sem, zsem, hsem):
    del xs_in
    td = hbuf_ref.shape[0]
    te = zbuf_ref.shape[0]
    i = pl.program_id(0)
    overflow = flag_ref[0] != 0

    @pl.when((i == 0) & overflow)
    def _():
        zbuf_ref[...] = jnp.zeros(zbuf_ref.shape, F32)

    def zero_copy(e):
        start = pl.multiple_of(jnp.maximum(zstart_ref[e], 0), te)
        return pltpu.make_async_copy(zbuf_ref, xs_ref.at[pl.ds(start, te)], zsem)

    @pl.when((i == 0) & overflow)
    def _():
        def start(e, c):
            @pl.when(zstart_ref[e] >= 0)
            def _():
                zero_copy(e).start()
            return c

        def wait(e, c):
            @pl.when(zstart_ref[e] >= 0)
            def _():
                zero_copy(e).wait()
            return c

        lax.fori_loop(0, zstart_ref.shape[0], start, 0)
        lax.fori_loop(0, zstart_ref.shape[0], wait, 0)

    @pl.when(overflow)
    def _():
        tile = pltpu.make_async_copy(h2_hbm.at[pl.ds(i * td, td)], hbuf_ref, hsem)
        tile.start()
        tile.wait()

        def start(j, c):
            ts = [j * ROW_UNROLL + u for u in range(ROW_UNROLL)]
            dests = [[dest_ref[kk, t] for kk in range(TOP_K)] for t in ts]
            for t, dest in zip(ts, dests):
                for kk in range(TOP_K):
                    pltpu.make_async_copy(hbuf_ref.at[pl.ds(t, 1)], xs_ref.at[pl.ds(dest[kk], 1)],
                                          sem).start(priority=kk % 2)
            return c

        lax.fori_loop(0, td // ROW_UNROLL, start, 0)
        for kk in range(TOP_K):
            pltpu.make_async_copy(hbuf_ref, xs_ref.at[pl.ds(0, td)], sem).wait()


def _moe_dispatch_fixup(overflow, zstart, dest_t, h2, xs):
    T, D = h2.shape
    td = min(DISP_TILE, T)
    smem_spec = pl.BlockSpec((TOP_K, td), lambda i, *_: (0, i), memory_space=pltpu.SMEM)
    return pl.pallas_call(
        _fixup_body,
        grid_spec=pltpu.PrefetchScalarGridSpec(
            num_scalar_prefetch=2,
            grid=(T // td,),
            in_specs=[smem_spec, pl.BlockSpec(memory_space=pl.ANY), pl.BlockSpec(memory_space=pl.ANY)],
            out_specs=pl.BlockSpec(memory_space=pl.ANY),
            scratch_shapes=[pltpu.VMEM((td, D), F32), pltpu.VMEM((EXPERT_TILE, D), F32),
                            pltpu.SemaphoreType.DMA, pltpu.SemaphoreType.DMA, pltpu.SemaphoreType.DMA]),
        out_shape=jax.ShapeDtypeStruct(xs.shape, xs.dtype),
        input_output_aliases={4: 0},
        compiler_params=pltpu.CompilerParams(dimension_semantics=("arbitrary",),
                                             vmem_limit_bytes=VMEM_LIMIT_BYTES),
        name="moe_dispatch_fixup",
    )(overflow, zstart, dest_t, h2, xs)


def _expert_body(blk_e_ref, blk_src_ref, nused_ref, first_ref, next_e_ref, xs_ref, wgu_hbm, bgu_ref,
                 wd_hbm, bd_ref, yb_ref, wgu_f32_ref, wd_f32_ref, wgu_bf_ref, wd_bf_ref, wsem):
    i = pl.program_id(0)
    d_ff = wd_bf_ref.shape[0]

    def weight_copies(e):
        return (pltpu.make_async_copy(wgu_hbm.at[pl.ds(e, 1)], wgu_f32_ref, wsem.at[0]),
                pltpu.make_async_copy(wd_hbm.at[pl.ds(e, 1)], wd_f32_ref, wsem.at[1]))

    @pl.when(i == 0)
    def _():
        for cp in weight_copies(blk_e_ref[0]):
            cp.start()

    @pl.when(first_ref[i] != 0)
    def _():
        for cp in weight_copies(blk_e_ref[i]):
            cp.wait()
        wgu_bf_ref[...] = wgu_f32_ref[0].astype(BF16)
        wd_bf_ref[...] = wd_f32_ref[0].astype(BF16)

    @pl.when((first_ref[i] != 0) & (next_e_ref[i] >= 0))
    def _():
        for cp in weight_copies(next_e_ref[i]):
            cp.start(priority=1)

    @pl.when(i < nused_ref[0])
    def _():
        gu = _dot(xs_ref[...].astype(BF16), wgu_bf_ref[...]) + bgu_ref[...]
        gate = jnp.minimum(gu[:, 0:d_ff], SWIGLU_LIMIT)
        up = jnp.clip(gu[:, d_ff:2 * d_ff], -SWIGLU_LIMIT, SWIGLU_LIMIT)
        act = (up + 1.0) * (gate * jax.nn.sigmoid(gate * SWIGLU_ALPHA))
        yb_ref[...] = _dot(act.astype(BF16), wd_bf_ref[...]) + bd_ref[...]

    @pl.when(i >= nused_ref[0])
    def _():
        yb_ref[...] = jnp.zeros(yb_ref.shape, F32)


def _moe_experts(xs, blk_e, blk_src, nused, first, next_e, w_gate_up, b_gate_up, w_down, b_down):
    D = xs.shape[1]
    E, _, two_ff = w_gate_up.shape
    d_ff = two_ff // 2
    te = EXPERT_TILE
    n_rows = blk_e.shape[0] * te
    return pl.pallas_call(
        _expert_body,
        grid_spec=pltpu.PrefetchScalarGridSpec(
            num_scalar_prefetch=5,
            grid=(n_rows // te,),
            in_specs=[pl.BlockSpec((te, D), lambda i, be, bs, *_: (bs[i], 0)),
                      pl.BlockSpec(memory_space=pl.ANY),
                      pl.BlockSpec((None, 1, two_ff), lambda i, be, *_: (be[i], 0, 0)),
                      pl.BlockSpec(memory_space=pl.ANY),
                      pl.BlockSpec((None, 1, D), lambda i, be, *_: (be[i], 0, 0))],
            out_specs=pl.BlockSpec((te, D), lambda i, *_: (i, 0)),
            scratch_shapes=[pltpu.VMEM((1, D, two_ff), F32), pltpu.VMEM((1, d_ff, D), F32),
                            pltpu.VMEM((D, two_ff), BF16), pltpu.VMEM((d_ff, D), BF16),
                            pltpu.SemaphoreType.DMA((2,))]),
        out_shape=jax.ShapeDtypeStruct((n_rows, D), F32),
        compiler_params=pltpu.CompilerParams(dimension_semantics=("arbitrary",),
                                             vmem_limit_bytes=VMEM_LIMIT_BYTES),
        name="moe_experts",
    )(blk_e, blk_src, nused, first, next_e, xs, w_gate_up, b_gate_up.reshape(E, 1, two_ff),
      w_down, b_down.reshape(E, 1, D))


def _comb_body(ws_ref, ok_ref, dest_ref, destn_ref, yb_ref, pos_ref, w_ref, x1_ref, p_ref,
               gple_ref, wpg_ref, wpp_ref, gfin_ref, out_ref, gbuf_ref, sems):
    tc = x1_ref.shape[0]
    n_rows = gbuf_ref.shape[1]
    i = pl.program_id(0)
    slot = i % 2

    def window_copy(tile, e, s):
        src = pl.multiple_of(ws_ref[tile * N_EXPERTS + e], SUBLANES)
        return pltpu.make_async_copy(yb_ref.at[pl.ds(src, COMB_WINDOW)],
                                     gbuf_ref.at[s, pl.ds(e * COMB_WINDOW, COMB_WINDOW)], sems.at[s])

    def issue(tile, dest_r, s):
        @pl.when(ok_ref[tile] != 0)
        def _():
            for e in range(N_EXPERTS):
                window_copy(tile, e, s).start(priority=e % 2)

        @pl.when(ok_ref[tile] == 0)
        def _():
            def start(j, c):
                ts = [j * ROW_UNROLL + u for u in range(ROW_UNROLL)]
                srcs = [[dest_r[kk, t] for kk in range(TOP_K)] for t in ts]
                for t, src in zip(ts, srcs):
                    for kk in range(TOP_K):
                        pltpu.make_async_copy(yb_ref.at[pl.ds(src[kk], 1)],
                                              gbuf_ref.at[s, pl.ds(kk * tc + t, 1)],
                                              sems.at[s]).start(priority=kk % 2)
                return c

            lax.fori_loop(0, tc // ROW_UNROLL, start, 0)

    @pl.when(i == 0)
    def _():
        issue(0, dest_ref, 0)

    @pl.when(i + 1 < pl.num_programs(0))
    def _():
        issue(i + 1, destn_ref, 1 - slot)

    def to_cols(rows):
        pad = jnp.zeros((LANES - rows.shape[0], tc), F32)
        return jnp.concatenate([rows, pad], axis=0).T

    w_col = to_cols(w_ref[...])
    fast = ok_ref[i] != 0
    slow_pos = (lax.broadcasted_iota(I32, (TOP_K, tc), 0) * tc
                + lax.broadcasted_iota(I32, (TOP_K, tc), 1))
    pos_col = to_cols(jnp.where(fast, pos_ref[...], slow_pos).astype(F32))
    pp = _dot(p_ref[...].astype(BF16), wpp_ref[...])

    @pl.when(fast)
    def _():
        pltpu.make_async_copy(yb_ref.at[pl.ds(0, n_rows)], gbuf_ref.at[slot], sems.at[slot]).wait()

    @pl.when(jnp.logical_not(fast))
    def _():
        pltpu.make_async_copy(yb_ref.at[pl.ds(0, TOP_K * tc)], gbuf_ref.at[slot, pl.ds(0, TOP_K * tc)],
                              sems.at[slot]).wait()
        gbuf_ref[slot, pl.ds(TOP_K * tc, n_rows - TOP_K * tc), :] = jnp.zeros(
            (n_rows - TOP_K * tc, gbuf_ref.shape[2]), F32)

    ciota = lax.broadcasted_iota(I32, (tc, n_rows), 1).astype(F32)
    sel = jnp.zeros((tc, n_rows), F32)
    for kk in range(TOP_K):
        sel = jnp.where(ciota == pos_col[:, kk:kk + 1], w_col[:, kk:kk + 1], sel)
    moe = _dot(sel.astype(BF16), gbuf_ref[slot].astype(BF16))
    x2 = x1_ref[...] + moe
    h3 = _rms(x2, gple_ref[...]).astype(BF16)
    gate = jax.nn.sigmoid(_dot(h3, wpg_ref[...]))
    x3 = x2 + gate * pp
    out_ref[...] = _rms(x3, gfin_ref[...])


def _moe_combine(ws, ok, dest_t, pos_t, yb, w_t, x1, p2d, g_ple, w_ple_gate, w_ple_proj, g_final):
    T, D = x1.shape
    tc = min(COMB_TILE, T)
    n_tiles = T // tc
    ple = p2d.shape[1]
    smem_spec = pl.BlockSpec((TOP_K, tc), lambda i, *_: (0, i), memory_space=pltpu.SMEM)
    smem_next = pl.BlockSpec((TOP_K, tc), lambda i, *_: (0, jnp.minimum(i + 1, n_tiles - 1)),
                             memory_space=pltpu.SMEM)
    top_spec = pl.BlockSpec((TOP_K, tc), lambda i, *_: (0, i))
    const = lambda shape: pl.BlockSpec(shape, lambda i, *_: (0,) * len(shape),
                                       pipeline_mode=pl.Buffered(1))
    row = lambda a: a.reshape(1, -1)
    return pl.pallas_call(
        _comb_body,
        grid_spec=pltpu.PrefetchScalarGridSpec(
            num_scalar_prefetch=2,
            grid=(n_tiles,),
            in_specs=[smem_spec, smem_next,
                      pl.BlockSpec(memory_space=pl.ANY),
                      top_spec, top_spec,
                      pl.BlockSpec((tc, D), lambda i, *_: (i, 0)),
                      pl.BlockSpec((tc, ple), lambda i, *_: (i, 0)),
                      const((1, D)), const((D, D)), const((ple, D)), const((1, D))],
            out_specs=pl.BlockSpec((tc, D), lambda i, *_: (i, 0)),
            scratch_shapes=[pltpu.VMEM((2, N_EXPERTS * COMB_WINDOW, D), F32),
                            pltpu.SemaphoreType.DMA((2,))]),
        out_shape=jax.ShapeDtypeStruct((T, D), F32),
        compiler_params=pltpu.CompilerParams(dimension_semantics=("arbitrary",),
                                             vmem_limit_bytes=VMEM_LIMIT_BYTES),
        name="moe_combine",
    )(ws, ok, dest_t, dest_t, yb, pos_t, w_t, x1, p2d, row(g_ple), w_ple_gate.astype(BF16),
      w_ple_proj.astype(BF16), row(g_final))


def _combine_windows(pstart, trun, counts, idx_t, dest_t, tile, n_rows):
    n_tiles = trun.shape[0]
    tcnt = jnp.concatenate([trun[1:], counts[None, :]], axis=0) - trun
    start = pstart[None, :] + trun
    ws = jnp.minimum(start // SUBLANES * SUBLANES, n_rows - COMB_WINDOW).astype(I32)
    ok = jnp.all(start + tcnt - ws <= COMB_WINDOW, axis=1).astype(I32)
    ws_tok = _select_expert(jnp.repeat(ws, tile, axis=0)[None], idx_t)
    pos_t = idx_t * COMB_WINDOW + dest_t - ws_tok
    return ws.reshape(-1), ok, pos_t.astype(I32)


def _select_expert(table, idx_t):
    onehot = idx_t[:, :, None] == jnp.arange(N_EXPERTS, dtype=I32)
    return jnp.sum(jnp.where(onehot, table, 0), axis=-1)


def _group_layout(counts, n_blocks):
    te = EXPERT_TILE
    padded = (counts + te - 1) // te * te
    pends = jnp.cumsum(padded)
    pstart = (pends - padded).astype(I32)
    nused = (pends[-1] // te).astype(I32)
    tail = nused + jnp.arange(N_EXPERTS, dtype=I32)
    zstart = jnp.concatenate([jnp.where(padded > 0, pends - te, -1),
                              jnp.where(tail < n_blocks, tail * te, -1)]).astype(I32)
    blk_src = jnp.minimum(jnp.arange(n_blocks, dtype=I32), jnp.maximum(nused - 1, 0))
    blk_e = jnp.sum((blk_src * te)[:, None] >= pends[None, :], axis=1)
    blk_e = jnp.clip(blk_e, 0, N_EXPERTS - 1).astype(I32)
    blk = jnp.arange(n_blocks, dtype=I32)
    prev_e = jnp.concatenate([blk_e[:1], blk_e[:-1]])
    first = ((blk < nused) & ((blk == 0) | (blk_e != prev_e))).astype(I32)
    experts = jnp.arange(N_EXPERTS, dtype=I32)[None, :]
    later = (experts > blk_e[:, None]) & (counts[None, :] > 0)
    next_e = jnp.min(jnp.where(later, experts, N_EXPERTS), axis=1)
    next_e = jnp.where(next_e < N_EXPERTS, next_e, -1).astype(I32)
    overflow = jnp.any(counts > EXPERT_CAP)
    pstart_blk = jnp.sum(jnp.where(blk_e[:, None] == experts, pstart[None, :], 0), axis=1)
    blk_cap = blk_e * (EXPERT_CAP // te) + (blk_src - pstart_blk // te)
    blk_src = jnp.where(overflow, blk_src, blk_cap).astype(I32)
    return (pstart, zstart, blk_e, blk_src, nused.reshape(1), first, next_e,
            overflow.astype(I32).reshape(1))


def kernel(x, p, positions, g_mix_norm, w_in, g_ret_norm, w_pool, pool_scale, w_branch, w_out,
           g_ffn_norm, w_router, b_router, w_gate_up, b_gate_up, w_down, b_down,
           g_ple_norm, w_ple_gate, w_ple_proj, g_final):
    B, S, D = x.shape
    depth = w_in.shape[0]
    T = B * S
    xt = x.reshape(T, D)
    cos, sin = _rope_tables(positions.reshape(T, 1))
    n_blocks = (T * TOP_K) // EXPERT_TILE + N_EXPERTS
    for i in range(depth):
        x1, h2, idx_t, w_t, rank_t, cnt, trun, xs = _token_mix(
            xt, cos, sin, g_mix_norm[i], w_in[i], g_ret_norm[i], w_pool[i], pool_scale[i],
            w_branch[i], w_out[i], g_ffn_norm[i], w_router[i], b_router[i], S)
        (pstart, zstart, blk_e, blk_src, nused, first, next_e,
         overflow) = _group_layout(cnt[:, 0], n_blocks)
        dest_t = _select_expert(pstart[None, None, :], idx_t) + rank_t
        xs = _moe_dispatch_fixup(overflow, zstart, dest_t, h2, xs)
        yb = _moe_experts(xs, blk_e, blk_src, nused, first, next_e,
                          w_gate_up[i], b_gate_up[i], w_down[i], b_down[i])
        assert depth == 1
        assert min(MIX_TILE, S) == min(COMB_TILE, T)
        ws, ok, pos_t = _combine_windows(pstart, trun[:, :, 0], cnt[:, 0], idx_t, dest_t,
                                         min(COMB_TILE, T), n_blocks * EXPERT_TILE)
        xt = _moe_combine(ws, ok, dest_t, pos_t, yb, w_t, x1, p[i].reshape(T, -1), g_ple_norm[i],
                          w_ple_gate[i], w_ple_proj[i], g_final)
    return xt.reshape(B, S, D)
```

```python
import functools

import numpy as np
import jax
import jax.numpy as jnp
from jax import lax
from jax.experimental import pallas as pl
from jax.experimental.pallas import tpu as pltpu

F32 = jnp.float32
BF16 = jnp.bfloat16
I32 = jnp.int32

RET_HEADS = 8
RET_HEAD_DIM = 128
ROPE_BASE = 10000.0
GN_EPS = 1e-5
RMS_EPS = 1e-6
POOL_WINDOWS = (2, 4, 8, 16)
N_EXPERTS = 32
TOP_K = 4
SWIGLU_ALPHA = 1.702
SWIGLU_LIMIT = 7.0

LANES = 128
SUBLANES = 8
VMEM_LIMIT_BYTES = 56 * 1024 * 1024

MIX_TILE = 256
ROPE_TILE = 1024
DISP_TILE = 2048
EXPERT_CAP = 2560
DISPATCH_LAG_SLOTS = 3
EXPERT_TILE = 256
COMB_TILE = 256
POOL_HALO = 16
POOL_PAD = 8
COMB_WINDOW = 64
ROW_UNROLL = 8


def _const_spec(shape):
    nd = len(shape)
    return pl.BlockSpec(shape, lambda *_: (0,) * nd, pipeline_mode=pl.Buffered(1))


def _zero_unused_capacity(e, cnt, zbuf_ref, xs_ref, zsem):
    te = zbuf_ref.shape[0]
    cnt = jnp.minimum(cnt, EXPERT_CAP)
    first = e * EXPERT_CAP + cnt
    n = (-cnt) % te
    head = jnp.minimum(n, (-first) % SUBLANES)
    out = []
    for r in range(SUBLANES - 1):
        out.append((r < head, pltpu.make_async_copy(zbuf_ref.at[pl.ds(0, 1)],
                                                    xs_ref.at[pl.ds(first + r, 1)], zsem)))
    rest = n - head
    pos = first + head
    size = te // 2
    while size >= SUBLANES:
        at = pl.multiple_of(pos + (rest & ~(2 * size - 1)), SUBLANES)
        out.append(((rest & size) != 0,
                    pltpu.make_async_copy(zbuf_ref.at[pl.ds(0, size)], xs_ref.at[pl.ds(at, size)], zsem)))
        size //= 2
    used_end = first + n
    for b in range(EXPERT_CAP // te):
        at = pl.multiple_of(used_end + b * te, te)
        out.append((at < (e + 1) * EXPERT_CAP,
                    pltpu.make_async_copy(zbuf_ref, xs_ref.at[pl.ds(at, te)], zsem)))
    return out


def _rms(x, g):
    return x * lax.rsqrt(jnp.mean(x * x, axis=-1, keepdims=True) + RMS_EPS) * g


def _dot(a, b):
    return jnp.dot(a, b, preferred_element_type=F32)


def _dot_nt(a, b, precision=None):
    return lax.dot_general(a, b, (((1,), (1,)), ((), ())),
                           preferred_element_type=F32, precision=precision)


def _dot_tn(a, b):
    return lax.dot_general(a, b, (((0,), (0,)), ((), ())), preferred_element_type=F32)


def _rope_body(pos_ref, inv_ref, sign_ref, cos_ref, sin_ref):
    half_rows = pos_ref.shape[0] // 2
    half = RET_HEAD_DIM // 2
    lane = lax.broadcasted_iota(I32, (half_rows, RET_HEAD_DIM), 1)
    first = lane < half
    pos = jnp.where(first, pos_ref[pl.ds(0, half_rows, stride=2), :],
                    pos_ref[pl.ds(1, half_rows, stride=2), :]).astype(F32)
    ang = pos * inv_ref[...]
    cos = jnp.cos(ang)
    sin = jnp.sin(ang)
    cos_sw = pltpu.roll(cos, half, 1)
    sin_sw = pltpu.roll(sin, half, 1)
    sign = sign_ref[...]
    cos_ref[pl.ds(0, half_rows, stride=2), :] = jnp.where(first, cos, cos_sw)
    cos_ref[pl.ds(1, half_rows, stride=2), :] = jnp.where(first, cos_sw, cos)
    sin_ref[pl.ds(0, half_rows, stride=2), :] = jnp.where(first, sin, sin_sw) * sign
    sin_ref[pl.ds(1, half_rows, stride=2), :] = jnp.where(first, sin_sw, sin) * sign


def _rope_tables(pos_col):
    T = pos_col.shape[0]
    half = RET_HEAD_DIM // 2
    inv = ROPE_BASE ** (-jnp.arange(half, dtype=F32) / half)
    inv_full = jnp.concatenate([inv, inv]).reshape(1, RET_HEAD_DIM)
    sign = jnp.concatenate([-jnp.ones((half,), F32), jnp.ones((half,), F32)]).reshape(1, RET_HEAD_DIM)
    tile = min(ROPE_TILE, T)
    return pl.pallas_call(
        _rope_body,
        grid=(T // tile,),
        in_specs=[pl.BlockSpec((tile, 1), lambda i: (i, 0)),
                  pl.BlockSpec((1, RET_HEAD_DIM), lambda i: (0, 0)),
                  pl.BlockSpec((1, RET_HEAD_DIM), lambda i: (0, 0))],
        out_specs=[pl.BlockSpec((tile, RET_HEAD_DIM), lambda i: (i, 0)),
                   pl.BlockSpec((tile, RET_HEAD_DIM), lambda i: (i, 0))],
        out_shape=[jax.ShapeDtypeStruct((T, RET_HEAD_DIM), F32),
                   jax.ShapeDtypeStruct((T, RET_HEAD_DIM), F32)],
        name="rope_tables",
    )(pos_col, inv_full, sign)


def _retention_constants(tile):
    h = np.arange(RET_HEADS, dtype=np.float64)
    log_gamma = np.log1p(-np.exp2(-5.0 - h))
    idx = np.arange(tile, dtype=np.float64)
    diff = idx[:, None] - idx[None, :]
    dmat = np.where(diff >= 0, np.exp(log_gamma[:, None, None] * np.maximum(diff, 0.0)[None]), 0.0)
    xi = np.exp(log_gamma[:, None] * (idx + 1.0)[None])
    zeta = np.exp(log_gamma[:, None] * (tile - 1.0 - idx)[None])
    chunk_decay = np.exp(log_gamma * tile)
    xi_b = np.broadcast_to(xi[:, :, None], (RET_HEADS, tile, RET_HEAD_DIM))
    zeta_b = np.broadcast_to(zeta[:, :, None], (RET_HEADS, tile, RET_HEAD_DIM))
    return (jnp.asarray(dmat, F32), jnp.asarray(xi_b, F32), jnp.asarray(zeta_b, F32),
            tuple(float(c) for c in chunk_decay))


def _mix_body(x_ref, cos_ref, sin_ref, gmix_ref, win_ref, dmat_ref, xi_ref, zeta_ref, gret_ref,
              wpool_ref, pscale_ref, wbr_ref, wout_ref, gffn_ref, wr_ref, br_ref, tri_ref,
              x1_ref, h2_ref, idx_ref, w_ref, rank_ref, cnt_ref, trun_ref, xs_hbm,
              state_ref, ue_ref, lv_ref, run_ref, hbuf_ref, dvm_ref, dsm_ref, cvm_ref, csm_ref,
              zbuf_ref, rsem, dsem, csem, zsem, *, tiles_per_seq, chunk_decay, trash):
    tm, d_model = x_ref.shape
    ret_width = RET_HEADS * RET_HEAD_DIM
    i = pl.program_id(0)
    seq_tile = i % tiles_per_seq
    slot = i % DISPATCH_LAG_SLOTS
    prev = (i + DISPATCH_LAG_SLOTS - 1) % DISPATCH_LAG_SLOTS
    old = (i + DISPATCH_LAG_SLOTS - 2) % DISPATCH_LAG_SLOTS

    def row_copy(src_slot, t, dest):
        return pltpu.make_async_copy(hbuf_ref.at[src_slot, pl.ds(t, 1)], xs_hbm.at[pl.ds(dest, 1)], rsem)

    def wait_tile_rows():
        for _ in range(TOP_K):
            pltpu.make_async_copy(hbuf_ref.at[0], xs_hbm.at[pl.ds(0, tm)], rsem).wait()

    def dest_copy(s_):
        return pltpu.make_async_copy(dvm_ref.at[s_], dsm_ref.at[s_], dsem.at[s_])

    @pl.when(seq_tile == 0)
    def _():
        state_ref[...] = jnp.zeros(state_ref.shape, F32)
        ue_ref[0:POOL_PAD + POOL_HALO, :] = jnp.zeros((POOL_PAD + POOL_HALO, ue_ref.shape[1]), F32)

    @pl.when(i == 0)
    def _():
        run_ref[...] = jnp.zeros(run_ref.shape, F32)
        lv_ref[:, 0:POOL_PAD, :] = jnp.zeros((lv_ref.shape[0], POOL_PAD, lv_ref.shape[2]), F32)
        for s_ in range(1, DISPATCH_LAG_SLOTS):
            hbuf_ref[s_] = jnp.zeros(hbuf_ref.shape[1:], F32)
            for kk in range(TOP_K):
                dvm_ref[s_, kk:kk + 1, :] = trash + kk * tm + lax.broadcasted_iota(I32, (1, tm), 1)
            dest_copy(s_).start()

    dest_copy(old).wait()

    x = x_ref[...]
    hb = _rms(x, gmix_ref[...]).astype(BF16)

    qkvg = _dot(hb, win_ref[:, 0:4 * ret_width])
    cos = cos_ref[...]
    sin = sin_ref[...]

    def rot(a):
        return a * cos + pltpu.roll(a, RET_HEAD_DIM // 2, 1) * sin

    ys = []
    for h in range(RET_HEADS):
        lo = h * RET_HEAD_DIM
        q = rot(qkvg[:, lo:lo + RET_HEAD_DIM]).astype(BF16)
        k = (rot(qkvg[:, ret_width + lo:ret_width + lo + RET_HEAD_DIM])
             * (RET_HEAD_DIM ** -0.5)).astype(BF16)
        v = qkvg[:, 2 * ret_width + lo:2 * ret_width + lo + RET_HEAD_DIM]
        g = qkvg[:, 3 * ret_width + lo:3 * ret_width + lo + RET_HEAD_DIM]
        scores = _dot_nt(q, k) * dmat_ref[h]
        inner = _dot(scores.astype(BF16), v.astype(BF16))
        st = state_ref[h]
        cross = _dot(q, st.astype(BF16)) * xi_ref[h]
        state_ref[h] = st * chunk_decay[h] + _dot_tn(k, (v * zeta_ref[h]).astype(BF16))
        ret = inner + cross
        mu = jnp.mean(ret, axis=-1, keepdims=True)
        dev = ret - mu
        var = jnp.mean(dev * dev, axis=-1, keepdims=True)
        rn = dev * lax.rsqrt(var + GN_EPS) * gret_ref[:, lo:lo + RET_HEAD_DIM]
        ys.append(((g * jax.nn.sigmoid(g)) * rn).astype(BF16))
        per_head = tm // RET_HEADS
        for t in range(h * per_head, (h + 1) * per_head):
            for kk in range(TOP_K):
                row_copy(old, t, dsm_ref[old, kk, t]).start(priority=kk % 2)
    y_ret = jnp.concatenate(ys, axis=1)
    branch_a = _dot(y_ret, wbr_ref[0])

    u = _dot(hb, win_ref[:, 4 * ret_width:4 * ret_width + d_model])
    top = POOL_PAD + POOL_HALO
    ext = tm + POOL_HALO
    group = d_model // len(POOL_WINDOWS)
    ue_ref[top:top + tm, :] = u
    s2 = ue_ref[POOL_PAD:POOL_PAD + ext, :] + ue_ref[POOL_PAD - 1:POOL_PAD - 1 + ext, :]
    lv_ref[0, POOL_PAD:POOL_PAD + ext, :] = s2
    s4 = s2[:, group:] + lv_ref[0, POOL_PAD - 2:POOL_PAD - 2 + ext, group:]
    lv_ref[1, POOL_PAD:POOL_PAD + ext, group:] = s4
    s8 = s4[:, group:] + lv_ref[1, POOL_PAD - 4:POOL_PAD - 4 + ext, 2 * group:]
    lv_ref[2, POOL_PAD:POOL_PAD + ext, 2 * group:] = s8
    sums = [lv_ref[0, top:top + tm, 0:group],
            lv_ref[1, top:top + tm, group:2 * group],
            lv_ref[2, top:top + tm, 2 * group:3 * group],
            lv_ref[2, top:top + tm, 3 * group:] + lv_ref[2, top - 8:top - 8 + tm, 3 * group:]]
    ue_ref[POOL_PAD:top, :] = ue_ref[tm + POOL_PAD:tm + top, :]
    pos = seq_tile * tm + lax.broadcasted_iota(I32, (tm, 1), 0)
    outs = []
    for gi, w in enumerate(POOL_WINDOWS):
        ug = u[:, gi * group:(gi + 1) * group]
        inv_count = 1.0 / jnp.minimum(pos + 1, w).astype(F32)
        mixed = (sums[gi] * inv_count - ug).astype(BF16)
        outs.append(_dot(mixed, wpool_ref[gi]))
    y_pool = (jnp.concatenate(outs, axis=1) * pscale_ref[...]).astype(BF16)
    branch_b = _dot(y_pool, wbr_ref[1])

    gates = _dot(hb, win_ref[:, 4 * ret_width + d_model:4 * ret_width + 3 * d_model])
    merged = (jax.nn.sigmoid(gates[:, 0:d_model]) * branch_a
              + jax.nn.sigmoid(gates[:, d_model:2 * d_model]) * branch_b)
    x1 = x + _dot(merged.astype(BF16), wout_ref[...])
    x1_ref[...] = x1

    h2 = _rms(x1, gffn_ref[...])
    h2_ref[...] = h2
    n_exp = br_ref.shape[0]
    h2_hi = h2.astype(BF16)
    h2_lo = (h2 - h2_hi.astype(F32)).astype(BF16)
    both = _dot(h2_hi, wr_ref[...])
    lg = both[:, 0:LANES] + both[:, LANES:2 * LANES] + _dot(h2_lo, wr_ref[:, 0:LANES])
    logits = lg.T[0:n_exp, :] + br_ref[...]
    eiota = lax.broadcasted_iota(I32, (n_exp, tm), 0)
    vals, idxs = [], []
    l = logits
    for _ in range(TOP_K):
        m = jnp.max(l, axis=0, keepdims=True)
        sel = jnp.min(jnp.where(l == m, eiota, n_exp), axis=0, keepdims=True)
        vals.append(m)
        idxs.append(sel)
        l = jnp.where(eiota == sel, -jnp.inf, l)
    exps = [jnp.exp(v - vals[0]) for v in vals]
    denom = exps[0] + exps[1] + exps[2] + exps[3]
    inv_denom = 1.0 / denom
    onehot = jnp.zeros((n_exp, tm), F32)
    for kk in range(TOP_K):
        onehot = onehot + (eiota == idxs[kk]).astype(F32)
    base = _dot(onehot.astype(BF16), tri_ref[...]) + run_ref[:, 0:1]
    tiota = lax.broadcasted_iota(I32, (1, tm), 1)
    for kk in range(TOP_K):
        idx_ref[kk:kk + 1, :] = idxs[kk]
        w_ref[kk:kk + 1, :] = exps[kk] * inv_denom
        rank = jnp.sum(jnp.where(eiota == idxs[kk], base, 0.0), axis=0, keepdims=True).astype(I32)
        rank_ref[kk:kk + 1, :] = rank
        dvm_ref[slot, kk:kk + 1, :] = jnp.where(rank < EXPERT_CAP, idxs[kk] * EXPERT_CAP + rank,
                                                trash + kk * tm + tiota)
    dest_copy(slot).start()
    trun_ref[...] = run_ref[...].astype(I32)
    run = run_ref[...] + jnp.sum(onehot, axis=1, keepdims=True)
    run_ref[...] = run
    cnt_ref[...] = run.astype(I32)

    hbuf_ref[slot] = h2
    wait_tile_rows()

    @pl.when(i == pl.num_programs(0) - 1)
    def _():
        cvm_ref[...] = run.astype(I32)
        counts_copy = pltpu.make_async_copy(cvm_ref, csm_ref, csem)
        counts_copy.start()
        zbuf_ref[...] = jnp.zeros(zbuf_ref.shape, F32)
        counts_copy.wait()

        def zero_start(e, c):
            for cond, cp in _zero_unused_capacity(e, csm_ref[e, 0], zbuf_ref, xs_hbm, zsem):
                @pl.when(cond)
                def _():
                    cp.start()
            return c

        def zero_wait(e, c):
            for cond, cp in _zero_unused_capacity(e, csm_ref[e, 0], zbuf_ref, xs_hbm, zsem):
                @pl.when(cond)
                def _():
                    cp.wait()
            return c

        lax.fori_loop(0, N_EXPERTS, zero_start, 0)

        for s_ in (prev, slot):
            dest_copy(s_).wait()

            def start(j, c):
                ts = [j * ROW_UNROLL + u for u in range(ROW_UNROLL)]
                dests = [[dsm_ref[s_, kk, t] for kk in range(TOP_K)] for t in ts]
                for t, dest in zip(ts, dests):
                    for kk in range(TOP_K):
                        row_copy(s_, t, dest[kk]).start(priority=kk % 2)
                return c

            lax.fori_loop(0, tm // ROW_UNROLL, start, 0)
            wait_tile_rows()

        lax.fori_loop(0, N_EXPERTS, zero_wait, 0)


def _split_router(w_router):
    d, e = w_router.shape
    hi = w_router.astype(BF16)
    lo = (w_router - hi.astype(F32)).astype(BF16)
    pad = jnp.zeros((d, LANES - e), BF16)
    return jnp.concatenate([hi, pad, lo, pad], axis=1)


def _token_mix(x2d, cos, sin, g_mix, w_in, g_ret, w_pool, pool_scale, w_branch, w_out,
               g_ffn, w_router, b_router, seq_len):
    T, D = x2d.shape
    tm = min(MIX_TILE, seq_len)
    in_width = w_in.shape[1]
    ret_width = RET_HEADS * RET_HEAD_DIM
    assert T // tm >= DISPATCH_LAG_SLOTS - 1
    assert POOL_WINDOWS == (2, 4, 8, 16)
    dmat, xi_b, zeta_b, chunk_decay = _retention_constants(tm)
    tri = jnp.asarray(np.triu(np.ones((tm, tm), np.float32), 1), BF16)
    row = lambda a: a.reshape(1, -1)
    tile_spec = lambda w: pl.BlockSpec((tm, w), lambda i: (i, 0))
    top_spec = pl.BlockSpec((TOP_K, tm), lambda i: (0, i))
    n_xs = max(N_EXPERTS * EXPERT_CAP, (T * TOP_K // EXPERT_TILE + N_EXPERTS) * EXPERT_TILE)
    body = functools.partial(_mix_body, tiles_per_seq=seq_len // tm, chunk_decay=chunk_decay, trash=n_xs)
    return pl.pallas_call(
        body,
        grid=(T // tm,),
        in_specs=[tile_spec(D), tile_spec(RET_HEAD_DIM), tile_spec(RET_HEAD_DIM),
                  _const_spec((1, D)), _const_spec((D, in_width)),
                  _const_spec(dmat.shape), _const_spec(xi_b.shape), _const_spec(zeta_b.shape),
                  _const_spec((1, ret_width)), _const_spec(w_pool.shape), _const_spec((1, D)),
                  _const_spec(w_branch.shape), _const_spec((D, D)), _const_spec((1, D)),
                  _const_spec((D, 2 * LANES)), _const_spec((N_EXPERTS, 1)), _const_spec((tm, tm))],
        out_specs=[tile_spec(D), tile_spec(D), top_spec, top_spec, top_spec,
                   pl.BlockSpec((N_EXPERTS, LANES), lambda i: (0, 0)),
                   pl.BlockSpec((None, N_EXPERTS, LANES), lambda i: (i, 0, 0)),
                   pl.BlockSpec(memory_space=pl.ANY)],
        out_shape=[jax.ShapeDtypeStruct((T, D), F32), jax.ShapeDtypeStruct((T, D), F32),
                   jax.ShapeDtypeStruct((TOP_K, T), I32), jax.ShapeDtypeStruct((TOP_K, T), F32),
                   jax.ShapeDtypeStruct((TOP_K, T), I32),
                   jax.ShapeDtypeStruct((N_EXPERTS, LANES), I32),
                   jax.ShapeDtypeStruct((T // tm, N_EXPERTS, LANES), I32),
                   jax.ShapeDtypeStruct((n_xs + TOP_K * tm, D), F32)],
        scratch_shapes=[pltpu.VMEM((RET_HEADS, RET_HEAD_DIM, RET_HEAD_DIM), F32),
                        pltpu.VMEM((tm + POOL_PAD + POOL_HALO, D), F32),
                        pltpu.VMEM((3, tm + POOL_PAD + POOL_HALO, D), F32),
                        pltpu.VMEM((N_EXPERTS, LANES), F32),
                        pltpu.VMEM((DISPATCH_LAG_SLOTS, tm, D), F32),
                        pltpu.VMEM((DISPATCH_LAG_SLOTS, TOP_K, tm), I32),
                        pltpu.SMEM((DISPATCH_LAG_SLOTS, TOP_K, tm), I32),
                        pltpu.VMEM((N_EXPERTS, LANES), I32), pltpu.SMEM((N_EXPERTS, LANES), I32),
                        pltpu.VMEM((EXPERT_TILE, D), F32),
                        pltpu.SemaphoreType.DMA, pltpu.SemaphoreType.DMA((DISPATCH_LAG_SLOTS,)),
                        pltpu.SemaphoreType.DMA, pltpu.SemaphoreType.DMA],
        compiler_params=pltpu.CompilerParams(dimension_semantics=("arbitrary",),
                                             vmem_limit_bytes=VMEM_LIMIT_BYTES),
        name="token_mix",
    )(x2d, cos, sin, row(g_mix), w_in.astype(BF16), dmat, xi_b, zeta_b, row(g_ret),
      w_pool.astype(BF16), row(pool_scale), w_branch.astype(BF16), w_out.astype(BF16),
      row(g_ffn), _split_router(w_router), b_router.reshape(-1, 1), tri)


def _fixup_body(flag_ref, zstart_ref, dest_ref, h2_hbm, xs_in, xs_ref, hbuf_ref, zbuf_ref,
                sem, zsem, hsem):
    del xs_in
    td = hbuf_ref.shape[0]
    te = zbuf_ref.shape[0]
    i = pl.program_id(0)
    overflow = flag_ref[0] != 0

    @pl.when((i == 0) & overflow)
    def _():
        zbuf_ref[...] = jnp.zeros(zbuf_ref.shape, F32)

    def zero_copy(e):
        start = pl.multiple_of(jnp.maximum(zstart_ref[e], 0), te)
        return pltpu.make_async_copy(zbuf_ref, xs_ref.at[pl.ds(start, te)], zsem)

    @pl.when((i == 0) & overflow)
    def _():
        def start(e, c):
            @pl.when(zstart_ref[e] >= 0)
            def _():
                zero_copy(e).start()
            return c

        def wait(e, c):
            @pl.when(zstart_ref[e] >= 0)
            def _():
                zero_copy(e).wait()
            return c

        lax.fori_loop(0, zstart_ref.shape[0], start, 0)
        lax.fori_loop(0, zstart_ref.shape[0], wait, 0)

    @pl.when(overflow)
    def _():
        tile = pltpu.make_async_copy(h2_hbm.at[pl.ds(i * td, td)], hbuf_ref, hsem)
        tile.start()
        tile.wait()

        def start(j, c):
            ts = [j * ROW_UNROLL + u for u in range(ROW_UNROLL)]
            dests = [[dest_ref[kk, t] for kk in range(TOP_K)] for t in ts]
            for t, dest in zip(ts, dests):
                for kk in range(TOP_K):
                    pltpu.make_async_copy(hbuf_ref.at[pl.ds(t, 1)], xs_ref.at[pl.ds(dest[kk], 1)],
                                          sem).start(priority=kk % 2)
            return c

        lax.fori_loop(0, td // ROW_UNROLL, start, 0)
        for kk in range(TOP_K):
            pltpu.make_async_copy(hbuf_ref, xs_ref.at[pl.ds(0, td)], sem).wait()


def _moe_dispatch_fixup(overflow, zstart, dest_t, h2, xs):
    T, D = h2.shape
    td = min(DISP_TILE, T)
    smem_spec = pl.BlockSpec((TOP_K, td), lambda i, *_: (0, i), memory_space=pltpu.SMEM)
    return pl.pallas_call(
        _fixup_body,
        grid_spec=pltpu.PrefetchScalarGridSpec(
            num_scalar_prefetch=2,
            grid=(T // td,),
            in_specs=[smem_spec, pl.BlockSpec(memory_space=pl.ANY), pl.BlockSpec(memory_space=pl.ANY)],
            out_specs=pl.BlockSpec(memory_space=pl.ANY),
            scratch_shapes=[pltpu.VMEM((td, D), F32), pltpu.VMEM((EXPERT_TILE, D), F32),
                            pltpu.SemaphoreType.DMA, pltpu.SemaphoreType.DMA, pltpu.SemaphoreType.DMA]),
        out_shape=jax.ShapeDtypeStruct(xs.shape, xs.dtype),
        input_output_aliases={4: 0},
        compiler_params=pltpu.CompilerParams(dimension_semantics=("arbitrary",),
                                             vmem_limit_bytes=VMEM_LIMIT_BYTES),
        name="moe_dispatch_fixup",
    )(overflow, zstart, dest_t, h2, xs)


def _expert_body(blk_e_ref, blk_src_ref, nused_ref, first_ref, next_e_ref, xs_ref, wgu_hbm, bgu_ref,
                 wd_hbm, bd_ref, yb_ref, wgu_f32_ref, wd_f32_ref, wgu_bf_ref, wd_bf_ref, wsem):
    i = pl.program_id(0)
    d_ff = wd_bf_ref.shape[0]

    def weight_copies(e):
        return (pltpu.make_async_copy(wgu_hbm.at[pl.ds(e, 1)], wgu_f32_ref, wsem.at[0]),
                pltpu.make_async_copy(wd_hbm.at[pl.ds(e, 1)], wd_f32_ref, wsem.at[1]))

    @pl.when(i == 0)
    def _():
        for cp in weight_copies(blk_e_ref[0]):
            cp.start()

    @pl.when(first_ref[i] != 0)
    def _():
        for cp in weight_copies(blk_e_ref[i]):
            cp.wait()
        wgu_bf_ref[...] = wgu_f32_ref[0].astype(BF16)
        wd_bf_ref[...] = wd_f32_ref[0].astype(BF16)

    @pl.when((first_ref[i] != 0) & (next_e_ref[i] >= 0))
    def _():
        for cp in weight_copies(next_e_ref[i]):
            cp.start(priority=1)

    @pl.when(i < nused_ref[0])
    def _():
        gu = _dot(xs_ref[...].astype(BF16), wgu_bf_ref[...]) + bgu_ref[...]
        gate = jnp.minimum(gu[:, 0:d_ff], SWIGLU_LIMIT)
        up = jnp.clip(gu[:, d_ff:2 * d_ff], -SWIGLU_LIMIT, SWIGLU_LIMIT)
        act = (up + 1.0) * (gate * jax.nn.sigmoid(gate * SWIGLU_ALPHA))
        yb_ref[...] = _dot(act.astype(BF16), wd_bf_ref[...]) + bd_ref[...]

    @pl.when(i >= nused_ref[0])
    def _():
        yb_ref[...] = jnp.zeros(yb_ref.shape, F32)


def _moe_experts(xs, blk_e, blk_src, nused, first, next_e, w_gate_up, b_gate_up, w_down, b_down):
    D = xs.shape[1]
    E, _, two_ff = w_gate_up.shape
    d_ff = two_ff // 2
    te = EXPERT_TILE
    n_rows = blk_e.shape[0] * te
    return pl.pallas_call(
        _expert_body,
        grid_spec=pltpu.PrefetchScalarGridSpec(
            num_scalar_prefetch=5,
            grid=(n_rows // te,),
            in_specs=[pl.BlockSpec((te, D), lambda i, be, bs, *_: (bs[i], 0)),
                      pl.BlockSpec(memory_space=pl.ANY),
                      pl.BlockSpec((None, 1, two_ff), lambda i, be, *_: (be[i], 0, 0)),
                      pl.BlockSpec(memory_space=pl.ANY),
                      pl.BlockSpec((None, 1, D), lambda i, be, *_: (be[i], 0, 0))],
            out_specs=pl.BlockSpec((te, D), lambda i, *_: (i, 0)),
            scratch_shapes=[pltpu.VMEM((1, D, two_ff), F32), pltpu.VMEM((1, d_ff, D), F32),
                            pltpu.VMEM((D, two_ff), BF16), pltpu.VMEM((d_ff, D), BF16),
                            pltpu.SemaphoreType.DMA((2,))]),
        out_shape=jax.ShapeDtypeStruct((n_rows, D), F32),
        compiler_params=pltpu.CompilerParams(dimension_semantics=("arbitrary",),
                                             vmem_limit_bytes=VMEM_LIMIT_BYTES),
        name="moe_experts",
    )(blk_e, blk_src, nused, first, next_e, xs, w_gate_up, b_gate_up.reshape(E, 1, two_ff),
      w_down, b_down.reshape(E, 1, D))


def _comb_body(ws_ref, ok_ref, dest_ref, destn_ref, yb_ref, pos_ref, w_ref, x1_ref, p_ref,
               gple_ref, wpg_ref, wpp_ref, gfin_ref, out_ref, gbuf_ref, sems):
    tc = x1_ref.shape[0]
    n_rows = gbuf_ref.shape[1]
    i = pl.program_id(0)
    slot = i % 2

    def window_copy(tile, e, s):
        src = pl.multiple_of(ws_ref[tile * N_EXPERTS + e], SUBLANES)
        return pltpu.make_async_copy(yb_ref.at[pl.ds(src, COMB_WINDOW)],
                                     gbuf_ref.at[s, pl.ds(e * COMB_WINDOW, COMB_WINDOW)], sems.at[s])

    def issue(tile, dest_r, s):
        @pl.when(ok_ref[tile] != 0)
        def _():
            for e in range(N_EXPERTS):
                window_copy(tile, e, s).start(priority=min(e % 4, 1))

        @pl.when(ok_ref[tile] == 0)
        def _():
            def start(j, c):
                ts = [j * ROW_UNROLL + u for u in range(ROW_UNROLL)]
                srcs = [[dest_r[kk, t] for kk in range(TOP_K)] for t in ts]
                for t, src in zip(ts, srcs):
                    for kk in range(TOP_K):
                        pltpu.make_async_copy(yb_ref.at[pl.ds(src[kk], 1)],
                                              gbuf_ref.at[s, pl.ds(kk * tc + t, 1)],
                                              sems.at[s]).start(priority=kk % 2)
                return c

            lax.fori_loop(0, tc // ROW_UNROLL, start, 0)

    @pl.when(i == 0)
    def _():
        issue(0, dest_ref, 0)

    @pl.when(i + 1 < pl.num_programs(0))
    def _():
        issue(i + 1, destn_ref, 1 - slot)

    def to_cols(rows):
        pad = jnp.zeros((LANES - rows.shape[0], tc), F32)
        return jnp.concatenate([rows, pad], axis=0).T

    w_col = to_cols(w_ref[...])
    fast = ok_ref[i] != 0
    slow_pos = (lax.broadcasted_iota(I32, (TOP_K, tc), 0) * tc
                + lax.broadcasted_iota(I32, (TOP_K, tc), 1))
    pos_col = to_cols(jnp.where(fast, pos_ref[...], slow_pos).astype(F32))
    pp = _dot(p_ref[...].astype(BF16), wpp_ref[...])

    @pl.when(fast)
    def _():
        pltpu.make_async_copy(yb_ref.at[pl.ds(0, n_rows)], gbuf_ref.at[slot], sems.at[slot]).wait()

    @pl.when(jnp.logical_not(fast))
    def _():
        pltpu.make_async_copy(yb_ref.at[pl.ds(0, TOP_K * tc)], gbuf_ref.at[slot, pl.ds(0, TOP_K * tc)],
                              sems.at[slot]).wait()
        gbuf_ref[slot, pl.ds(TOP_K * tc, n_rows - TOP_K * tc), :] = jnp.zeros(
            (n_rows - TOP_K * tc, gbuf_ref.shape[2]), F32)

    ciota = lax.broadcasted_iota(I32, (tc, n_rows), 1).astype(F32)
    sel = jnp.zeros((tc, n_rows), F32)
    for kk in range(TOP_K):
        sel = jnp.where(ciota == pos_col[:, kk:kk + 1], w_col[:, kk:kk + 1], sel)
    moe = _dot(sel.astype(BF16), gbuf_ref[slot].astype(BF16))
    x2 = x1_ref[...] + moe
    h3 = _rms(x2, gple_ref[...]).astype(BF16)
    gate = jax.nn.sigmoid(_dot(h3, wpg_ref[...]))
    x3 = x2 + gate * pp
    out_ref[...] = _rms(x3, gfin_ref[...])


def _moe_combine(ws, ok, dest_t, pos_t, yb, w_t, x1, p_all, layer, g_ple, w_ple_gate, w_ple_proj, g_final):
    T, D = x1.shape
    tc = min(COMB_TILE, T)
    n_tiles = T // tc
    ple = p_all.shape[1]
    smem_spec = pl.BlockSpec((TOP_K, tc), lambda i, *_: (0, i), memory_space=pltpu.SMEM)
    smem_next = pl.BlockSpec((TOP_K, tc), lambda i, *_: (0, jnp.minimum(i + 1, n_tiles - 1)),
                             memory_space=pltpu.SMEM)
    top_spec = pl.BlockSpec((TOP_K, tc), lambda i, *_: (0, i))
    const = lambda shape: pl.BlockSpec(shape, lambda i, *_: (0,) * len(shape),
                                       pipeline_mode=pl.Buffered(1))
    row = lambda a: a.reshape(1, -1)
    return pl.pallas_call(
        _comb_body,
        grid_spec=pltpu.PrefetchScalarGridSpec(
            num_scalar_prefetch=2,
            grid=(n_tiles,),
            in_specs=[smem_spec, smem_next,
                      pl.BlockSpec(memory_space=pl.ANY),
                      top_spec, top_spec,
                      pl.BlockSpec((tc, D), lambda i, *_: (i, 0)),
                      pl.BlockSpec((tc, ple), lambda i, *_: (layer * n_tiles + i, 0)),
                      const((1, D)), const((D, D)), const((ple, D)), const((1, D))],
            out_specs=pl.BlockSpec((tc, D), lambda i, *_: (i, 0)),
            scratch_shapes=[pltpu.VMEM((2, N_EXPERTS * COMB_WINDOW, D), F32),
                            pltpu.SemaphoreType.DMA((2,))]),
        out_shape=jax.ShapeDtypeStruct((T, D), F32),
        compiler_params=pltpu.CompilerParams(dimension_semantics=("arbitrary",),
                                             vmem_limit_bytes=VMEM_LIMIT_BYTES),
        name="moe_combine",
    )(ws, ok, dest_t, dest_t, yb, pos_t, w_t, x1, p_all, row(g_ple), w_ple_gate.astype(BF16),
      w_ple_proj.astype(BF16), row(g_final))


def _combine_windows(pstart, trun, counts, idx_t, dest_t, tile, n_rows):
    n_tiles = trun.shape[0]
    tcnt = jnp.concatenate([trun[1:], counts[None, :]], axis=0) - trun
    start = pstart[None, :] + trun
    ws = jnp.minimum(start // SUBLANES * SUBLANES, n_rows - COMB_WINDOW).astype(I32)
    ok = jnp.all(start + tcnt - ws <= COMB_WINDOW, axis=1).astype(I32)
    ws_tok = _select_expert(jnp.repeat(ws, tile, axis=0)[None], idx_t)
    pos_t = idx_t * COMB_WINDOW + dest_t - ws_tok
    return ws.reshape(-1), ok, pos_t.astype(I32)


def _select_expert(table, idx_t):
    onehot = idx_t[:, :, None] == jnp.arange(N_EXPERTS, dtype=I32)
    return jnp.sum(jnp.where(onehot, table, 0), axis=-1)


def _group_layout(counts, n_blocks):
    te = EXPERT_TILE
    padded = (counts + te - 1) // te * te
    pends = jnp.cumsum(padded)
    pstart = (pends - padded).astype(I32)
    nused = (pends[-1] // te).astype(I32)
    tail = nused + jnp.arange(N_EXPERTS, dtype=I32)
    zstart = jnp.concatenate([jnp.where(padded > 0, pends - te, -1),
                              jnp.where(tail < n_blocks, tail * te, -1)]).astype(I32)
    blk_src = jnp.minimum(jnp.arange(n_blocks, dtype=I32), jnp.maximum(nused - 1, 0))
    blk_e = jnp.sum((blk_src * te)[:, None] >= pends[None, :], axis=1)
    blk_e = jnp.clip(blk_e, 0, N_EXPERTS - 1).astype(I32)
    blk = jnp.arange(n_blocks, dtype=I32)
    prev_e = jnp.concatenate([blk_e[:1], blk_e[:-1]])
    first = ((blk < nused) & ((blk == 0) | (blk_e != prev_e))).astype(I32)
    experts = jnp.arange(N_EXPERTS, dtype=I32)[None, :]
    later = (experts > blk_e[:, None]) & (counts[None, :] > 0)
    next_e = jnp.min(jnp.where(later, experts, N_EXPERTS), axis=1)
    next_e = jnp.where(next_e < N_EXPERTS, next_e, -1).astype(I32)
    overflow = jnp.any(counts > EXPERT_CAP)
    pstart_blk = jnp.sum(jnp.where(blk_e[:, None] == experts, pstart[None, :], 0), axis=1)
    blk_cap = blk_e * (EXPERT_CAP // te) + (blk_src - pstart_blk // te)
    blk_src = jnp.where(overflow, blk_src, blk_cap).astype(I32)
    return (pstart, zstart, blk_e, blk_src, nused.reshape(1), first, next_e,
            overflow.astype(I32).reshape(1))


def kernel(x, p, positions, g_mix_norm, w_in, g_ret_norm, w_pool, pool_scale, w_branch, w_out,
           g_ffn_norm, w_router, b_router, w_gate_up, b_gate_up, w_down, b_down,
           g_ple_norm, w_ple_gate, w_ple_proj, g_final):
    B, S, D = x.shape
    depth = w_in.shape[0]
    T = B * S
    xt = x.reshape(T, D)
    cos, sin = _rope_tables(positions.reshape(T, 1))
    n_blocks = (T * TOP_K) // EXPERT_TILE + N_EXPERTS
    for i in range(depth):
        x1, h2, idx_t, w_t, rank_t, cnt, trun, xs = _token_mix(
            xt, cos, sin, g_mix_norm[i], w_in[i], g_ret_norm[i], w_pool[i], pool_scale[i],
            w_branch[i], w_out[i], g_ffn_norm[i], w_router[i], b_router[i], S)
        (pstart, zstart, blk_e, blk_src, nused, first, next_e,
         overflow) = _group_layout(cnt[:, 0], n_blocks)
        dest_t = _select_expert(pstart[None, None, :], idx_t) + rank_t
        xs = _moe_dispatch_fixup(overflow, zstart, dest_t, h2, xs)
        yb = _moe_experts(xs, blk_e, blk_src, nused, first, next_e,
                          w_gate_up[i], b_gate_up[i], w_down[i], b_down[i])
        assert depth == 1
        assert min(MIX_TILE, S) == min(COMB_TILE, T)
        ws, ok, pos_t = _combine_windows(pstart, trun[:, :, 0], cnt[:, 0], idx_t, dest_t,
                                         min(COMB_TILE, T), n_blocks * EXPERT_TILE)
        xt = _moe_combine(ws, ok, dest_t, pos_t, yb, w_t, x1, p.reshape(depth * T, -1), i, g_ple_norm[i],
                          w_ple_gate[i], w_ple_proj[i], g_final)
    return xt.reshape(B, S, D)
```

```python
import functools

import numpy as np
import jax
import jax.numpy as jnp
from jax import lax
from jax.experimental import pallas as pl
from jax.experimental.pallas import tpu as pltpu

F32 = jnp.float32
BF16 = jnp.bfloat16
I32 = jnp.int32

RET_HEADS = 8
RET_HEAD_DIM = 128
ROPE_BASE = 10000.0
GN_EPS = 1e-5
RMS_EPS = 1e-6
POOL_WINDOWS = (2, 4, 8, 16)
N_EXPERTS = 32
TOP_K = 4
SWIGLU_ALPHA = 1.702
SWIGLU_LIMIT = 7.0

LANES = 128
SUBLANES = 8
VMEM_LIMIT_BYTES = 56 * 1024 * 1024

MIX_TILE = 256
ROPE_TILE = 1024
DISP_TILE = 2048
EXPERT_CAP = 2560
DISPATCH_LAG_SLOTS = 3
EXPERT_TILE = 512
EXPERT_CHAINS = 2
COMB_TILE = 256
POOL_HALO = 16
POOL_PAD = 8
COMB_WINDOW = 64
ROW_UNROLL = 8


def _const_spec(shape):
    nd = len(shape)
    return pl.BlockSpec(shape, lambda *_: (0,) * nd, pipeline_mode=pl.Buffered(1))


def _zero_unused_capacity(e, cnt, zbuf_ref, xs_ref, zsem):
    te = zbuf_ref.shape[0]
    cnt = jnp.minimum(cnt, EXPERT_CAP)
    first = e * EXPERT_CAP + cnt
    n = (-cnt) % te
    head = jnp.minimum(n, (-first) % SUBLANES)
    out = []
    for r in range(SUBLANES - 1):
        out.append((r < head, pltpu.make_async_copy(zbuf_ref.at[pl.ds(0, 1)],
                                                    xs_ref.at[pl.ds(first + r, 1)], zsem)))
    rest = n - head
    pos = first + head
    size = te // 2
    while size >= SUBLANES:
        at = pl.multiple_of(pos + (rest & ~(2 * size - 1)), SUBLANES)
        out.append(((rest & size) != 0,
                    pltpu.make_async_copy(zbuf_ref.at[pl.ds(0, size)], xs_ref.at[pl.ds(at, size)], zsem)))
        size //= 2
    used_end = first + n
    for b in range(EXPERT_CAP // te):
        at = pl.multiple_of(used_end + b * te, te)
        out.append((at < (e + 1) * EXPERT_CAP,
                    pltpu.make_async_copy(zbuf_ref, xs_ref.at[pl.ds(at, te)], zsem)))
    return out


def _rms(x, g):
    return x * lax.rsqrt(jnp.mean(x * x, axis=-1, keepdims=True) + RMS_EPS) * g


def _dot(a, b):
    return jnp.dot(a, b, preferred_element_type=F32)


def _dot_nt(a, b):
    return lax.dot_general(a, b, (((1,), (1,)), ((), ())), preferred_element_type=F32)


def _dot_tn(a, b):
    return lax.dot_general(a, b, (((0,), (0,)), ((), ())), preferred_element_type=F32)


def _rope_body(pos_ref, inv_ref, sign_ref, cos_ref, sin_ref):
    half_rows = pos_ref.shape[0] // 2
    half = RET_HEAD_DIM // 2
    lane = lax.broadcasted_iota(I32, (half_rows, RET_HEAD_DIM), 1)
    first = lane < half
    pos = jnp.where(first, pos_ref[pl.ds(0, half_rows, stride=2), :],
                    pos_ref[pl.ds(1, half_rows, stride=2), :]).astype(F32)
    ang = pos * inv_ref[...]
    cos = jnp.cos(ang)
    sin = jnp.sin(ang)
    cos_sw = pltpu.roll(cos, half, 1)
    sin_sw = pltpu.roll(sin, half, 1)
    sign = sign_ref[...]
    cos_ref[pl.ds(0, half_rows, stride=2), :] = jnp.where(first, cos, cos_sw)
    cos_ref[pl.ds(1, half_rows, stride=2), :] = jnp.where(first, cos_sw, cos)
    sin_ref[pl.ds(0, half_rows, stride=2), :] = jnp.where(first, sin, sin_sw) * sign
    sin_ref[pl.ds(1, half_rows, stride=2), :] = jnp.where(first, sin_sw, sin) * sign


def _rope_tables(pos_col):
    T = pos_col.shape[0]
    half = RET_HEAD_DIM // 2
    inv = ROPE_BASE ** (-jnp.arange(half, dtype=F32) / half)
    inv_full = jnp.concatenate([inv, inv]).reshape(1, RET_HEAD_DIM)
    sign = jnp.concatenate([-jnp.ones((half,), F32), jnp.ones((half,), F32)]).reshape(1, RET_HEAD_DIM)
    tile = min(ROPE_TILE, T)
    return pl.pallas_call(
        _rope_body,
        grid=(T // tile,),
        in_specs=[pl.BlockSpec((tile, 1), lambda i: (i, 0)),
                  pl.BlockSpec((1, RET_HEAD_DIM), lambda i: (0, 0)),
                  pl.BlockSpec((1, RET_HEAD_DIM), lambda i: (0, 0))],
        out_specs=[pl.BlockSpec((tile, RET_HEAD_DIM), lambda i: (i, 0)),
                   pl.BlockSpec((tile, RET_HEAD_DIM), lambda i: (i, 0))],
        out_shape=[jax.ShapeDtypeStruct((T, RET_HEAD_DIM), F32),
                   jax.ShapeDtypeStruct((T, RET_HEAD_DIM), F32)],
        name="rope_tables",
    )(pos_col, inv_full, sign)


def _retention_constants(tile):
    h = np.arange(RET_HEADS, dtype=np.float64)
    log_gamma = np.log1p(-np.exp2(-5.0 - h))
    idx = np.arange(tile, dtype=np.float64)
    diff = idx[:, None] - idx[None, :]
    dmat = np.where(diff >= 0, np.exp(log_gamma[:, None, None] * np.maximum(diff, 0.0)[None]), 0.0)
    xi = np.exp(log_gamma[:, None] * (idx + 1.0)[None])
    zeta = np.exp(log_gamma[:, None] * (tile - 1.0 - idx)[None])
    chunk_decay = np.exp(log_gamma * tile)
    xi_b = np.broadcast_to(xi[:, :, None], (RET_HEADS, tile, RET_HEAD_DIM))
    zeta_b = np.broadcast_to(zeta[:, :, None], (RET_HEADS, tile, RET_HEAD_DIM))
    return (jnp.asarray(dmat, F32), jnp.asarray(xi_b, F32), jnp.asarray(zeta_b, F32),
            tuple(float(c) for c in chunk_decay))


def _mix_body(x_ref, cos_ref, sin_ref, gmix_ref, win_ref, dmat_ref, xi_ref, zeta_ref, gret_ref,
              wpool_ref, pscale_ref, wbr_ref, wout_ref, gffn_ref, wr_ref, br_ref, tri_ref,
              x1_ref, h2_ref, idx_ref, w_ref, rank_ref, cnt_ref, trun_ref, xs_hbm,
              state_ref, ue_ref, lv_ref, run_ref, hbuf_ref, dvm_ref, dsm_ref, cvm_ref, csm_ref,
              zbuf_ref, rsem, dsem, csem, zsem, *, tiles_per_seq, chunk_decay, trash):
    tm, d_model = x_ref.shape
    ret_width = RET_HEADS * RET_HEAD_DIM
    i = pl.program_id(0)
    seq_tile = i % tiles_per_seq
    slot = i % DISPATCH_LAG_SLOTS
    prev = (i + DISPATCH_LAG_SLOTS - 1) % DISPATCH_LAG_SLOTS
    old = (i + DISPATCH_LAG_SLOTS - 2) % DISPATCH_LAG_SLOTS

    def row_copy(src_slot, t, dest):
        return pltpu.make_async_copy(hbuf_ref.at[src_slot, pl.ds(t, 1)], xs_hbm.at[pl.ds(dest, 1)], rsem)

    def wait_tile_rows():
        for _ in range(TOP_K):
            pltpu.make_async_copy(hbuf_ref.at[0], xs_hbm.at[pl.ds(0, tm)], rsem).wait()

    def dest_copy(s_):
        return pltpu.make_async_copy(dvm_ref.at[s_], dsm_ref.at[s_], dsem.at[s_])

    @pl.when(seq_tile == 0)
    def _():
        state_ref[...] = jnp.zeros(state_ref.shape, F32)
        ue_ref[0:POOL_PAD + POOL_HALO, :] = jnp.zeros((POOL_PAD + POOL_HALO, ue_ref.shape[1]), F32)

    @pl.when(i == 0)
    def _():
        run_ref[...] = jnp.zeros(run_ref.shape, F32)
        lv_ref[:, 0:POOL_PAD, :] = jnp.zeros((lv_ref.shape[0], POOL_PAD, lv_ref.shape[2]), F32)
        for s_ in range(1, DISPATCH_LAG_SLOTS):
            hbuf_ref[s_] = jnp.zeros(hbuf_ref.shape[1:], F32)
            for kk in range(TOP_K):
                dvm_ref[s_, kk:kk + 1, :] = trash + kk * tm + lax.broadcasted_iota(I32, (1, tm), 1)
            dest_copy(s_).start()

    dest_copy(old).wait()

    x = x_ref[...]
    hb = _rms(x, gmix_ref[...]).astype(BF16)

    qkvg = _dot(hb, win_ref[:, 0:4 * ret_width])
    cos = cos_ref[...]
    sin = sin_ref[...]

    def rot(a):
        return a * cos + pltpu.roll(a, RET_HEAD_DIM // 2, 1) * sin

    ys = []
    for h in range(RET_HEADS):
        lo = h * RET_HEAD_DIM
        q = rot(qkvg[:, lo:lo + RET_HEAD_DIM]).astype(BF16)
        k = (rot(qkvg[:, ret_width + lo:ret_width + lo + RET_HEAD_DIM])
             * (RET_HEAD_DIM ** -0.5)).astype(BF16)
        v = qkvg[:, 2 * ret_width + lo:2 * ret_width + lo + RET_HEAD_DIM]
        g = qkvg[:, 3 * ret_width + lo:3 * ret_width + lo + RET_HEAD_DIM]
        scores = _dot_nt(q, k) * dmat_ref[h]
        inner = _dot(scores.astype(BF16), v.astype(BF16))
        st = state_ref[h]
        cross = _dot(q, st.astype(BF16)) * xi_ref[h]
        state_ref[h] = st * chunk_decay[h] + _dot_tn(k, (v * zeta_ref[h]).astype(BF16))
        ret = inner + cross
        mu = jnp.mean(ret, axis=-1, keepdims=True)
        dev = ret - mu
        var = jnp.mean(dev * dev, axis=-1, keepdims=True)
        rn = dev * lax.rsqrt(var + GN_EPS) * gret_ref[:, lo:lo + RET_HEAD_DIM]
        ys.append(((g * jax.nn.sigmoid(g)) * rn).astype(BF16))
        per_head = tm // RET_HEADS
        for t in range(h * per_head, (h + 1) * per_head):
            for kk in range(TOP_K):
                row_copy(old, t, dsm_ref[old, kk, t]).start(priority=kk % 2)
    y_ret = jnp.concatenate(ys, axis=1)
    branch_a = _dot(y_ret, wbr_ref[0])

    u = _dot(hb, win_ref[:, 4 * ret_width:4 * ret_width + d_model])
    top = POOL_PAD + POOL_HALO
    ext = tm + POOL_HALO
    group = d_model // len(POOL_WINDOWS)
    ue_ref[top:top + tm, :] = u
    s2 = ue_ref[POOL_PAD:POOL_PAD + ext, :] + ue_ref[POOL_PAD - 1:POOL_PAD - 1 + ext, :]
    lv_ref[0, POOL_PAD:POOL_PAD + ext, :] = s2
    s4 = s2[:, group:] + lv_ref[0, POOL_PAD - 2:POOL_PAD - 2 + ext, group:]
    lv_ref[1, POOL_PAD:POOL_PAD + ext, group:] = s4
    s8 = s4[:, group:] + lv_ref[1, POOL_PAD - 4:POOL_PAD - 4 + ext, 2 * group:]
    lv_ref[2, POOL_PAD:POOL_PAD + ext, 2 * group:] = s8
    sums = [lv_ref[0, top:top + tm, 0:group],
            lv_ref[1, top:top + tm, group:2 * group],
            lv_ref[2, top:top + tm, 2 * group:3 * group],
            lv_ref[2, top:top + tm, 3 * group:] + lv_ref[2, top - 8:top - 8 + tm, 3 * group:]]
    ue_ref[POOL_PAD:top, :] = ue_ref[tm + POOL_PAD:tm + top, :]
    pos = seq_tile * tm + lax.broadcasted_iota(I32, (tm, 1), 0)
    outs = []
    for gi, w in enumerate(POOL_WINDOWS):
        ug = u[:, gi * group:(gi + 1) * group]
        inv_count = 1.0 / jnp.minimum(pos + 1, w).astype(F32)
        mixed = (sums[gi] * inv_count - ug).astype(BF16)
        outs.append(_dot(mixed, wpool_ref[gi]))
    y_pool = (jnp.concatenate(outs, axis=1) * pscale_ref[...]).astype(BF16)
    branch_b = _dot(y_pool, wbr_ref[1])

    gates = _dot(hb, win_ref[:, 4 * ret_width + d_model:4 * ret_width + 3 * d_model])
    merged = (jax.nn.sigmoid(gates[:, 0:d_model]) * branch_a
              + jax.nn.sigmoid(gates[:, d_model:2 * d_model]) * branch_b)
    x1 = x + _dot(merged.astype(BF16), wout_ref[...])
    x1_ref[...] = x1

    h2 = _rms(x1, gffn_ref[...])
    h2_ref[...] = h2
    n_exp = br_ref.shape[0]
    h2_hi = h2.astype(BF16)
    h2_lo = (h2 - h2_hi.astype(F32)).astype(BF16)
    both = _dot(h2_hi, wr_ref[...])
    lg = both[:, 0:LANES] + both[:, LANES:2 * LANES] + _dot(h2_lo, wr_ref[:, 0:LANES])
    logits = lg.T[0:n_exp, :] + br_ref[...]
    eiota = lax.broadcasted_iota(I32, (n_exp, tm), 0)
    vals, idxs = [], []
    l = logits
    for _ in range(TOP_K):
        m = jnp.max(l, axis=0, keepdims=True)
        sel = jnp.min(jnp.where(l == m, eiota, n_exp), axis=0, keepdims=True)
        vals.append(m)
        idxs.append(sel)
        l = jnp.where(eiota == sel, -jnp.inf, l)
    exps = [jnp.exp(v - vals[0]) for v in vals]
    denom = exps[0] + exps[1] + exps[2] + exps[3]
    inv_denom = 1.0 / denom
    onehot = jnp.zeros((n_exp, tm), F32)
    for kk in range(TOP_K):
        onehot = onehot + (eiota == idxs[kk]).astype(F32)
    base = _dot(onehot.astype(BF16), tri_ref[...]) + run_ref[:, 0:1]
    tiota = lax.broadcasted_iota(I32, (1, tm), 1)
    for kk in range(TOP_K):
        idx_ref[kk:kk + 1, :] = idxs[kk]
        w_ref[kk:kk + 1, :] = exps[kk] * inv_denom
        rank = jnp.sum(jnp.where(eiota == idxs[kk], base, 0.0), axis=0, keepdims=True).astype(I32)
        rank_ref[kk:kk + 1, :] = rank
        dvm_ref[slot, kk:kk + 1, :] = jnp.where(rank < EXPERT_CAP, idxs[kk] * EXPERT_CAP + rank,
                                                trash + kk * tm + tiota)
    dest_copy(slot).start()
    trun_ref[...] = run_ref[...].astype(I32)
    run = run_ref[...] + jnp.sum(onehot, axis=1, keepdims=True)
    run_ref[...] = run
    cnt_ref[...] = run.astype(I32)

    hbuf_ref[slot] = h2
    wait_tile_rows()

    @pl.when(i == pl.num_programs(0) - 1)
    def _():
        cvm_ref[...] = run.astype(I32)
        counts_copy = pltpu.make_async_copy(cvm_ref, csm_ref, csem)
        counts_copy.start()
        zbuf_ref[...] = jnp.zeros(zbuf_ref.shape, F32)
        counts_copy.wait()

        def zero_start(e, c):
            for cond, cp in _zero_unused_capacity(e, csm_ref[e, 0], zbuf_ref, xs_hbm, zsem):
                @pl.when(cond)
                def _():
                    cp.start()
            return c

        def zero_wait(e, c):
            for cond, cp in _zero_unused_capacity(e, csm_ref[e, 0], zbuf_ref, xs_hbm, zsem):
                @pl.when(cond)
                def _():
                    cp.wait()
            return c

        lax.fori_loop(0, N_EXPERTS, zero_start, 0)

        for s_ in (prev, slot):
            dest_copy(s_).wait()

            def start(j, c):
                ts = [j * ROW_UNROLL + u for u in range(ROW_UNROLL)]
                dests = [[dsm_ref[s_, kk, t] for kk in range(TOP_K)] for t in ts]
                for t, dest in zip(ts, dests):
                    for kk in range(TOP_K):
                        row_copy(s_, t, dest[kk]).start(priority=kk % 2)
                return c

            lax.fori_loop(0, tm // ROW_UNROLL, start, 0)
            wait_tile_rows()

        lax.fori_loop(0, N_EXPERTS, zero_wait, 0)


def _split_router(w_router):
    d, e = w_router.shape
    hi = w_router.astype(BF16)
    lo = (w_router - hi.astype(F32)).astype(BF16)
    pad = jnp.zeros((d, LANES - e), BF16)
    return jnp.concatenate([hi, pad, lo, pad], axis=1)


def _token_mix(x2d, cos, sin, g_mix, w_in, g_ret, w_pool, pool_scale, w_branch, w_out,
               g_ffn, w_router, b_router, seq_len):
    T, D = x2d.shape
    tm = min(MIX_TILE, seq_len)
    in_width = w_in.shape[1]
    ret_width = RET_HEADS * RET_HEAD_DIM
    assert T // tm >= DISPATCH_LAG_SLOTS - 1
    assert POOL_WINDOWS == (2, 4, 8, 16)
    dmat, xi_b, zeta_b, chunk_decay = _retention_constants(tm)
    tri = jnp.asarray(np.triu(np.ones((tm, tm), np.float32), 1), BF16)
    row = lambda a: a.reshape(1, -1)
    tile_spec = lambda w: pl.BlockSpec((tm, w), lambda i: (i, 0))
    top_spec = pl.BlockSpec((TOP_K, tm), lambda i: (0, i))
    n_xs = max(N_EXPERTS * EXPERT_CAP, (T * TOP_K // EXPERT_TILE + N_EXPERTS) * EXPERT_TILE)
    body = functools.partial(_mix_body, tiles_per_seq=seq_len // tm, chunk_decay=chunk_decay, trash=n_xs)
    return pl.pallas_call(
        body,
        grid=(T // tm,),
        in_specs=[tile_spec(D), tile_spec(RET_HEAD_DIM), tile_spec(RET_HEAD_DIM),
                  _const_spec((1, D)), _const_spec((D, in_width)),
                  _const_spec(dmat.shape), _const_spec(xi_b.shape), _const_spec(zeta_b.shape),
                  _const_spec((1, ret_width)), _const_spec(w_pool.shape), _const_spec((1, D)),
                  _const_spec(w_branch.shape), _const_spec((D, D)), _const_spec((1, D)),
                  _const_spec((D, 2 * LANES)), _const_spec((N_EXPERTS, 1)), _const_spec((tm, tm))],
        out_specs=[tile_spec(D), tile_spec(D), top_spec, top_spec, top_spec,
                   pl.BlockSpec((N_EXPERTS, LANES), lambda i: (0, 0)),
                   pl.BlockSpec((None, N_EXPERTS, LANES), lambda i: (i, 0, 0)),
                   pl.BlockSpec(memory_space=pl.ANY)],
        out_shape=[jax.ShapeDtypeStruct((T, D), F32), jax.ShapeDtypeStruct((T, D), F32),
                   jax.ShapeDtypeStruct((TOP_K, T), I32), jax.ShapeDtypeStruct((TOP_K, T), F32),
                   jax.ShapeDtypeStruct((TOP_K, T), I32),
                   jax.ShapeDtypeStruct((N_EXPERTS, LANES), I32),
                   jax.ShapeDtypeStruct((T // tm, N_EXPERTS, LANES), I32),
                   jax.ShapeDtypeStruct((n_xs + TOP_K * tm, D), F32)],
        scratch_shapes=[pltpu.VMEM((RET_HEADS, RET_HEAD_DIM, RET_HEAD_DIM), F32),
                        pltpu.VMEM((tm + POOL_PAD + POOL_HALO, D), F32),
                        pltpu.VMEM((3, tm + POOL_PAD + POOL_HALO, D), F32),
                        pltpu.VMEM((N_EXPERTS, LANES), F32),
                        pltpu.VMEM((DISPATCH_LAG_SLOTS, tm, D), F32),
                        pltpu.VMEM((DISPATCH_LAG_SLOTS, TOP_K, tm), I32),
                        pltpu.SMEM((DISPATCH_LAG_SLOTS, TOP_K, tm), I32),
                        pltpu.VMEM((N_EXPERTS, LANES), I32), pltpu.SMEM((N_EXPERTS, LANES), I32),
                        pltpu.VMEM((EXPERT_TILE, D), F32),
                        pltpu.SemaphoreType.DMA, pltpu.SemaphoreType.DMA((DISPATCH_LAG_SLOTS,)),
                        pltpu.SemaphoreType.DMA, pltpu.SemaphoreType.DMA],
        compiler_params=pltpu.CompilerParams(dimension_semantics=("arbitrary",),
                                             vmem_limit_bytes=VMEM_LIMIT_BYTES),
        name="token_mix",
    )(x2d, cos, sin, row(g_mix), w_in.astype(BF16), dmat, xi_b, zeta_b, row(g_ret),
      w_pool.astype(BF16), row(pool_scale), w_branch.astype(BF16), w_out.astype(BF16),
      row(g_ffn), _split_router(w_router), b_router.reshape(-1, 1), tri)


def _fixup_body(flag_ref, zstart_ref, dest_ref, h2_hbm, xs_in, xs_ref, hbuf_ref, zbuf_ref,
                sem, zsem, hsem):
    del xs_in
    td = hbuf_ref.shape[0]
    te = zbuf_ref.shape[0]
    i = pl.program_id(0)
    overflow = flag_ref[0] != 0

    @pl.when((i == 0) & overflow)
    def _():
        zbuf_ref[...] = jnp.zeros(zbuf_ref.shape, F32)

    def zero_copy(e):
        start = pl.multiple_of(jnp.maximum(zstart_ref[e], 0), te)
        return pltpu.make_async_copy(zbuf_ref, xs_ref.at[pl.ds(start, te)], zsem)

    @pl.when((i == 0) & overflow)
    def _():
        def start(e, c):
            @pl.when(zstart_ref[e] >= 0)
            def _():
                zero_copy(e).start()
            return c

        def wait(e, c):
            @pl.when(zstart_ref[e] >= 0)
            def _():
                zero_copy(e).wait()
            return c

        lax.fori_loop(0, zstart_ref.shape[0], start, 0)
        lax.fori_loop(0, zstart_ref.shape[0], wait, 0)

    @pl.when(overflow)
    def _():
        tile = pltpu.make_async_copy(h2_hbm.at[pl.ds(i * td, td)], hbuf_ref, hsem)
        tile.start()
        tile.wait()

        def start(j, c):
            ts = [j * ROW_UNROLL + u for u in range(ROW_UNROLL)]
            dests = [[dest_ref[kk, t] for kk in range(TOP_K)] for t in ts]
            for t, dest in zip(ts, dests):
                for kk in range(TOP_K):
                    pltpu.make_async_copy(hbuf_ref.at[pl.ds(t, 1)], xs_ref.at[pl.ds(dest[kk], 1)],
                                          sem).start(priority=kk % 2)
            return c

        lax.fori_loop(0, td // ROW_UNROLL, start, 0)
        for kk in range(TOP_K):
            pltpu.make_async_copy(hbuf_ref, xs_ref.at[pl.ds(0, td)], sem).wait()


def _moe_dispatch_fixup(overflow, zstart, dest_t, h2, xs):
    T, D = h2.shape
    td = min(DISP_TILE, T)
    smem_spec = pl.BlockSpec((TOP_K, td), lambda i, *_: (0, i), memory_space=pltpu.SMEM)
    return pl.pallas_call(
        _fixup_body,
        grid_spec=pltpu.PrefetchScalarGridSpec(
            num_scalar_prefetch=2,
            grid=(T // td,),
            in_specs=[smem_spec, pl.BlockSpec(memory_space=pl.ANY), pl.BlockSpec(memory_space=pl.ANY)],
            out_specs=pl.BlockSpec(memory_space=pl.ANY),
            scratch_shapes=[pltpu.VMEM((td, D), F32), pltpu.VMEM((EXPERT_TILE, D), F32),
                            pltpu.SemaphoreType.DMA, pltpu.SemaphoreType.DMA, pltpu.SemaphoreType.DMA]),
        out_shape=jax.ShapeDtypeStruct(xs.shape, xs.dtype),
        input_output_aliases={4: 0},
        compiler_params=pltpu.CompilerParams(dimension_semantics=("arbitrary",),
                                             vmem_limit_bytes=VMEM_LIMIT_BYTES),
        name="moe_dispatch_fixup",
    )(overflow, zstart, dest_t, h2, xs)


def _expert_body(blk_e_ref, blk_src_ref, nused_ref, first_ref, next_e_ref, xs_ref, wgu_hbm, bgu_ref,
                 wd_hbm, bd_ref, yb_ref, wgu_f32_ref, wd_f32_ref, wgu_bf_ref, wd_bf_ref, wsem):
    i = pl.program_id(0)
    d_ff = wd_bf_ref.shape[0]

    def weight_copies(e):
        return (pltpu.make_async_copy(wgu_hbm.at[pl.ds(e, 1)], wgu_f32_ref, wsem.at[0]),
                pltpu.make_async_copy(wd_hbm.at[pl.ds(e, 1)], wd_f32_ref, wsem.at[1]))

    @pl.when(i == 0)
    def _():
        for cp in weight_copies(blk_e_ref[0]):
            cp.start()

    @pl.when(first_ref[i] != 0)
    def _():
        for cp in weight_copies(blk_e_ref[i]):
            cp.wait()
        wgu_bf_ref[...] = wgu_f32_ref[0].astype(BF16)
        wd_bf_ref[...] = wd_f32_ref[0].astype(BF16)

    @pl.when((first_ref[i] != 0) & (next_e_ref[i] >= 0))
    def _():
        for cp in weight_copies(next_e_ref[i]):
            cp.start(priority=1)

    @pl.when(i < nused_ref[0])
    def _():
        rows = xs_ref.shape[0] // EXPERT_CHAINS
        for r0 in range(0, xs_ref.shape[0], rows):
            gu = _dot(xs_ref[r0:r0 + rows, :].astype(BF16), wgu_bf_ref[...]) + bgu_ref[...]
            gate = jnp.minimum(gu[:, 0:d_ff], SWIGLU_LIMIT)
            up = jnp.clip(gu[:, d_ff:2 * d_ff], -SWIGLU_LIMIT, SWIGLU_LIMIT)
            act = (up + 1.0) * (gate * jax.nn.sigmoid(gate * SWIGLU_ALPHA))
            yb_ref[r0:r0 + rows, :] = _dot(act.astype(BF16), wd_bf_ref[...]) + bd_ref[...]

    @pl.when(i >= nused_ref[0])
    def _():
        yb_ref[...] = jnp.zeros(yb_ref.shape, F32)


def _moe_experts(xs, blk_e, blk_src, nused, first, next_e, w_gate_up, b_gate_up, w_down, b_down):
    D = xs.shape[1]
    E, _, two_ff = w_gate_up.shape
    d_ff = two_ff // 2
    te = EXPERT_TILE
    n_rows = blk_e.shape[0] * te
    return pl.pallas_call(
        _expert_body,
        grid_spec=pltpu.PrefetchScalarGridSpec(
            num_scalar_prefetch=5,
            grid=(n_rows // te,),
            in_specs=[pl.BlockSpec((te, D), lambda i, be, bs, *_: (bs[i], 0)),
                      pl.BlockSpec(memory_space=pl.ANY),
                      pl.BlockSpec((None, 1, two_ff), lambda i, be, *_: (be[i], 0, 0)),
                      pl.BlockSpec(memory_space=pl.ANY),
                      pl.BlockSpec((None, 1, D), lambda i, be, *_: (be[i], 0, 0))],
            out_specs=pl.BlockSpec((te, D), lambda i, *_: (i, 0)),
            scratch_shapes=[pltpu.VMEM((1, D, two_ff), F32), pltpu.VMEM((1, d_ff, D), F32),
                            pltpu.VMEM((D, two_ff), BF16), pltpu.VMEM((d_ff, D), BF16),
                            pltpu.SemaphoreType.DMA((2,))]),
        out_shape=jax.ShapeDtypeStruct((n_rows, D), F32),
        compiler_params=pltpu.CompilerParams(dimension_semantics=("arbitrary",),
                                             vmem_limit_bytes=VMEM_LIMIT_BYTES),
        name="moe_experts",
    )(blk_e, blk_src, nused, first, next_e, xs, w_gate_up, b_gate_up.reshape(E, 1, two_ff),
      w_down, b_down.reshape(E, 1, D))


def _comb_body(ws_ref, ok_ref, dest_ref, destn_ref, yb_ref, pos_ref, w_ref, x1_ref, p_ref,
               gple_ref, wpg_ref, wpp_ref, gfin_ref, out_ref, gbuf_ref, sems):
    tc = x1_ref.shape[0]
    n_rows = gbuf_ref.shape[1]
    i = pl.program_id(0)
    slot = i % 2

    def window_copy(tile, e, s):
        src = pl.multiple_of(ws_ref[tile * N_EXPERTS + e], SUBLANES)
        return pltpu.make_async_copy(yb_ref.at[pl.ds(src, COMB_WINDOW)],
                                     gbuf_ref.at[s, pl.ds(e * COMB_WINDOW, COMB_WINDOW)], sems.at[s])

    def issue(tile, dest_r, s):
        @pl.when(ok_ref[tile] != 0)
        def _():
            for e in range(N_EXPERTS):
                window_copy(tile, e, s).start(priority=min(e % 4, 1))

        @pl.when(ok_ref[tile] == 0)
        def _():
            def start(j, c):
                ts = [j * ROW_UNROLL + u for u in range(ROW_UNROLL)]
                srcs = [[dest_r[kk, t] for kk in range(TOP_K)] for t in ts]
                for t, src in zip(ts, srcs):
                    for kk in range(TOP_K):
                        pltpu.make_async_copy(yb_ref.at[pl.ds(src[kk], 1)],
                                              gbuf_ref.at[s, pl.ds(kk * tc + t, 1)],
                                              sems.at[s]).start(priority=kk % 2)
                return c

            lax.fori_loop(0, tc // ROW_UNROLL, start, 0)

    @pl.when(i == 0)
    def _():
        issue(0, dest_ref, 0)

    @pl.when(i + 1 < pl.num_programs(0))
    def _():
        issue(i + 1, destn_ref, 1 - slot)

    def to_cols(rows):
        pad = jnp.zeros((LANES - rows.shape[0], tc), F32)
        return jnp.concatenate([rows, pad], axis=0).T

    w_col = to_cols(w_ref[...])
    fast = ok_ref[i] != 0
    slow_pos = (lax.broadcasted_iota(I32, (TOP_K, tc), 0) * tc
                + lax.broadcasted_iota(I32, (TOP_K, tc), 1))
    pos_col = to_cols(jnp.where(fast, pos_ref[...], slow_pos).astype(F32))
    pp = _dot(p_ref[...].astype(BF16), wpp_ref[...])

    @pl.when(fast)
    def _():
        pltpu.make_async_copy(yb_ref.at[pl.ds(0, n_rows)], gbuf_ref.at[slot], sems.at[slot]).wait()

    @pl.when(jnp.logical_not(fast))
    def _():
        pltpu.make_async_copy(yb_ref.at[pl.ds(0, TOP_K * tc)], gbuf_ref.at[slot, pl.ds(0, TOP_K * tc)],
                              sems.at[slot]).wait()
        gbuf_ref[slot, pl.ds(TOP_K * tc, n_rows - TOP_K * tc), :] = jnp.zeros(
            (n_rows - TOP_K * tc, gbuf_ref.shape[2]), F32)

    ciota = lax.broadcasted_iota(I32, (tc, n_rows), 1).astype(F32)
    sel = jnp.zeros((tc, n_rows), F32)
    for kk in range(TOP_K):
        sel = jnp.where(ciota == pos_col[:, kk:kk + 1], w_col[:, kk:kk + 1], sel)
    moe = _dot(sel.astype(BF16), gbuf_ref[slot].astype(BF16))
    x2 = x1_ref[...] + moe
    h3 = _rms(x2, gple_ref[...]).astype(BF16)
    gate = jax.nn.sigmoid(_dot(h3, wpg_ref[...]))
    x3 = x2 + gate * pp
    out_ref[...] = _rms(x3, gfin_ref[...])


def _moe_combine(ws, ok, dest_t, pos_t, yb, w_t, x1, p_all, layer, g_ple, w_ple_gate, w_ple_proj, g_final):
    T, D = x1.shape
    tc = min(COMB_TILE, T)
    n_tiles = T // tc
    ple = p_all.shape[1]
    smem_spec = pl.BlockSpec((TOP_K, tc), lambda i, *_: (0, i), memory_space=pltpu.SMEM)
    smem_next = pl.BlockSpec((TOP_K, tc), lambda i, *_: (0, jnp.minimum(i + 1, n_tiles - 1)),
                             memory_space=pltpu.SMEM)
    top_spec = pl.BlockSpec((TOP_K, tc), lambda i, *_: (0, i))
    const = lambda shape: pl.BlockSpec(shape, lambda i, *_: (0,) * len(shape),
                                       pipeline_mode=pl.Buffered(1))
    row = lambda a: a.reshape(1, -1)
    return pl.pallas_call(
        _comb_body,
        grid_spec=pltpu.PrefetchScalarGridSpec(
            num_scalar_prefetch=2,
            grid=(n_tiles,),
            in_specs=[smem_spec, smem_next,
                      pl.BlockSpec(memory_space=pl.ANY),
                      top_spec, top_spec,
                      pl.BlockSpec((tc, D), lambda i, *_: (i, 0)),
                      pl.BlockSpec((tc, ple), lambda i, *_: (layer * n_tiles + i, 0)),
                      const((1, D)), const((D, D)), const((ple, D)), const((1, D))],
            out_specs=pl.BlockSpec((tc, D), lambda i, *_: (i, 0)),
            scratch_shapes=[pltpu.VMEM((2, N_EXPERTS * COMB_WINDOW, D), F32),
                            pltpu.SemaphoreType.DMA((2,))]),
        out_shape=jax.ShapeDtypeStruct((T, D), F32),
        compiler_params=pltpu.CompilerParams(dimension_semantics=("arbitrary",),
                                             vmem_limit_bytes=VMEM_LIMIT_BYTES),
        name="moe_combine",
    )(ws, ok, dest_t, dest_t, yb, pos_t, w_t, x1, p_all, row(g_ple), w_ple_gate.astype(BF16),
      w_ple_proj.astype(BF16), row(g_final))


def _combine_windows(pstart, trun, counts, idx_t, dest_t, tile, n_rows):
    n_tiles = trun.shape[0]
    tcnt = jnp.concatenate([trun[1:], counts[None, :]], axis=0) - trun
    start = pstart[None, :] + trun
    ws = jnp.minimum(start // SUBLANES * SUBLANES, n_rows - COMB_WINDOW).astype(I32)
    ok = jnp.all(start + tcnt - ws <= COMB_WINDOW, axis=1).astype(I32)
    ws_tok = _select_expert(jnp.repeat(ws, tile, axis=0)[None], idx_t)
    pos_t = idx_t * COMB_WINDOW + dest_t - ws_tok
    return ws.reshape(-1), ok, pos_t.astype(I32)


def _select_expert(table, idx_t):
    onehot = idx_t[:, :, None] == jnp.arange(N_EXPERTS, dtype=I32)
    return jnp.sum(jnp.where(onehot, table, 0), axis=-1)


def _group_layout(counts, n_blocks):
    te = EXPERT_TILE
    padded = (counts + te - 1) // te * te
    pends = jnp.cumsum(padded)
    pstart = (pends - padded).astype(I32)
    nused = (pends[-1] // te).astype(I32)
    tail = nused + jnp.arange(N_EXPERTS, dtype=I32)
    zstart = jnp.concatenate([jnp.where(padded > 0, pends - te, -1),
                              jnp.where(tail < n_blocks, tail * te, -1)]).astype(I32)
    blk_src = jnp.minimum(jnp.arange(n_blocks, dtype=I32), jnp.maximum(nused - 1, 0))
    blk_e = jnp.sum((blk_src * te)[:, None] >= pends[None, :], axis=1)
    blk_e = jnp.clip(blk_e, 0, N_EXPERTS - 1).astype(I32)
    blk = jnp.arange(n_blocks, dtype=I32)
    prev_e = jnp.concatenate([blk_e[:1], blk_e[:-1]])
    first = ((blk < nused) & ((blk == 0) | (blk_e != prev_e))).astype(I32)
    experts = jnp.arange(N_EXPERTS, dtype=I32)[None, :]
    later = (experts > blk_e[:, None]) & (counts[None, :] > 0)
    next_e = jnp.min(jnp.where(later, experts, N_EXPERTS), axis=1)
    next_e = jnp.where(next_e < N_EXPERTS, next_e, -1).astype(I32)
    overflow = jnp.any(counts > EXPERT_CAP)
    pstart_blk = jnp.sum(jnp.where(blk_e[:, None] == experts, pstart[None, :], 0), axis=1)
    blk_cap = blk_e * (EXPERT_CAP // te) + (blk_src - pstart_blk // te)
    blk_src = jnp.where(overflow, blk_src, blk_cap).astype(I32)
    return (pstart, zstart, blk_e, blk_src, nused.reshape(1), first, next_e,
            overflow.astype(I32).reshape(1))


def kernel(x, p, positions, g_mix_norm, w_in, g_ret_norm, w_pool, pool_scale, w_branch, w_out,
           g_ffn_norm, w_router, b_router, w_gate_up, b_gate_up, w_down, b_down,
           g_ple_norm, w_ple_gate, w_ple_proj, g_final):
    B, S, D = x.shape
    depth = w_in.shape[0]
    T = B * S
    xt = x.reshape(T, D)
    cos, sin = _rope_tables(positions.reshape(T, 1))
    n_blocks = (T * TOP_K) // EXPERT_TILE + N_EXPERTS
    for i in range(depth):
        x1, h2, idx_t, w_t, rank_t, cnt, trun, xs = _token_mix(
            xt, cos, sin, g_mix_norm[i], w_in[i], g_ret_norm[i], w_pool[i], pool_scale[i],
            w_branch[i], w_out[i], g_ffn_norm[i], w_router[i], b_router[i], S)
        (pstart, zstart, blk_e, blk_src, nused, first, next_e,
         overflow) = _group_layout(cnt[:, 0], n_blocks)
        dest_t = _select_expert(pstart[None, None, :], idx_t) + rank_t
        xs = _moe_dispatch_fixup(overflow, zstart, dest_t, h2, xs)
        yb = _moe_experts(xs, blk_e, blk_src, nused, first, next_e,
                          w_gate_up[i], b_gate_up[i], w_down[i], b_down[i])
        assert depth == 1
        assert min(MIX_TILE, S) == min(COMB_TILE, T)
        ws, ok, pos_t = _combine_windows(pstart, trun[:, :, 0], cnt[:, 0], idx_t, dest_t,
                                         min(COMB_TILE, T), n_blocks * EXPERT_TILE)
        xt = _moe_combine(ws, ok, dest_t, pos_t, yb, w_t, x1, p.reshape(depth * T, -1), i, g_ple_norm[i],
                          w_ple_gate[i], w_ple_proj[i], g_final)
    return xt.reshape(B, S, D)
```
